```python
import jax, jax.numpy as jnp
from jax import lax
import numpy as np

D_MODEL = 2048
BATCH = 4
SEQ = 4096
DEPTH = 2

CTX_LEN = 256
GRID_W = 64
N_HEADS = 16
N_KV_HEADS = 4
HEAD_DIM = 128
ATT_WIDTH = N_HEADS * HEAD_DIM
KV_WIDTH = N_KV_HEADS * HEAD_DIM
Q_BLOCK = 128
ROPE_THETA = 10000.0
RET_HEADS = 8
RET_DK = 128
RET_DV = 256
RET_QK_WIDTH = RET_HEADS * RET_DK
RET_V_WIDTH = RET_HEADS * RET_DV
RET_CHUNK = 128
D_FF = 5632
N_EXPERTS = 8
TOP_K = 2
EXPERT_FF = 7168
N_DENSE = (DEPTH + 1) // 2
N_MOE = DEPTH // 2
IN_SPLITS = (ATT_WIDTH, KV_WIDTH, KV_WIDTH, RET_QK_WIDTH, RET_QK_WIDTH, RET_V_WIDTH, RET_V_WIDTH, RET_V_WIDTH, D_MODEL, D_MODEL)
IN_WIDTH = ATT_WIDTH + 2 * KV_WIDTH + 2 * RET_QK_WIDTH + 3 * RET_V_WIDTH + 2 * D_MODEL
N_MOD = 6
EPS = 1e-6
GN_EPS = 1e-5

kernel_name = 'hybrid_gqa_retention_moe_dit'


def rms_norm(x, g):
    xf = x.astype(jnp.float32)
    y = xf * lax.rsqrt(jnp.mean(xf * xf, axis=-1, keepdims=True) + EPS)
    return (y * g.astype(jnp.float32)).astype(x.dtype)


def modulate(h, shift, scale):
    return h * (1 + scale) + shift


def adaln(cvec, w, b):
    m = jax.nn.silu(cvec) @ w + b
    return jnp.split(m[:, None, :], N_MOD, axis=-1)


def rope(x, ang):
    half = x.shape[-1] // 2
    xf = x.astype(jnp.float32)
    cos = jnp.cos(ang)[None, :, None, :]
    sin = jnp.sin(ang)[None, :, None, :]
    x1, x2 = xf[..., :half], xf[..., half:]
    return jnp.concatenate([x1 * cos - x2 * sin, x1 * sin + x2 * cos], axis=-1).astype(x.dtype)


def axial_angles(n_tokens):
    rows = n_tokens // GRID_W
    row = jnp.repeat(jnp.arange(rows), GRID_W).astype(jnp.float32)
    col = jnp.tile(jnp.arange(GRID_W), rows).astype(jnp.float32)
    n_freq = HEAD_DIM // 4
    inv = ROPE_THETA ** (-jnp.arange(n_freq, dtype=jnp.float32) / n_freq)
    return jnp.concatenate([row[:, None] * inv, col[:, None] * inv], axis=-1)


def linear_angles(pos, dim):
    n_freq = dim // 2
    inv = ROPE_THETA ** (-jnp.arange(n_freq, dtype=jnp.float32) / n_freq)
    return pos.astype(jnp.float32)[:, None] * inv


def stream_projections(h, w_in, q_g, k_g, att_ang, ret_ang):
    B, N, _ = h.shape
    split_points = [int(i) for i in np.cumsum(IN_SPLITS)[:-1]]
    q, k, v, rq, rk, rv, rg_f, rg_b, g_att, g_ret = jnp.split(h @ w_in, split_points, axis=-1)
    q = rms_norm(q.reshape(B, N, N_HEADS, HEAD_DIM), q_g)
    k = rms_norm(k.reshape(B, N, N_KV_HEADS, HEAD_DIM), k_g)
    v = v.reshape(B, N, N_KV_HEADS, HEAD_DIM)
    if att_ang is not None:
        q = rope(q, att_ang)
        k = rope(k, att_ang)
    rq = rope(rq.reshape(B, N, RET_HEADS, RET_DK), ret_ang).astype(jnp.float32).transpose(0, 2, 1, 3)
    rk = (rope(rk.reshape(B, N, RET_HEADS, RET_DK), ret_ang).astype(jnp.float32) * (RET_DK ** -0.5)).transpose(0, 2, 1, 3)
    rv = rv.reshape(B, N, RET_HEADS, RET_DV).astype(jnp.float32).transpose(0, 2, 1, 3)
    return q, k, v, rq, rk, rv, rg_f, rg_b, g_att, g_ret


def blocked_attention(q, k, v):
    B, Nq, H, hd = q.shape
    G = H // N_KV_HEADS
    nb = Nq // Q_BLOCK
    qb = q.reshape(B, nb, Q_BLOCK, N_KV_HEADS, G, hd).transpose(1, 0, 2, 3, 4, 5)
    scale = hd ** -0.5

    def one_block(qi):
        s = jnp.einsum('bqhgd,bkhd->bhgqk', qi, k).astype(jnp.float32) * scale
        p = jax.nn.softmax(s, axis=-1).astype(v.dtype)
        return jnp.einsum('bhgqk,bkhd->bqhgd', p, v)

    o = lax.map(one_block, qb)
    return o.transpose(1, 0, 2, 3, 4, 5).reshape(B, Nq, H * hd)


def retention_scan(q, k, v, log_gamma, state0):
    B, H, N, _ = q.shape
    dv = v.shape[-1]
    L = RET_CHUNK
    nc = N // L
    pos = jnp.arange(L, dtype=jnp.float32)
    diff = pos[:, None] - pos[None, :]
    decay = jnp.where(diff >= 0, jnp.exp(jnp.maximum(diff, 0.0) * log_gamma[:, None, None]), 0.0)
    xi = jnp.exp((pos + 1.0) * log_gamma[:, None])[:, :, None]
    zeta = jnp.exp((L - 1.0 - pos) * log_gamma[:, None])[:, :, None]
    chunk_decay = jnp.exp(L * log_gamma)[:, None, None]

    def to_chunks(t):
        return t.reshape(B, H, nc, L, t.shape[-1]).transpose(2, 0, 1, 3, 4)

    def step(R, inp):
        qc, kc, vc = inp
        inner = jnp.einsum('bhnd,bhmd->bhnm', qc, kc) * decay
        o = jnp.einsum('bhnm,bhme->bhne', inner, vc) + jnp.einsum('bhnd,bhde->bhne', qc, R) * xi
        R = R * chunk_decay + jnp.einsum('bhmd,bhme->bhde', kc * zeta, vc)
        return R, o

    R, o = lax.scan(step, state0, (to_chunks(q), to_chunks(k), to_chunks(v)))
    return R, o.transpose(1, 2, 0, 3, 4).reshape(B, H, N, dv)


def bidir_retention(q, k, v, qc, kc, vc, lg_f, lg_b):
    B, H, _, dk = q.shape
    dv = v.shape[-1]
    zero = jnp.zeros((B, H, dk, dv), jnp.float32)
    flip = lambda t: jnp.flip(t, axis=2)
    s_f, oc_f = retention_scan(qc, kc, vc, lg_f, zero)
    _, o_f = retention_scan(q, k, v, lg_f, s_f)
    s_b, oc_b = retention_scan(flip(qc), flip(kc), flip(vc), lg_b, zero)
    _, o_b = retention_scan(flip(q), flip(k), flip(v), lg_b, s_b)
    return o_f, flip(o_b), oc_f, flip(oc_b)


def head_group_norm(o):
    B, H, N, dv = o.shape
    mu = jnp.mean(o, axis=-1, keepdims=True)
    var = jnp.mean(jnp.square(o - mu), axis=-1, keepdims=True)
    y = (o - mu) * lax.rsqrt(var + GN_EPS)
    return y.transpose(0, 2, 1, 3).reshape(B, N, H * dv)


def retention_merge(o_f, o_b, g_f, g_b, dtype):
    y = jax.nn.silu(g_f.astype(jnp.float32)) * head_group_norm(o_f) + jax.nn.silu(g_b.astype(jnp.float32)) * head_group_norm(o_b)
    return y.astype(dtype)


def branch_merge(att, ret, g_att, g_ret, w_att_o, w_ret_o, w_out):
    z = jax.nn.sigmoid(g_att) * (att @ w_att_o) + jax.nn.sigmoid(g_ret) * (ret @ w_ret_o)
    return z @ w_out


def swiglu(t, w_gate, w_up, w_down):
    return (jax.nn.silu(t @ w_gate) * (t @ w_up)) @ w_down


def moe_swiglu(h, w_router, w_gate, w_up, w_down):
    B, N, D = h.shape
    t = h.reshape(B * N, D)
    logits = (t @ w_router).astype(jnp.float32)
    top_vals, top_idx = lax.top_k(logits, TOP_K)
    top_w = jax.nn.softmax(top_vals, axis=-1)
    gates = jnp.sum(jax.nn.one_hot(top_idx, N_EXPERTS, dtype=jnp.float32) * top_w[..., None], axis=1)
    out = jnp.zeros_like(t)
    for e in range(N_EXPERTS):
        out = out + gates[:, e:e + 1].astype(t.dtype) * swiglu(t, w_gate[e], w_up[e], w_down[e])
    return out.reshape(B, N, D)


def setup_inputs(seed: int = 0) -> dict:
    key = jax.random.key(seed)
    ks = jax.random.split(key, 24)
    f32 = jnp.float32
    D = D_MODEL

    def nrm(k, shape, fan_in):
        return jax.random.normal(k, shape, f32) * (fan_in ** -0.5)

    def gain(k, shape):
        return 1.0 + 0.02 * jax.random.normal(k, shape, f32)

    h = jnp.arange(RET_HEADS, dtype=f32)
    gamma = 1.0 - 2.0 ** (-5.0 - h)
    decay_logit = jnp.log(gamma) - jnp.log1p(-gamma)
    return {
        'x': jax.random.normal(ks[0], (BATCH, SEQ, D), f32),
        'c': jax.random.normal(ks[1], (BATCH, D), f32),
        'ctx': jax.random.normal(ks[2], (BATCH, CTX_LEN, D), f32),
        'c_ctx': jax.random.normal(ks[3], (D,), f32),
        'w_ada': nrm(ks[4], (DEPTH, D, N_MOD * D), D),
        'b_ada': 0.02 * jax.random.normal(ks[5], (DEPTH, N_MOD * D), f32),
        'norm1_g': gain(ks[6], (DEPTH, D)),
        'norm2_g': gain(ks[7], (DEPTH, D)),
        'w_in': nrm(ks[8], (DEPTH, D, IN_WIDTH), D),
        'q_norm_g': gain(ks[9], (DEPTH, HEAD_DIM)),
        'k_norm_g': gain(ks[10], (DEPTH, HEAD_DIM)),
        'ret_decay_f': decay_logit + 0.05 * jax.random.normal(ks[11], (DEPTH, RET_HEADS), f32),
        'ret_decay_b': decay_logit + 0.05 * jax.random.normal(ks[12], (DEPTH, RET_HEADS), f32),
        'w_att_o': nrm(ks[13], (DEPTH, ATT_WIDTH, D), ATT_WIDTH),
        'w_ret_o': nrm(ks[14], (DEPTH, RET_V_WIDTH, D), RET_V_WIDTH),
        'w_out': nrm(ks[15], (DEPTH, D, D), D),
        'ffn_w_gate': nrm(ks[16], (N_DENSE, D, D_FF), D),
        'ffn_w_up': nrm(ks[17], (N_DENSE, D, D_FF), D),
        'ffn_w_down': nrm(ks[18], (N_DENSE, D_FF, D), D_FF),
        'moe_router': nrm(ks[19], (N_MOE, D, N_EXPERTS), D),
        'moe_w_gate': nrm(ks[20], (N_MOE, N_EXPERTS, D, EXPERT_FF), D),
        'moe_w_up': nrm(ks[21], (N_MOE, N_EXPERTS, D, EXPERT_FF), D),
        'moe_w_down': nrm(ks[22], (N_MOE, N_EXPERTS, EXPERT_FF, D), EXPERT_FF),
        'final_g': gain(ks[23], (D,)),
    }


def reference(x, c, ctx, c_ctx, w_ada, b_ada, norm1_g, norm2_g, w_in, q_norm_g, k_norm_g, ret_decay_f, ret_decay_b, w_att_o, w_ret_o, w_out, ffn_w_gate, ffn_w_up, ffn_w_down, moe_router, moe_w_gate, moe_w_up, moe_w_down, final_g):
    S = x.shape[1]
    C = ctx.shape[1]
    ang_att = axial_angles(S)
    ang_ret_ctx = linear_angles(jnp.arange(C), RET_DK)
    ang_ret_lat = linear_angles(C + jnp.arange(S), RET_DK)
    xc = ctx
    for layer in range(DEPTH):
        last = layer == DEPTH - 1
        sh1, sc1, g1, sh2, sc2, g2 = adaln(c, w_ada[layer], b_ada[layer])
        sh1c, sc1c, g1c, sh2c, sc2c, g2c = adaln(c_ctx[None, :], w_ada[layer], b_ada[layer])

        h = modulate(rms_norm(x, norm1_g[layer]), sh1, sc1)
        hc = modulate(rms_norm(xc, norm1_g[layer]), sh1c, sc1c)
        q, k, v, rq, rk, rv, rgf, rgb, ga, gr = stream_projections(h, w_in[layer], q_norm_g[layer], k_norm_g[layer], ang_att, ang_ret_lat)
        qc, kc, vc, rqc, rkc, rvc, rgfc, rgbc, gac, grc = stream_projections(hc, w_in[layer], q_norm_g[layer], k_norm_g[layer], None, ang_ret_ctx)

        att = blocked_attention(q, jnp.concatenate([k, kc], axis=1), jnp.concatenate([v, vc], axis=1))
        lg_f = jax.nn.log_sigmoid(ret_decay_f[layer].astype(jnp.float32))
        lg_b = jax.nn.log_sigmoid(ret_decay_b[layer].astype(jnp.float32))
        o_f, o_b, oc_f, oc_b = bidir_retention(rq, rk, rv, rqc, rkc, rvc, lg_f, lg_b)
        ret = retention_merge(o_f, o_b, rgf, rgb, x.dtype)
        x = x + g1 * branch_merge(att, ret, ga, gr, w_att_o[layer], w_ret_o[layer], w_out[layer])
        if not last:
            att_c = blocked_attention(qc, kc, vc)
            ret_c = retention_merge(oc_f, oc_b, rgfc, rgbc, xc.dtype)
            xc = xc + g1c * branch_merge(att_c, ret_c, gac, grc, w_att_o[layer], w_ret_o[layer], w_out[layer])

        j = layer // 2
        if layer % 2 == 0:
            x = x + g2 * swiglu(modulate(rms_norm(x, norm2_g[layer]), sh2, sc2), ffn_w_gate[j], ffn_w_up[j], ffn_w_down[j])
            if not last:
                xc = xc + g2c * swiglu(modulate(rms_norm(xc, norm2_g[layer]), sh2c, sc2c), ffn_w_gate[j], ffn_w_up[j], ffn_w_down[j])
        else:
            x = x + g2 * moe_swiglu(modulate(rms_norm(x, norm2_g[layer]), sh2, sc2), moe_router[j], moe_w_gate[j], moe_w_up[j], moe_w_down[j])
            if not last:
                xc = xc + g2c * moe_swiglu(modulate(rms_norm(xc, norm2_g[layer]), sh2c, sc2c), moe_router[j], moe_w_gate[j], moe_w_up[j], moe_w_down[j])
    return rms_norm(x, final_g)
```

```python
import functools

import jax
import jax.numpy as jnp
import numpy as np
from jax import lax
from jax.experimental import pallas as pl
from jax.experimental.pallas import tpu as pltpu

F32 = jnp.float32
BF16 = jnp.bfloat16

GRID_W = 64
N_HEADS = 16
N_KV_HEADS = 4
HEAD_DIM = 128
KV_GROUP = N_HEADS // N_KV_HEADS
ATT_WIDTH = N_HEADS * HEAD_DIM
KV_WIDTH = N_KV_HEADS * HEAD_DIM
ROPE_THETA = 10000.0
RET_HEADS = 8
RET_DK = 128
RET_DV = 256
RET_QK_WIDTH = RET_HEADS * RET_DK
RET_V_WIDTH = RET_HEADS * RET_DV
RET_CHUNK = 128
N_EXPERTS = 8
N_MOD = 6
EPS = 1e-6
GN_EPS = 1e-5

Q_TILE = 256
LANES = 128
VMEM_LIMIT = 56 * 2**20


def _cparams(*sem):
    return pltpu.CompilerParams(dimension_semantics=sem, vmem_limit_bytes=VMEM_LIMIT)


def _silu(v):
    return v * jax.nn.sigmoid(v)


def _adaln_kernel(c_ref, w_ref, b_ref, o_ref):
    s = _silu(c_ref[...]).astype(BF16)
    o_ref[...] = jnp.dot(s, w_ref[...].astype(BF16), preferred_element_type=F32) + b_ref[...]


def adaln_all(cc, w_ada, b_ada):
    depth, d, n = w_ada.shape
    r = cc.shape[0]
    tn = 1024 if n % 1024 == 0 else n
    return pl.pallas_call(
        _adaln_kernel,
        out_shape=jax.ShapeDtypeStruct((depth, r, n), F32),
        grid=(depth, n // tn),
        in_specs=[
            pl.BlockSpec((r, d), lambda l, j: (0, 0)),
            pl.BlockSpec((None, d, tn), lambda l, j: (l, 0, j)),
            pl.BlockSpec((None, 1, tn), lambda l, j: (l, 0, j)),
        ],
        out_specs=pl.BlockSpec((None, r, tn), lambda l, j: (l, 0, j)),
        compiler_params=_cparams("parallel", "parallel"),
        name="adaln",
    )(cc, w_ada, b_ada.reshape(depth, 1, n))


def _ctx_rows(i, tm, tiles_per_batch, seq):
    rows = lax.broadcasted_iota(jnp.int32, (tm, 1), 0) + (i % tiles_per_batch) * tm
    return rows >= seq


def _rms(x, g):
    return x * lax.rsqrt(jnp.mean(x * x, axis=-1, keepdims=True) + EPS) * g


def _norm_mod_kernel(x_ref, g_ref, scb_ref, scc_ref, shb_ref, shc_ref, o_ref, *, tm, tpb, seq):
    i = pl.program_id(0)
    is_ctx = _ctx_rows(i, tm, tpb, seq)
    y = _rms(x_ref[...], g_ref[...])
    scale = jnp.where(is_ctx, scc_ref[...], scb_ref[...])
    shift = jnp.where(is_ctx, shc_ref[...], shb_ref[...])
    o_ref[...] = (y * (1 + scale) + shift).astype(o_ref.dtype)


def _norm_mod_router_kernel(x_ref, g_ref, scb_ref, scc_ref, shb_ref, shc_ref, wr_ref, o_ref, gate_ref, *, tm, tpb, seq):
    is_ctx = _ctx_rows(pl.program_id(0), tm, tpb, seq)
    y = _rms(x_ref[...], g_ref[...])
    scale = jnp.where(is_ctx, scc_ref[...], scb_ref[...])
    shift = jnp.where(is_ctx, shc_ref[...], shb_ref[...])
    t = y * (1 + scale) + shift
    o_ref[...] = t.astype(o_ref.dtype)
    logits = jnp.dot(t, wr_ref[...], preferred_element_type=F32, precision=lax.Precision.HIGHEST)
    lane = lax.broadcasted_iota(jnp.int32, logits.shape, 1)
    neg = jnp.float32(-jnp.inf)
    logits = jnp.where(lane < N_EXPERTS, logits, neg)
    m1 = jnp.max(logits, axis=-1, keepdims=True)
    i1 = jnp.min(jnp.where(logits == m1, lane, LANES), axis=-1, keepdims=True)
    rest = jnp.where(lane == i1, neg, logits)
    m2 = jnp.max(rest, axis=-1, keepdims=True)
    i2 = jnp.min(jnp.where(rest == m2, lane, LANES), axis=-1, keepdims=True)
    e2 = jnp.exp(m2 - m1)
    w1 = 1.0 / (1.0 + e2)
    w2 = e2 / (1.0 + e2)
    gate_ref[...] = jnp.where(lane == i1, w1, 0.0) + jnp.where(lane == i2, w2, 0.0)


def _final_norm_kernel(x_ref, g_ref, o_ref):
    o_ref[...] = _rms(x_ref[...], g_ref[...])


def _mm_kernel(x_ref, w_ref, *rest, epilogue):
    o_ref = rest[-1]
    acc = jnp.dot(x_ref[...], w_ref[...], preferred_element_type=F32)
    o_ref[...] = epilogue(acc, *[r[...] for r in rest[:-1]]).astype(o_ref.dtype)


def _rope(y, cos, sin):
    return y * cos + pltpu.roll(y, HEAD_DIM // 2, 1) * sin


def _ep_plain(acc):
    return acc


def _ep_sigmoid(acc):
    return jax.nn.sigmoid(acc)


def _ep_head_norm_rope(acc, g, cos, sin):
    outs = []
    for h in range(acc.shape[1] // HEAD_DIM):
        y = _rms(acc[:, h * HEAD_DIM:(h + 1) * HEAD_DIM], g)
        outs.append(_rope(y, cos, sin))
    return jnp.concatenate(outs, axis=1)


def _ep_rope(acc, cos, sin, *, scale):
    outs = []
    for h in range(acc.shape[1] // RET_DK):
        y = _rope(acc[:, h * RET_DK:(h + 1) * RET_DK], cos, sin)
        outs.append(y * scale if scale != 1.0 else y)
    return jnp.concatenate(outs, axis=1)


def proj(h, w, col0, ncols, out_dtype, epilogue, extras=(), *, tm, tn=512, name):
    m, k = h.shape
    assert col0 % tn == 0 and ncols % tn == 0 and m % tm == 0
    jb = col0 // tn
    in_specs = [
        pl.BlockSpec((tm, k), lambda i, j: (i, 0)),
        pl.BlockSpec((k, tn), lambda i, j: (0, jb + j)),
    ]
    args = [h, w]
    for arr, bs, im in extras:
        in_specs.append(pl.BlockSpec(bs, im))
        args.append(arr)
    return pl.pallas_call(
        functools.partial(_mm_kernel, epilogue=epilogue),
        out_shape=jax.ShapeDtypeStruct((m, ncols), out_dtype),
        grid=(m // tm, ncols // tn),
        in_specs=in_specs,
        out_specs=pl.BlockSpec((tm, tn), lambda i, j: (i, j)),
        compiler_params=_cparams("parallel", "parallel"),
        name=name,
    )(*args)


def _attn_kernel(q_ref, k_ref, v_ref, o_ref, *, seq, n_lat_tiles):
    qi = pl.program_id(2)
    scale = HEAD_DIM ** -0.5

    def run(k, v):
        for g in range(KV_GROUP):
            q = q_ref[:, g * HEAD_DIM:(g + 1) * HEAD_DIM]
            s = lax.dot_general(q, k, (((1,), (1,)), ((), ())), preferred_element_type=F32) * scale
            m = jnp.max(s, axis=-1, keepdims=True)
            p = jnp.exp(s - m)
            l = jnp.sum(p, axis=-1, keepdims=True)
            o = jnp.dot(p.astype(BF16), v, preferred_element_type=F32) / l
            o_ref[:, g * HEAD_DIM:(g + 1) * HEAD_DIM] = o.astype(o_ref.dtype)

    @pl.when(qi < n_lat_tiles)
    def _():
        run(k_ref[...], v_ref[...])

    @pl.when(qi >= n_lat_tiles)
    def _():
        run(k_ref[seq:, :], v_ref[seq:, :])


def attention(q, k, v, *, seq, with_ctx):
    b, nt, _ = q.shape
    n_lat = seq // Q_TILE
    n_tiles = nt // Q_TILE if with_ctx else n_lat
    gw = KV_GROUP * HEAD_DIM
    return pl.pallas_call(
        functools.partial(_attn_kernel, seq=seq, n_lat_tiles=n_lat),
        out_shape=jax.ShapeDtypeStruct((b, nt, ATT_WIDTH), BF16),
        grid=(b, N_KV_HEADS, n_tiles),
        in_specs=[
            pl.BlockSpec((None, Q_TILE, gw), lambda bi, h, i: (bi, i, h)),
            pl.BlockSpec((None, nt, HEAD_DIM), lambda bi, h, i: (bi, 0, h)),
            pl.BlockSpec((None, nt, HEAD_DIM), lambda bi, h, i: (bi, 0, h)),
        ],
        out_specs=pl.BlockSpec((None, Q_TILE, gw), lambda bi, h, i: (bi, i, h)),
        compiler_params=_cparams("parallel", "parallel", "arbitrary"),
        name="attention",
    )(q, k, v)


def _retention_kernel(*refs, has_prev):
    if has_prev:
        q_ref, k_ref, v_ref, g_ref, dec_ref, xi_ref, zeta_ref, cd_ref, prev_ref, o_ref, r_ref = refs
    else:
        q_ref, k_ref, v_ref, g_ref, dec_ref, xi_ref, zeta_ref, cd_ref, o_ref, r_ref = refs

    @pl.when(pl.program_id(1) == 0)
    def _():
        r_ref[...] = jnp.zeros_like(r_ref)

    for h in range(RET_HEADS):
        ks = slice(h * RET_DK, (h + 1) * RET_DK)
        vs = slice(h * RET_DV, (h + 1) * RET_DV)
        q = q_ref[:, ks]
        k = k_ref[:, ks]
        v = v_ref[:, vs]
        state = r_ref[h]
        inner = lax.dot_general(q, k, (((1,), (1,)), ((), ())), preferred_element_type=F32) * dec_ref[h]
        o = jnp.dot(inner.astype(BF16), v, preferred_element_type=F32)
        o = o + jnp.dot(q, state.astype(BF16), preferred_element_type=F32) * xi_ref[h]
        kz = (k.astype(F32) * zeta_ref[h]).astype(BF16)
        r_ref[h] = state * cd_ref[h] + lax.dot_general(kz, v, (((0,), (0,)), ((), ())), preferred_element_type=F32)
        mu = jnp.mean(o, axis=-1, keepdims=True)
        d = o - mu
        var = jnp.mean(d * d, axis=-1, keepdims=True)
        y = _silu(g_ref[:, vs]) * (d * lax.rsqrt(var + GN_EPS))
        if has_prev:
            y = prev_ref[:, vs] + y
        o_ref[:, vs] = y.astype(o_ref.dtype)


def retention_dir(rq, rk, rv, rg, gate_col, tables, chunk_of, prev, out_dtype, *, name):
    b, nt, _ = rq.shape
    nc = nt // RET_CHUNK
    L = RET_CHUNK
    tok = lambda w: pl.BlockSpec((None, L, w), lambda bi, t: (bi, chunk_of(t), 0))
    full = lambda a: pl.BlockSpec(a.shape, lambda bi, t: (0,) * a.ndim)
    in_specs = [tok(RET_QK_WIDTH), tok(RET_QK_WIDTH), tok(RET_V_WIDTH),
                pl.BlockSpec((None, L, RET_V_WIDTH), lambda bi, t: (bi, chunk_of(t), gate_col))]
    in_specs += [full(a) for a in tables]
    args = [rq, rk, rv, rg, *tables]
    if prev is not None:
        in_specs.append(tok(RET_V_WIDTH))
        args.append(prev)
    return pl.pallas_call(
        functools.partial(_retention_kernel, has_prev=prev is not None),
        out_shape=jax.ShapeDtypeStruct((b, nt, RET_V_WIDTH), out_dtype),
        grid=(b, nc),
        in_specs=in_specs,
        out_specs=tok(RET_V_WIDTH),
        scratch_shapes=[pltpu.VMEM((RET_HEADS, RET_DK, RET_DV), F32)],
        compiler_params=_cparams("parallel", "arbitrary"),
        name=name,
    )(*args)


def retention_tables(decay_logit, backward):
    L = RET_CHUNK
    lg = jax.nn.log_sigmoid(decay_logit.astype(F32))
    pos = jnp.arange(L, dtype=F32)
    diff = pos[:, None] - pos[None, :]
    if backward:
        diff = -diff
        xi_e = L - pos
        zeta_e = pos
    else:
        xi_e = pos + 1.0
        zeta_e = L - 1.0 - pos
    decay = jnp.where(diff >= 0, jnp.exp(jnp.maximum(diff, 0.0) * lg[:, None, None]), 0.0)
    xi = jnp.broadcast_to(jnp.exp(xi_e * lg[:, None])[:, :, None], (RET_HEADS, L, RET_DV))
    zeta = jnp.broadcast_to(jnp.exp(zeta_e * lg[:, None])[:, :, None], (RET_HEADS, L, RET_DK))
    cd = jnp.broadcast_to(jnp.exp(L * lg)[:, None, None], (RET_HEADS, 1, RET_DV))
    return decay, xi, zeta, cd


def _merge_kernel(a_ref, r_ref, wa_ref, wr_ref, ga_ref, gr_ref, o_ref):
    za = jnp.dot(a_ref[...], wa_ref[...], preferred_element_type=F32)
    zr = jnp.dot(r_ref[...], wr_ref[...], preferred_element_type=F32)
    o_ref[...] = (ga_ref[...] * za + gr_ref[...] * zr).astype(o_ref.dtype)


def branch_merge(att, ret, w_att_o, w_ret_o, gsig, *, tm, tn=512):
    m, ka = att.shape
    kr = ret.shape[1]
    d = w_att_o.shape[1]
    nj = d // tn
    return pl.pallas_call(
        _merge_kernel,
        out_shape=jax.ShapeDtypeStruct((m, d), BF16),
        grid=(m // tm, nj),
        in_specs=[
            pl.BlockSpec((tm, ka), lambda i, j: (i, 0)),
            pl.BlockSpec((tm, kr), lambda i, j: (i, 0)),
            pl.BlockSpec((ka, tn), lambda i, j: (0, j)),
            pl.BlockSpec((kr, tn), lambda i, j: (0, j)),
            pl.BlockSpec((tm, tn), lambda i, j: (i, j)),
            pl.BlockSpec((tm, tn), lambda i, j: (i, nj + j)),
        ],
        out_specs=pl.BlockSpec((tm, tn), lambda i, j: (i, j)),
        compiler_params=_cparams("parallel", "parallel"),
        name="branch_merge",
    )(att, ret, w_att_o, w_ret_o, gsig, gsig)


def _resid_mm_kernel(z_ref, w_ref, x_ref, gb_ref, gc_ref, o_ref, *, tm, tpb, seq):
    i = pl.program_id(0)
    gate = jnp.where(_ctx_rows(i, tm, tpb, seq), gc_ref[...], gb_ref[...])
    o_ref[...] = x_ref[...] + gate * jnp.dot(z_ref[...], w_ref[...], preferred_element_type=F32)


def _mod_specs(vec, tn, tpb, ctx_row, two_d_grid=True):
    if two_d_grid:
        return (pl.BlockSpec((None, 1, tn), lambda i, j: (i // tpb, 0, j)),
                pl.BlockSpec((None, 1, tn), lambda i, j: (ctx_row, 0, j)))
    return (pl.BlockSpec((None, 1, tn), lambda i: (i // tpb, 0, 0)),
            pl.BlockSpec((None, 1, tn), lambda i: (ctx_row, 0, 0)))


def out_proj_residual(z, w_out, x, gate, *, tm, tpb, seq, ctx_row, tn=512):
    m, k = z.shape
    d = w_out.shape[1]
    sb, sc = _mod_specs(gate, tn, tpb, ctx_row)
    return pl.pallas_call(
        functools.partial(_resid_mm_kernel, tm=tm, tpb=tpb, seq=seq),
        out_shape=jax.ShapeDtypeStruct((m, d), F32),
        grid=(m // tm, d // tn),
        in_specs=[
            pl.BlockSpec((tm, k), lambda i, j: (i, 0)),
            pl.BlockSpec((k, tn), lambda i, j: (0, j)),
            pl.BlockSpec((tm, tn), lambda i, j: (i, j)),
            sb, sc,
        ],
        out_specs=pl.BlockSpec((tm, tn), lambda i, j: (i, j)),
        compiler_params=_cparams("parallel", "parallel"),
        name="out_proj",
    )(z, w_out, x, gate, gate)


def _ffn_kernel(*refs, tm, tpb, seq, moe):
    if moe:
        h_ref, wg_ref, wu_ref, wd_ref, eg_ref, x_ref, gb_ref, gc_ref, o_ref, acc_ref = refs
    else:
        h_ref, wg_ref, wu_ref, wd_ref, x_ref, gb_ref, gc_ref, o_ref, acc_ref = refs
    i, e, f = pl.program_id(0), pl.program_id(1), pl.program_id(2)

    @pl.when((e == 0) & (f == 0))
    def _():
        acc_ref[...] = jnp.zeros_like(acc_ref)

    h = h_ref[...]
    g = jnp.dot(h, wg_ref[...], preferred_element_type=F32)
    u = jnp.dot(h, wu_ref[...], preferred_element_type=F32)
    a = (_silu(g) * u).astype(BF16)
    y = jnp.dot(a, wd_ref[...], preferred_element_type=F32)
    if moe:
        y = eg_ref[...] * y
    acc_ref[...] += y

    @pl.when((e == pl.num_programs(1) - 1) & (f == pl.num_programs(2) - 1))
    def _():
        gate = jnp.where(_ctx_rows(i, tm, tpb, seq), gc_ref[...], gb_ref[...])
        o_ref[...] = x_ref[...] + gate * acc_ref[...]


def ffn_residual(h, wg, wu, wd, x, gate, expert_gates, *, tm, tpb, seq, ctx_row, tf=512):
    m, d = h.shape
    ne, _, ff = wg.shape
    moe = expert_gates is not None
    sb, sc = (pl.BlockSpec((None, 1, d), lambda i, e, f: (i // tpb, 0, 0)),
              pl.BlockSpec((None, 1, d), lambda i, e, f: (ctx_row, 0, 0)))
    in_specs = [
        pl.BlockSpec((tm, d), lambda i, e, f: (i, 0)),
        pl.BlockSpec((None, d, tf), lambda i, e, f: (e, 0, f)),
        pl.BlockSpec((None, d, tf), lambda i, e, f: (e, 0, f)),
        pl.BlockSpec((None, tf, d), lambda i, e, f: (e, f, 0)),
    ]
    args = [h, wg, wu, wd]
    if moe:
        in_specs.append(pl.BlockSpec((None, tm, 1), lambda i, e, f: (e, i, 0)))
        args.append(expert_gates)
    in_specs += [pl.BlockSpec((tm, d), lambda i, e, f: (i, 0)), sb, sc]
    args += [x, gate, gate]
    return pl.pallas_call(
        functools.partial(_ffn_kernel, tm=tm, tpb=tpb, seq=seq, moe=moe),
        out_shape=jax.ShapeDtypeStruct((m, d), F32),
        grid=(m // tm, ne, ff // tf),
        in_specs=in_specs,
        out_specs=pl.BlockSpec((tm, d), lambda i, e, f: (i, 0)),
        scratch_shapes=[pltpu.VMEM((tm, d), F32)],
        compiler_params=_cparams("parallel", "arbitrary", "arbitrary"),
        name="moe_ffn" if moe else "ffn",
    )(*args)


def _rope_tables(ang):
    c, s = jnp.cos(ang), jnp.sin(ang)
    return jnp.concatenate([c, c], axis=-1), jnp.concatenate([-s, s], axis=-1)


def _axial_angles(seq):
    rows = seq // GRID_W
    row = jnp.repeat(jnp.arange(rows), GRID_W).astype(F32)
    col = jnp.tile(jnp.arange(GRID_W), rows).astype(F32)
    n_freq = HEAD_DIM // 4
    inv = ROPE_THETA ** (-jnp.arange(n_freq, dtype=F32) / n_freq)
    return jnp.concatenate([row[:, None] * inv, col[:, None] * inv], axis=-1)


def _linear_angles(pos, dim):
    n_freq = dim // 2
    inv = ROPE_THETA ** (-jnp.arange(n_freq, dtype=F32) / n_freq)
    return pos.astype(F32)[:, None] * inv


def kernel(x, c, ctx, c_ctx, w_ada, b_ada, norm1_g, norm2_g, w_in, q_norm_g, k_norm_g, ret_decay_f, ret_decay_b, w_att_o, w_ret_o, w_out, ffn_w_gate, ffn_w_up, ffn_w_down, moe_router, moe_w_gate, moe_w_up, moe_w_down, final_g):
    b, seq, d = x.shape
    ctx_len = ctx.shape[1]
    depth = w_ada.shape[0]
    assert ctx_len == Q_TILE and seq % Q_TILE == 0 and seq % GRID_W == 0
    nt = seq + ctx_len
    m = b * nt
    tpb_big, tpb_ffn = 4, 8
    tm_big, tm_ffn = nt // tpb_big, nt // tpb_ffn
    tm_norm, tpb_norm = Q_TILE, nt // Q_TILE
    ctx_row = b
    nchunks = nt // RET_CHUNK
    nc_lat = seq // RET_CHUNK

    cos_a, sin_a = _rope_tables(_axial_angles(seq))
    cos_a = jnp.concatenate([cos_a, jnp.ones((ctx_len, HEAD_DIM), F32)], axis=0)
    sin_a = jnp.concatenate([sin_a, jnp.zeros((ctx_len, HEAD_DIM), F32)], axis=0)
    pos = jnp.concatenate([ctx_len + jnp.arange(seq), jnp.arange(ctx_len)])
    cos_r, sin_r = _rope_tables(_linear_angles(pos, RET_DK))

    n_rows = -(-(b + 1) // 16) * 16
    cc = jnp.zeros((n_rows, d), F32).at[:b].set(c).at[b].set(c_ctx)
    mods = adaln_all(cc, w_ada, b_ada)
    mods = mods.reshape(depth, n_rows, N_MOD, d).transpose(0, 2, 1, 3).reshape(depth, N_MOD, n_rows, 1, d)

    xa = jnp.concatenate([x, ctx], axis=1).reshape(m, d)

    def norm_mod(xcur, g, sc, sh):
        sb = pl.BlockSpec((None, 1, d), lambda i: (i // tpb_norm, 0, 0))
        scx = pl.BlockSpec((None, 1, d), lambda i: (ctx_row, 0, 0))
        row = pl.BlockSpec((tm_norm, d), lambda i: (i, 0))
        return pl.pallas_call(
            functools.partial(_norm_mod_kernel, tm=tm_norm, tpb=tpb_norm, seq=seq),
            out_shape=jax.ShapeDtypeStruct((m, d), BF16),
            grid=(m // tm_norm,),
            in_specs=[row, pl.BlockSpec((1, d), lambda i: (0, 0)), sb, scx, sb, scx],
            out_specs=row,
            compiler_params=_cparams("parallel"),
            name="norm_mod",
        )(xcur, g.reshape(1, d), sc, sc, sh, sh)

    tok_tab = lambda t: (t, (tm_big, HEAD_DIM), lambda i, j: (i % tpb_big, 0))
    row_vec = lambda v: (v.reshape(1, -1), (1, v.shape[-1]), lambda i, j: (0, 0))
    rope_scaled = functools.partial(_ep_rope, scale=RET_DK ** -0.5)
    rope_plain = functools.partial(_ep_rope, scale=1.0)
    fwd_chunk = lambda t: (t + nc_lat) % nchunks
    bwd_chunk = lambda t: nchunks - 1 - t

    c0 = 0
    c_q, c0 = c0, c0 + ATT_WIDTH
    c_k, c0 = c0, c0 + KV_WIDTH
    c_v, c0 = c0, c0 + KV_WIDTH
    c_rq, c0 = c0, c0 + RET_QK_WIDTH
    c_rk, c0 = c0, c0 + RET_QK_WIDTH
    c_rv, c0 = c0, c0 + RET_V_WIDTH
    c_rg, c0 = c0, c0 + 2 * RET_V_WIDTH
    c_bg, c0 = c0, c0 + 2 * d

    for layer in range(depth):
        last = layer == depth - 1
        sh1, sc1, g1, sh2, sc2, g2 = [mods[layer, n] for n in range(N_MOD)]
        w = w_in[layer].astype(BF16)
        tn_in = 512 if d % 512 == 0 else 256

        h = norm_mod(xa, norm1_g[layer], sc1, sh1)
        pj = functools.partial(proj, h, w, tm=tm_big, tn=tn_in)
        q = pj(c_q, ATT_WIDTH, BF16, _ep_head_norm_rope, [row_vec(q_norm_g[layer]), tok_tab(cos_a), tok_tab(sin_a)], name="proj_q")
        k = pj(c_k, KV_WIDTH, BF16, _ep_head_norm_rope, [row_vec(k_norm_g[layer]), tok_tab(cos_a), tok_tab(sin_a)], name="proj_k")
        v = pj(c_v, KV_WIDTH, BF16, _ep_plain, name="proj_v")
        rq = pj(c_rq, RET_QK_WIDTH, BF16, rope_plain, [tok_tab(cos_r), tok_tab(sin_r)], name="proj_rq")
        rk = pj(c_rk, RET_QK_WIDTH, BF16, rope_scaled, [tok_tab(cos_r), tok_tab(sin_r)], name="proj_rk")
        rv = pj(c_rv, RET_V_WIDTH, BF16, _ep_plain, name="proj_rv")
        rg = pj(c_rg, 2 * RET_V_WIDTH, F32, _ep_plain, name="proj_rg")
        gsig = pj(c_bg, 2 * d, F32, _ep_sigmoid, name="proj_gates")

        r3 = lambda a: a.reshape(b, nt, a.shape[-1])
        att = attention(r3(q), r3(k), r3(v), seq=seq, with_ctx=not last).reshape(m, ATT_WIDTH)
        tf_ = retention_tables(ret_decay_f[layer], backward=False)
        tb_ = retention_tables(ret_decay_b[layer], backward=True)
        yf = retention_dir(r3(rq), r3(rk), r3(rv), r3(rg), 0, tf_, fwd_chunk, None, F32, name="retention_fwd")
        ret = retention_dir(r3(rq), r3(rk), r3(rv), r3(rg), 1, tb_, bwd_chunk, yf, BF16, name="retention_bwd").reshape(m, RET_V_WIDTH)

        z = branch_merge(att, ret, w_att_o[layer].astype(BF16), w_ret_o[layer].astype(BF16), gsig, tm=tm_big, tn=tn_in)
        xa = out_proj_residual(z, w_out[layer].astype(BF16), xa, g1, tm=tm_big, tpb=tpb_big, seq=seq, ctx_row=ctx_row, tn=tn_in)

        j = layer // 2
        if layer % 2 == 0:
            h2 = norm_mod(xa, norm2_g[layer], sc2, sh2)
            xa = ffn_residual(h2, ffn_w_gate[j:j + 1].astype(BF16), ffn_w_up[j:j + 1].astype(BF16), ffn_w_down[j:j + 1].astype(BF16),
                              xa, g2, None, tm=tm_ffn, tpb=tpb_ffn, seq=seq, ctx_row=ctx_row)
        else:
            wr = jnp.zeros((d, LANES), F32).at[:, :N_EXPERTS].set(moe_router[j])
            sb = pl.BlockSpec((None, 1, d), lambda i: (i // tpb_norm, 0, 0))
            scx = pl.BlockSpec((None, 1, d), lambda i: (ctx_row, 0, 0))
            row = pl.BlockSpec((tm_norm, d), lambda i: (i, 0))
            h2, gates = pl.pallas_call(
                functools.partial(_norm_mod_router_kernel, tm=tm_norm, tpb=tpb_norm, seq=seq),
                out_shape=(jax.ShapeDtypeStruct((m, d), BF16), jax.ShapeDtypeStruct((m, LANES), F32)),
                grid=(m // tm_norm,),
                in_specs=[row, pl.BlockSpec((1, d), lambda i: (0, 0)), sb, scx, sb, scx, pl.BlockSpec((d, LANES), lambda i: (0, 0))],
                out_specs=(row, pl.BlockSpec((tm_norm, LANES), lambda i: (i, 0))),
                compiler_params=_cparams("parallel"),
                name="norm_mod_router",
            )(xa, norm2_g[layer].reshape(1, d), sc2, sc2, sh2, sh2, wr)
            eg = gates[:, :N_EXPERTS].T.reshape(N_EXPERTS, m, 1)
            xa = ffn_residual(h2, moe_w_gate[j].astype(BF16), moe_w_up[j].astype(BF16), moe_w_down[j].astype(BF16),
                              xa, g2, eg, tm=tm_ffn, tpb=tpb_ffn, seq=seq, ctx_row=ctx_row)

    tm_f = Q_TILE
    out = pl.pallas_call(
        _final_norm_kernel,
        out_shape=jax.ShapeDtypeStruct((b, seq, d), F32),
        grid=(b, seq // tm_f),
        in_specs=[pl.BlockSpec((None, tm_f, d), lambda bi, i: (bi, i, 0)), pl.BlockSpec((1, d), lambda bi, i: (0, 0))],
        out_specs=pl.BlockSpec((None, tm_f, d), lambda bi, i: (bi, i, 0)),
        compiler_params=_cparams("parallel", "parallel"),
        name="final_norm",
    )(xa.reshape(b, nt, d), final_g.reshape(1, d))
    return out
```

```python
import functools

import jax
import jax.numpy as jnp
import numpy as np
from jax import lax
from jax.experimental import pallas as pl
from jax.experimental.pallas import tpu as pltpu

F32 = jnp.float32
BF16 = jnp.bfloat16

GRID_W = 64
N_HEADS = 16
N_KV_HEADS = 4
HEAD_DIM = 128
KV_GROUP = N_HEADS // N_KV_HEADS
ATT_WIDTH = N_HEADS * HEAD_DIM
KV_WIDTH = N_KV_HEADS * HEAD_DIM
ROPE_THETA = 10000.0
RET_HEADS = 8
RET_DK = 128
RET_DV = 256
RET_QK_WIDTH = RET_HEADS * RET_DK
RET_V_WIDTH = RET_HEADS * RET_DV
RET_CHUNK = 128
N_EXPERTS = 8
N_MOD = 6
EPS = 1e-6
GN_EPS = 1e-5

Q_TILE = 256
LANES = 128
VMEM_LIMIT = 56 * 2**20


def _cparams(*sem):
    return pltpu.CompilerParams(dimension_semantics=sem, vmem_limit_bytes=VMEM_LIMIT)


def _silu(v):
    return v * jax.nn.sigmoid(v)


def _adaln_kernel(c_ref, w_ref, b_ref, o_ref):
    s = _silu(c_ref[...]).astype(BF16)
    o_ref[...] = jnp.dot(s, w_ref[...].astype(BF16), preferred_element_type=F32) + b_ref[...]


def adaln_all(cc, w_ada, b_ada):
    depth, d, n = w_ada.shape
    r = cc.shape[0]
    tn = 1024 if n % 1024 == 0 else n
    return pl.pallas_call(
        _adaln_kernel,
        out_shape=jax.ShapeDtypeStruct((depth, r, n), F32),
        grid=(depth, n // tn),
        in_specs=[
            pl.BlockSpec((r, d), lambda l, j: (0, 0)),
            pl.BlockSpec((None, d, tn), lambda l, j: (l, 0, j)),
            pl.BlockSpec((None, 1, tn), lambda l, j: (l, 0, j)),
        ],
        out_specs=pl.BlockSpec((None, r, tn), lambda l, j: (l, 0, j)),
        compiler_params=_cparams("parallel", "parallel"),
        name="adaln",
    )(cc, w_ada, b_ada.reshape(depth, 1, n))


def _ctx_rows(i, tm, tiles_per_batch, seq):
    rows = lax.broadcasted_iota(jnp.int32, (tm, 1), 0) + (i % tiles_per_batch) * tm
    return rows >= seq


def _rms(x, g):
    return x * lax.rsqrt(jnp.mean(x * x, axis=-1, keepdims=True) + EPS) * g


def _norm_mod_kernel(x_ref, g_ref, scb_ref, scc_ref, shb_ref, shc_ref, o_ref, *, tm, tpb, seq):
    i = pl.program_id(0)
    is_ctx = _ctx_rows(i, tm, tpb, seq)
    y = _rms(x_ref[...], g_ref[...])
    scale = jnp.where(is_ctx, scc_ref[...], scb_ref[...])
    shift = jnp.where(is_ctx, shc_ref[...], shb_ref[...])
    o_ref[...] = (y * (1 + scale) + shift).astype(o_ref.dtype)


def _norm_mod_router_kernel(x_ref, g_ref, scb_ref, scc_ref, shb_ref, shc_ref, wr_ref, o_ref, gate_ref, *, tm, tpb, seq):
    is_ctx = _ctx_rows(pl.program_id(0), tm, tpb, seq)
    y = _rms(x_ref[...], g_ref[...])
    scale = jnp.where(is_ctx, scc_ref[...], scb_ref[...])
    shift = jnp.where(is_ctx, shc_ref[...], shb_ref[...])
    t = y * (1 + scale) + shift
    o_ref[...] = t.astype(o_ref.dtype)
    logits = jnp.dot(t, wr_ref[...], preferred_element_type=F32, precision=lax.Precision.HIGHEST)
    lane = lax.broadcasted_iota(jnp.int32, logits.shape, 1)
    neg = jnp.float32(-jnp.inf)
    logits = jnp.where(lane < N_EXPERTS, logits, neg)
    m1 = jnp.max(logits, axis=-1, keepdims=True)
    i1 = jnp.min(jnp.where(logits == m1, lane, LANES), axis=-1, keepdims=True)
    rest = jnp.where(lane == i1, neg, logits)
    m2 = jnp.max(rest, axis=-1, keepdims=True)
    i2 = jnp.min(jnp.where(rest == m2, lane, LANES), axis=-1, keepdims=True)
    e2 = jnp.exp(m2 - m1)
    w1 = 1.0 / (1.0 + e2)
    w2 = e2 / (1.0 + e2)
    rec = jnp.where(lane == 0, i1.astype(F32), jnp.where(lane == 1, i2.astype(F32), jnp.where(lane == 2, w1, w2)))
    gate_ref[...] = jnp.where(lane < 4, rec, 0.0)


def _final_norm_kernel(x_ref, g_ref, o_ref):
    o_ref[...] = _rms(x_ref[...], g_ref[...])


def _mm_kernel(x_ref, w_ref, *rest, epilogue):
    o_ref = rest[-1]
    acc = jnp.dot(x_ref[...], w_ref[...], preferred_element_type=F32)
    o_ref[...] = epilogue(acc, *[r[...] for r in rest[:-1]]).astype(o_ref.dtype)


def _rope(y, cos, sin):
    return y * cos + pltpu.roll(y, HEAD_DIM // 2, 1) * sin


def _ep_plain(acc):
    return acc


def _ep_sigmoid(acc):
    return jax.nn.sigmoid(acc)


def _ep_head_norm_rope(acc, g, cos, sin):
    outs = []
    for h in range(acc.shape[1] // HEAD_DIM):
        y = _rms(acc[:, h * HEAD_DIM:(h + 1) * HEAD_DIM], g)
        outs.append(_rope(y, cos, sin))
    return jnp.concatenate(outs, axis=1)


def _ep_rope(acc, cos, sin, *, scale):
    outs = []
    for h in range(acc.shape[1] // RET_DK):
        y = _rope(acc[:, h * RET_DK:(h + 1) * RET_DK], cos, sin)
        outs.append(y * scale if scale != 1.0 else y)
    return jnp.concatenate(outs, axis=1)


def proj(h, w, col0, ncols, out_dtype, epilogue, extras=(), *, tm, tn=512, name):
    m, k = h.shape
    assert col0 % tn == 0 and ncols % tn == 0 and m % tm == 0
    jb = col0 // tn
    in_specs = [
        pl.BlockSpec((tm, k), lambda i, j: (i, 0)),
        pl.BlockSpec((k, tn), lambda i, j: (0, jb + j)),
    ]
    args = [h, w]
    for arr, bs, im in extras:
        in_specs.append(pl.BlockSpec(bs, im))
        args.append(arr)
    return pl.pallas_call(
        functools.partial(_mm_kernel, epilogue=epilogue),
        out_shape=jax.ShapeDtypeStruct((m, ncols), out_dtype),
        grid=(m // tm, ncols // tn),
        in_specs=in_specs,
        out_specs=pl.BlockSpec((tm, tn), lambda i, j: (i, j)),
        compiler_params=_cparams("parallel", "parallel"),
        name=name,
    )(*args)


def _attn_kernel(q_ref, k_ref, v_ref, o_ref, *, seq, n_lat_tiles, with_ctx):
    qi = pl.program_id(2)
    scale = HEAD_DIM ** -0.5

    def run(k, v):
        for g in range(KV_GROUP):
            q = q_ref[:, g * HEAD_DIM:(g + 1) * HEAD_DIM]
            s = lax.dot_general(q, k, (((1,), (1,)), ((), ())), preferred_element_type=F32) * scale
            m = jnp.max(s, axis=-1, keepdims=True)
            p = jnp.exp(s - m)
            l = jnp.sum(p, axis=-1, keepdims=True)
            o = jnp.dot(p.astype(BF16), v, preferred_element_type=F32) / l
            o_ref[:, g * HEAD_DIM:(g + 1) * HEAD_DIM] = o.astype(o_ref.dtype)

    @pl.when(qi < n_lat_tiles)
    def _():
        run(k_ref[...], v_ref[...])

    @pl.when(qi >= n_lat_tiles)
    def _():
        if with_ctx:
            run(k_ref[seq:, :], v_ref[seq:, :])
        else:
            o_ref[...] = jnp.zeros_like(o_ref)


def attention(q, k, v, *, seq, with_ctx):
    b, nt, _ = q.shape
    n_lat = seq // Q_TILE
    n_tiles = nt // Q_TILE
    gw = KV_GROUP * HEAD_DIM
    return pl.pallas_call(
        functools.partial(_attn_kernel, seq=seq, n_lat_tiles=n_lat, with_ctx=with_ctx),
        out_shape=jax.ShapeDtypeStruct((b, nt, ATT_WIDTH), BF16),
        grid=(b, N_KV_HEADS, n_tiles),
        in_specs=[
            pl.BlockSpec((None, Q_TILE, gw), lambda bi, h, i: (bi, i, h)),
            pl.BlockSpec((None, nt, HEAD_DIM), lambda bi, h, i: (bi, 0, h)),
            pl.BlockSpec((None, nt, HEAD_DIM), lambda bi, h, i: (bi, 0, h)),
        ],
        out_specs=pl.BlockSpec((None, Q_TILE, gw), lambda bi, h, i: (bi, i, h)),
        compiler_params=_cparams("parallel", "parallel", "arbitrary"),
        name="attention",
    )(q, k, v)


def _retention_kernel(*refs, has_prev):
    if has_prev:
        q_ref, k_ref, v_ref, g_ref, dec_ref, xi_ref, zeta_ref, cd_ref, prev_ref, o_ref, r_ref = refs
    else:
        q_ref, k_ref, v_ref, g_ref, dec_ref, xi_ref, zeta_ref, cd_ref, o_ref, r_ref = refs

    @pl.when(pl.program_id(1) == 0)
    def _():
        r_ref[...] = jnp.zeros_like(r_ref)

    for h in range(RET_HEADS):
        ks = slice(h * RET_DK, (h + 1) * RET_DK)
        vs = slice(h * RET_DV, (h + 1) * RET_DV)
        q = q_ref[:, ks]
        k = k_ref[:, ks]
        v = v_ref[:, vs]
        state = r_ref[h]
        inner = lax.dot_general(q, k, (((1,), (1,)), ((), ())), preferred_element_type=F32) * dec_ref[h]
        o = jnp.dot(inner.astype(BF16), v, preferred_element_type=F32)
        o = o + jnp.dot(q, state.astype(BF16), preferred_element_type=F32) * xi_ref[h]
        kz = (k.astype(F32) * zeta_ref[h]).astype(BF16)
        r_ref[h] = state * cd_ref[h] + lax.dot_general(kz, v, (((0,), (0,)), ((), ())), preferred_element_type=F32)
        mu = jnp.mean(o, axis=-1, keepdims=True)
        d = o - mu
        var = jnp.mean(d * d, axis=-1, keepdims=True)
        y = _silu(g_ref[:, vs]) * (d * lax.rsqrt(var + GN_EPS))
        if has_prev:
            y = prev_ref[:, vs] + y
        o_ref[:, vs] = y.astype(o_ref.dtype)


def retention_dir(rq, rk, rv, rg, gate_col, tables, chunk_of, prev, out_dtype, *, name):
    b, nt, _ = rq.shape
    nc = nt // RET_CHUNK
    L = RET_CHUNK
    tok = lambda w: pl.BlockSpec((None, L, w), lambda bi, t: (bi, chunk_of(t), 0))
    full = lambda a: pl.BlockSpec(a.shape, lambda bi, t: (0,) * a.ndim)
    in_specs = [tok(RET_QK_WIDTH), tok(RET_QK_WIDTH), tok(RET_V_WIDTH),
                pl.BlockSpec((None, L, RET_V_WIDTH), lambda bi, t: (bi, chunk_of(t), gate_col))]
    in_specs += [full(a) for a in tables]
    args = [rq, rk, rv, rg, *tables]
    if prev is not None:
        in_specs.append(tok(RET_V_WIDTH))
        args.append(prev)
    return pl.pallas_call(
        functools.partial(_retention_kernel, has_prev=prev is not None),
        out_shape=jax.ShapeDtypeStruct((b, nt, RET_V_WIDTH), out_dtype),
        grid=(b, nc),
        in_specs=in_specs,
        out_specs=tok(RET_V_WIDTH),
        scratch_shapes=[pltpu.VMEM((RET_HEADS, RET_DK, RET_DV), F32)],
        compiler_params=_cparams("parallel", "arbitrary"),
        name=name,
    )(*args)


def retention_tables(decay_logit, backward):
    L = RET_CHUNK
    lg = jax.nn.log_sigmoid(decay_logit.astype(F32))
    pos = jnp.arange(L, dtype=F32)
    diff = pos[:, None] - pos[None, :]
    if backward:
        diff = -diff
        xi_e = L - pos
        zeta_e = pos
    else:
        xi_e = pos + 1.0
        zeta_e = L - 1.0 - pos
    decay = jnp.where(diff >= 0, jnp.exp(jnp.maximum(diff, 0.0) * lg[:, None, None]), 0.0)
    xi = jnp.broadcast_to(jnp.exp(xi_e * lg[:, None])[:, :, None], (RET_HEADS, L, RET_DV))
    zeta = jnp.broadcast_to(jnp.exp(zeta_e * lg[:, None])[:, :, None], (RET_HEADS, L, RET_DK))
    cd = jnp.broadcast_to(jnp.exp(L * lg)[:, None, None], (RET_HEADS, 1, RET_DV))
    return decay, xi, zeta, cd


def _merge_kernel(a_ref, r_ref, wa_ref, wr_ref, ga_ref, gr_ref, o_ref):
    za = jnp.dot(a_ref[...], wa_ref[...], preferred_element_type=F32)
    zr = jnp.dot(r_ref[...], wr_ref[...], preferred_element_type=F32)
    o_ref[...] = (ga_ref[...] * za + gr_ref[...] * zr).astype(o_ref.dtype)


def branch_merge(att, ret, w_att_o, w_ret_o, gsig, *, tm, tn=512):
    m, ka = att.shape
    kr = ret.shape[1]
    d = w_att_o.shape[1]
    nj = d // tn
    return pl.pallas_call(
        _merge_kernel,
        out_shape=jax.ShapeDtypeStruct((m, d), BF16),
        grid=(m // tm, nj),
        in_specs=[
            pl.BlockSpec((tm, ka), lambda i, j: (i, 0)),
            pl.BlockSpec((tm, kr), lambda i, j: (i, 0)),
            pl.BlockSpec((ka, tn), lambda i, j: (0, j)),
            pl.BlockSpec((kr, tn), lambda i, j: (0, j)),
            pl.BlockSpec((tm, tn), lambda i, j: (i, j)),
            pl.BlockSpec((tm, tn), lambda i, j: (i, nj + j)),
        ],
        out_specs=pl.BlockSpec((tm, tn), lambda i, j: (i, j)),
        compiler_params=_cparams("parallel", "parallel"),
        name="branch_merge",
    )(att, ret, w_att_o, w_ret_o, gsig, gsig)


def _resid_mm_kernel(z_ref, w_ref, x_ref, gb_ref, gc_ref, o_ref, *, tm, tpb, seq):
    i = pl.program_id(0)
    gate = jnp.where(_ctx_rows(i, tm, tpb, seq), gc_ref[...], gb_ref[...])
    o_ref[...] = x_ref[...] + gate * jnp.dot(z_ref[...], w_ref[...], preferred_element_type=F32)


def _mod_specs(vec, tn, tpb, ctx_row, two_d_grid=True):
    if two_d_grid:
        return (pl.BlockSpec((None, 1, tn), lambda i, j: (i // tpb, 0, j)),
                pl.BlockSpec((None, 1, tn), lambda i, j: (ctx_row, 0, j)))
    return (pl.BlockSpec((None, 1, tn), lambda i: (i // tpb, 0, 0)),
            pl.BlockSpec((None, 1, tn), lambda i: (ctx_row, 0, 0)))


def out_proj_residual(z, w_out, x, gate, *, tm, tpb, seq, ctx_row, tn=512):
    m, k = z.shape
    d = w_out.shape[1]
    sb, sc = _mod_specs(gate, tn, tpb, ctx_row)
    return pl.pallas_call(
        functools.partial(_resid_mm_kernel, tm=tm, tpb=tpb, seq=seq),
        out_shape=jax.ShapeDtypeStruct((m, d), F32),
        grid=(m // tm, d // tn),
        in_specs=[
            pl.BlockSpec((tm, k), lambda i, j: (i, 0)),
            pl.BlockSpec((k, tn), lambda i, j: (0, j)),
            pl.BlockSpec((tm, tn), lambda i, j: (i, j)),
            sb, sc,
        ],
        out_specs=pl.BlockSpec((tm, tn), lambda i, j: (i, j)),
        compiler_params=_cparams("parallel", "parallel"),
        name="out_proj",
    )(z, w_out, x, gate, gate)


def _ffn_kernel(h_ref, wg_ref, wu_ref, wd_ref, x_ref, gb_ref, gc_ref, o_ref, acc_ref, *, tm, tpb, seq):
    i, f = pl.program_id(0), pl.program_id(1)

    @pl.when(f == 0)
    def _():
        acc_ref[...] = jnp.zeros_like(acc_ref)

    h = h_ref[...]
    g = jnp.dot(h, wg_ref[...], preferred_element_type=F32)
    u = jnp.dot(h, wu_ref[...], preferred_element_type=F32)
    a = (_silu(g) * u).astype(BF16)
    acc_ref[...] += jnp.dot(a, wd_ref[...], preferred_element_type=F32)

    @pl.when(f == pl.num_programs(1) - 1)
    def _():
        gate = jnp.where(_ctx_rows(i, tm, tpb, seq), gc_ref[...], gb_ref[...])
        o_ref[...] = x_ref[...] + gate * acc_ref[...]


def ffn_residual(h, wg, wu, wd, x, gate, *, tm, tpb, seq, ctx_row, tf=512):
    m, d = h.shape
    ff = wg.shape[1]
    return pl.pallas_call(
        functools.partial(_ffn_kernel, tm=tm, tpb=tpb, seq=seq),
        out_shape=jax.ShapeDtypeStruct((m, d), F32),
        grid=(m // tm, ff // tf),
        in_specs=[
            pl.BlockSpec((tm, d), lambda i, f: (i, 0)),
            pl.BlockSpec((d, tf), lambda i, f: (0, f)),
            pl.BlockSpec((d, tf), lambda i, f: (0, f)),
            pl.BlockSpec((tf, d), lambda i, f: (f, 0)),
            pl.BlockSpec((tm, d), lambda i, f: (i, 0)),
            pl.BlockSpec((None, 1, d), lambda i, f: (i // tpb, 0, 0)),
            pl.BlockSpec((None, 1, d), lambda i, f: (ctx_row, 0, 0)),
        ],
        out_specs=pl.BlockSpec((tm, d), lambda i, f: (i, 0)),
        scratch_shapes=[pltpu.VMEM((tm, d), F32)],
        compiler_params=_cparams("parallel", "arbitrary"),
        name="ffn",
    )(h, wg, wu, wd, x, gate, gate)


def moe_plan(e1, e2, w1, w2, src_rows, tm):
    t = e1.shape[0]
    p = 2 * t
    e = jnp.stack([e1, e2], axis=1).reshape(p)
    onehot = (e[:, None] == jnp.arange(N_EXPERTS, dtype=jnp.int32)[None, :]).astype(jnp.int32)
    csum = jnp.cumsum(onehot, axis=0)
    rank = jnp.take_along_axis(csum, e[:, None], axis=1)[:, 0] - 1
    cnt = csum[-1]
    gsz = (cnt + tm - 1) // tm * tm
    gend = jnp.cumsum(gsz)
    gstart = gend - gsz
    pos = gstart[e] + rank
    n_rows = p + N_EXPERTS * tm
    n_tiles = n_rows // tm
    row_src = jnp.zeros((n_rows,), jnp.int32).at[pos].set(jnp.repeat(src_rows, 2))
    row_dst = jnp.zeros((n_rows,), jnp.int32).at[pos].set(jnp.arange(p, dtype=jnp.int32))
    row_w = jnp.zeros((n_rows,), F32).at[pos].set(jnp.stack([w1, w2], axis=1).reshape(p))
    tile_start = jnp.arange(n_tiles, dtype=jnp.int32) * tm
    tile_e = jnp.minimum(jnp.searchsorted(gend, tile_start, side="right").astype(jnp.int32), N_EXPERTS - 1)
    tile_valid = jnp.clip(gstart[tile_e] + cnt[tile_e] - tile_start, 0, tm).astype(jnp.int32)
    used = gend[-1] // tm
    tile_e = jnp.where(tile_start < gend[-1], tile_e, tile_e[jnp.maximum(used - 1, 0)])
    return (row_src.reshape(n_tiles, 1, tm), row_dst.reshape(n_tiles, 1, tm), row_w.reshape(n_tiles, tm, 1), tile_e, tile_valid)


def _moe_kernel(te_ref, tv_ref, src_ref, dst_ref, h_hbm, w_ref, wg_ref, wu_ref, wd_ref, pairs_hbm,
                xg_ref, xb_ref, acc_ref, gsem, ssem, *, tm):
    i, f = pl.program_id(0), pl.program_id(1)
    valid = tv_ref[i]

    def gather_copy(r, row):
        return pltpu.make_async_copy(h_hbm.at[pl.ds(row, 1)], xg_ref.at[pl.ds(r, 1)], gsem)

    def scatter_copy(r, row):
        return pltpu.make_async_copy(xg_ref.at[pl.ds(r, 1)], pairs_hbm.at[pl.ds(row, 1)], ssem)

    @pl.when(valid > 0)
    def _():
        @pl.when(f == 0)
        def _():
            def start(r, carry):
                gather_copy(r, src_ref[0, r]).start()
                return carry
            lax.fori_loop(0, tm, start, 0)

            def wait(r, carry):
                gather_copy(r, 0).wait()
                return carry
            lax.fori_loop(0, tm, wait, 0)
            xb_ref[...] = xg_ref[...].astype(BF16)
            acc_ref[...] = jnp.zeros_like(acc_ref)

        x = xb_ref[...]
        g = jnp.dot(x, wg_ref[...], preferred_element_type=F32)
        u = jnp.dot(x, wu_ref[...], preferred_element_type=F32)
        a = (_silu(g) * u).astype(BF16)
        acc_ref[...] += jnp.dot(a, wd_ref[...], preferred_element_type=F32)

        @pl.when(f == pl.num_programs(1) - 1)
        def _():
            xg_ref[...] = w_ref[...] * acc_ref[...]

            def start(r, carry):
                scatter_copy(r, dst_ref[0, r]).start()
                return carry
            lax.fori_loop(0, valid, start, 0)

            def wait(r, carry):
                scatter_copy(r, 0).wait()
                return carry
            lax.fori_loop(0, valid, wait, 0)


def moe_experts(h, wg, wu, wd, plan, n_pairs, *, tm, tf=512):
    row_src, row_dst, row_w, tile_e, tile_valid = plan
    n_tiles = row_src.shape[0]
    _, d, ff = wg.shape
    nf = ff // tf
    fe = lambda i, f, te, tv: jnp.where(tv[i] > 0, f, nf - 1)
    smem_rows = pl.BlockSpec((None, 1, tm), lambda i, f, te, tv: (i, 0, 0), memory_space=pltpu.SMEM)
    grid_spec = pltpu.PrefetchScalarGridSpec(
        num_scalar_prefetch=2,
        grid=(n_tiles, nf),
        in_specs=[
            smem_rows, smem_rows,
            pl.BlockSpec(memory_space=pl.ANY),
            pl.BlockSpec((None, tm, 1), lambda i, f, te, tv: (i, 0, 0)),
            pl.BlockSpec((None, d, tf), lambda i, f, te, tv: (te[i], 0, fe(i, f, te, tv))),
            pl.BlockSpec((None, d, tf), lambda i, f, te, tv: (te[i], 0, fe(i, f, te, tv))),
            pl.BlockSpec((None, tf, d), lambda i, f, te, tv: (te[i], fe(i, f, te, tv), 0)),
        ],
        out_specs=pl.BlockSpec(memory_space=pl.ANY),
        scratch_shapes=[pltpu.VMEM((tm, d), F32), pltpu.VMEM((tm, d), BF16), pltpu.VMEM((tm, d), F32),
                        pltpu.SemaphoreType.DMA(()), pltpu.SemaphoreType.DMA(())],
    )
    return pl.pallas_call(
        functools.partial(_moe_kernel, tm=tm),
        out_shape=jax.ShapeDtypeStruct((n_pairs, d), F32),
        grid_spec=grid_spec,
        compiler_params=_cparams("arbitrary", "arbitrary"),
        name="moe_experts",
    )(tile_e, tile_valid, row_src, row_dst, h, row_w, wg, wu, wd)


def _combine_final_kernel(p_ref, x_ref, g_ref, fg_ref, o_ref):
    d = x_ref.shape[-1]
    xn = x_ref[...] + g_ref[...] * (p_ref[:, :d] + p_ref[:, d:])
    o_ref[...] = _rms(xn, fg_ref[...])


def _rope_tables(ang):
    c, s = jnp.cos(ang), jnp.sin(ang)
    return jnp.concatenate([c, c], axis=-1), jnp.concatenate([-s, s], axis=-1)


def _axial_angles(seq):
    rows = seq // GRID_W
    row = jnp.repeat(jnp.arange(rows), GRID_W).astype(F32)
    col = jnp.tile(jnp.arange(GRID_W), rows).astype(F32)
    n_freq = HEAD_DIM // 4
    inv = ROPE_THETA ** (-jnp.arange(n_freq, dtype=F32) / n_freq)
    return jnp.concatenate([row[:, None] * inv, col[:, None] * inv], axis=-1)


def _linear_angles(pos, dim):
    n_freq = dim // 2
    inv = ROPE_THETA ** (-jnp.arange(n_freq, dtype=F32) / n_freq)
    return pos.astype(F32)[:, None] * inv


def kernel(x, c, ctx, c_ctx, w_ada, b_ada, norm1_g, norm2_g, w_in, q_norm_g, k_norm_g, ret_decay_f, ret_decay_b, w_att_o, w_ret_o, w_out, ffn_w_gate, ffn_w_up, ffn_w_down, moe_router, moe_w_gate, moe_w_up, moe_w_down, final_g):
    b, seq, d = x.shape
    ctx_len = ctx.shape[1]
    depth = w_ada.shape[0]
    assert ctx_len == Q_TILE and seq % Q_TILE == 0 and seq % GRID_W == 0
    nt = seq + ctx_len
    m = b * nt
    tpb_big, tpb_ffn = 4, 8
    tm_big, tm_ffn = nt // tpb_big, nt // tpb_ffn
    tm_norm, tpb_norm = Q_TILE, nt // Q_TILE
    ctx_row = b
    nchunks = nt // RET_CHUNK
    nc_lat = seq // RET_CHUNK

    cos_a, sin_a = _rope_tables(_axial_angles(seq))
    cos_a = jnp.concatenate([cos_a, jnp.ones((ctx_len, HEAD_DIM), F32)], axis=0)
    sin_a = jnp.concatenate([sin_a, jnp.zeros((ctx_len, HEAD_DIM), F32)], axis=0)
    pos = jnp.concatenate([ctx_len + jnp.arange(seq), jnp.arange(ctx_len)])
    cos_r, sin_r = _rope_tables(_linear_angles(pos, RET_DK))

    n_rows = -(-(b + 1) // 16) * 16
    cc = jnp.zeros((n_rows, d), F32).at[:b].set(c).at[b].set(c_ctx)
    mods = adaln_all(cc, w_ada, b_ada)
    mods = mods.reshape(depth, n_rows, N_MOD, d).transpose(0, 2, 1, 3).reshape(depth, N_MOD, n_rows, 1, d)

    xa = jnp.concatenate([x, ctx], axis=1).reshape(m, d)

    def norm_mod(xcur, g, sc, sh):
        sb = pl.BlockSpec((None, 1, d), lambda i: (i // tpb_norm, 0, 0))
        scx = pl.BlockSpec((None, 1, d), lambda i: (ctx_row, 0, 0))
        row = pl.BlockSpec((tm_norm, d), lambda i: (i, 0))
        return pl.pallas_call(
            functools.partial(_norm_mod_kernel, tm=tm_norm, tpb=tpb_norm, seq=seq),
            out_shape=jax.ShapeDtypeStruct((m, d), BF16),
            grid=(m // tm_norm,),
            in_specs=[row, pl.BlockSpec((1, d), lambda i: (0, 0)), sb, scx, sb, scx],
            out_specs=row,
            compiler_params=_cparams("parallel"),
            name="norm_mod",
        )(xcur, g.reshape(1, d), sc, sc, sh, sh)

    tok_tab = lambda t: (t, (tm_big, HEAD_DIM), lambda i, j: (i % tpb_big, 0))
    row_vec = lambda v: (v.reshape(1, -1), (1, v.shape[-1]), lambda i, j: (0, 0))
    rope_scaled = functools.partial(_ep_rope, scale=RET_DK ** -0.5)
    rope_plain = functools.partial(_ep_rope, scale=1.0)
    fwd_chunk = lambda t: (t + nc_lat) % nchunks
    bwd_chunk = lambda t: nchunks - 1 - t

    c0 = 0
    c_q, c0 = c0, c0 + ATT_WIDTH
    c_k, c0 = c0, c0 + KV_WIDTH
    c_v, c0 = c0, c0 + KV_WIDTH
    c_rq, c0 = c0, c0 + RET_QK_WIDTH
    c_rk, c0 = c0, c0 + RET_QK_WIDTH
    c_rv, c0 = c0, c0 + RET_V_WIDTH
    c_rg, c0 = c0, c0 + 2 * RET_V_WIDTH
    c_bg, c0 = c0, c0 + 2 * d

    for layer in range(depth):
        last = layer == depth - 1
        sh1, sc1, g1, sh2, sc2, g2 = [mods[layer, n] for n in range(N_MOD)]
        w = w_in[layer].astype(BF16)
        tn_in = 512 if d % 512 == 0 else 256

        h = norm_mod(xa, norm1_g[layer], sc1, sh1)
        pj = functools.partial(proj, h, w, tm=tm_big, tn=tn_in)
        q = pj(c_q, ATT_WIDTH, BF16, _ep_head_norm_rope, [row_vec(q_norm_g[layer]), tok_tab(cos_a), tok_tab(sin_a)], name="proj_q")
        k = pj(c_k, KV_WIDTH, BF16, _ep_head_norm_rope, [row_vec(k_norm_g[layer]), tok_tab(cos_a), tok_tab(sin_a)], name="proj_k")
        v = pj(c_v, KV_WIDTH, BF16, _ep_plain, name="proj_v")
        rq = pj(c_rq, RET_QK_WIDTH, BF16, rope_plain, [tok_tab(cos_r), tok_tab(sin_r)], name="proj_rq")
        rk = pj(c_rk, RET_QK_WIDTH, BF16, rope_scaled, [tok_tab(cos_r), tok_tab(sin_r)], name="proj_rk")
        rv = pj(c_rv, RET_V_WIDTH, BF16, _ep_plain, name="proj_rv")
        rg = pj(c_rg, 2 * RET_V_WIDTH, F32, _ep_plain, name="proj_rg")
        gsig = pj(c_bg, 2 * d, F32, _ep_sigmoid, name="proj_gates")

        r3 = lambda a: a.reshape(b, nt, a.shape[-1])
        att = attention(r3(q), r3(k), r3(v), seq=seq, with_ctx=not last).reshape(m, ATT_WIDTH)
        tf_ = retention_tables(ret_decay_f[layer], backward=False)
        tb_ = retention_tables(ret_decay_b[layer], backward=True)
        yf = retention_dir(r3(rq), r3(rk), r3(rv), r3(rg), 0, tf_, fwd_chunk, None, F32, name="retention_fwd")
        ret = retention_dir(r3(rq), r3(rk), r3(rv), r3(rg), 1, tb_, bwd_chunk, yf, BF16, name="retention_bwd").reshape(m, RET_V_WIDTH)

        z = branch_merge(att, ret, w_att_o[layer].astype(BF16), w_ret_o[layer].astype(BF16), gsig, tm=tm_big, tn=tn_in)
        xa = out_proj_residual(z, w_out[layer].astype(BF16), xa, g1, tm=tm_big, tpb=tpb_big, seq=seq, ctx_row=ctx_row, tn=tn_in)

        j = layer // 2
        if layer % 2 == 0:
            h2 = norm_mod(xa, norm2_g[layer], sc2, sh2)
            xa = ffn_residual(h2, ffn_w_gate[j].astype(BF16), ffn_w_up[j].astype(BF16), ffn_w_down[j].astype(BF16),
                              xa, g2, tm=tm_ffn, tpb=tpb_ffn, seq=seq, ctx_row=ctx_row)
        else:
            wr = jnp.zeros((d, LANES), F32).at[:, :N_EXPERTS].set(moe_router[j])
            sb = pl.BlockSpec((None, 1, d), lambda i: (i // tpb_norm, 0, 0))
            scx = pl.BlockSpec((None, 1, d), lambda i: (ctx_row, 0, 0))
            row = pl.BlockSpec((tm_norm, d), lambda i: (i, 0))
            assert last, "routed-expert layers before the last layer are not supported"
            h2, rec = pl.pallas_call(
                functools.partial(_norm_mod_router_kernel, tm=tm_norm, tpb=tpb_norm, seq=seq),
                out_shape=(jax.ShapeDtypeStruct((m, d), F32), jax.ShapeDtypeStruct((m, LANES), F32)),
                grid=(m // tm_norm,),
                in_specs=[row, pl.BlockSpec((1, d), lambda i: (0, 0)), sb, scx, sb, scx, pl.BlockSpec((d, LANES), lambda i: (0, 0))],
                out_specs=(row, pl.BlockSpec((tm_norm, LANES), lambda i: (i, 0))),
                compiler_params=_cparams("parallel"),
                name="norm_mod_router",
            )(xa, norm2_g[layer].reshape(1, d), sc2, sc2, sh2, sh2, wr)
            rec = rec.reshape(b, nt, LANES)[:, :seq, :4].reshape(b * seq, 4)
            src_rows = (jnp.arange(b, dtype=jnp.int32)[:, None] * nt + jnp.arange(seq, dtype=jnp.int32)[None, :]).reshape(b * seq)
            tm_moe = 512 if (2 * b * seq) % 512 == 0 else 128
            plan = moe_plan(rec[:, 0].astype(jnp.int32), rec[:, 1].astype(jnp.int32), rec[:, 2], rec[:, 3], src_rows, tm_moe)
            pairs = moe_experts(h2, moe_w_gate[j].astype(BF16), moe_w_up[j].astype(BF16), moe_w_down[j].astype(BF16),
                                plan, 2 * b * seq, tm=tm_moe)
            tmc = Q_TILE
            return pl.pallas_call(
                _combine_final_kernel,
                out_shape=jax.ShapeDtypeStruct((b, seq, d), F32),
                grid=(b, seq // tmc),
                in_specs=[
                    pl.BlockSpec((tmc, 2 * d), lambda bi, i: (bi * (seq // tmc) + i, 0)),
                    pl.BlockSpec((None, tmc, d), lambda bi, i: (bi, i, 0)),
                    pl.BlockSpec((None, 1, d), lambda bi, i: (bi, 0, 0)),
                    pl.BlockSpec((1, d), lambda bi, i: (0, 0)),
                ],
                out_specs=pl.BlockSpec((None, tmc, d), lambda bi, i: (bi, i, 0)),
                compiler_params=_cparams("parallel", "parallel"),
                name="combine_final",
            )(pairs.reshape(b * seq, 2 * d), xa.reshape(b, nt, d), g2, final_g.reshape(1, d))

    tm_f = Q_TILE
    return pl.pallas_call(
        _final_norm_kernel,
        out_shape=jax.ShapeDtypeStruct((b, seq, d), F32),
        grid=(b, seq // tm_f),
        in_specs=[pl.BlockSpec((None, tm_f, d), lambda bi, i: (bi, i, 0)), pl.BlockSpec((1, d), lambda bi, i: (0, 0))],
        out_specs=pl.BlockSpec((None, tm_f, d), lambda bi, i: (bi, i, 0)),
        compiler_params=_cparams("parallel", "parallel"),
        name="final_norm",
    )(xa.reshape(b, nt, d), final_g.reshape(1, d))
```

```python
import functools

import jax
import jax.numpy as jnp
import numpy as np
from jax import lax
from jax.experimental import pallas as pl
from jax.experimental.pallas import tpu as pltpu

F32 = jnp.float32
BF16 = jnp.bfloat16

GRID_W = 64
N_HEADS = 16
N_KV_HEADS = 4
HEAD_DIM = 128
KV_GROUP = N_HEADS // N_KV_HEADS
ATT_WIDTH = N_HEADS * HEAD_DIM
KV_WIDTH = N_KV_HEADS * HEAD_DIM
ROPE_THETA = 10000.0
RET_HEADS = 8
RET_DK = 128
RET_DV = 256
RET_QK_WIDTH = RET_HEADS * RET_DK
RET_V_WIDTH = RET_HEADS * RET_DV
RET_CHUNK = 128
N_EXPERTS = 8
N_MOD = 6
EPS = 1e-6
GN_EPS = 1e-5

Q_TILE = 256
LANES = 128
VMEM_LIMIT = 56 * 2**20


def _cparams(*sem):
    return pltpu.CompilerParams(dimension_semantics=sem, vmem_limit_bytes=VMEM_LIMIT)


def _silu(v):
    return v * jax.nn.sigmoid(v)


def _adaln_kernel(c_ref, w_ref, b_ref, o_ref):
    s = _silu(c_ref[...]).astype(BF16)
    o_ref[...] = jnp.dot(s, w_ref[...].astype(BF16), preferred_element_type=F32) + b_ref[...]


def adaln_all(cc, w_ada, b_ada):
    depth, d, n = w_ada.shape
    r = cc.shape[0]
    tn = 1024 if n % 1024 == 0 else n
    return pl.pallas_call(
        _adaln_kernel,
        out_shape=jax.ShapeDtypeStruct((depth, r, n), F32),
        grid=(depth, n // tn),
        in_specs=[
            pl.BlockSpec((r, d), lambda l, j: (0, 0)),
            pl.BlockSpec((None, d, tn), lambda l, j: (l, 0, j)),
            pl.BlockSpec((None, 1, tn), lambda l, j: (l, 0, j)),
        ],
        out_specs=pl.BlockSpec((None, r, tn), lambda l, j: (l, 0, j)),
        compiler_params=_cparams("parallel", "parallel"),
        name="adaln",
    )(cc, w_ada, b_ada.reshape(depth, 1, n))


def _ctx_rows(i, tm, tiles_per_batch, seq):
    rows = lax.broadcasted_iota(jnp.int32, (tm, 1), 0) + (i % tiles_per_batch) * tm
    return rows >= seq


def _rms(x, g):
    return x * lax.rsqrt(jnp.mean(x * x, axis=-1, keepdims=True) + EPS) * g


def _norm_mod_kernel(x_ref, g_ref, scb_ref, scc_ref, shb_ref, shc_ref, o_ref, *, tm, tpb, seq):
    i = pl.program_id(0)
    is_ctx = _ctx_rows(i, tm, tpb, seq)
    y = _rms(x_ref[...], g_ref[...])
    scale = jnp.where(is_ctx, scc_ref[...], scb_ref[...])
    shift = jnp.where(is_ctx, shc_ref[...], shb_ref[...])
    o_ref[...] = (y * (1 + scale) + shift).astype(o_ref.dtype)


def _norm_mod_router_kernel(x_ref, g_ref, scb_ref, scc_ref, shb_ref, shc_ref, wr_ref, o_ref, gate_ref, *, tm, tpb, seq):
    is_ctx = _ctx_rows(pl.program_id(0), tm, tpb, seq)
    y = _rms(x_ref[...], g_ref[...])
    scale = jnp.where(is_ctx, scc_ref[...], scb_ref[...])
    shift = jnp.where(is_ctx, shc_ref[...], shb_ref[...])
    t = y * (1 + scale) + shift
    o_ref[...] = t.astype(o_ref.dtype)
    logits = jnp.dot(t, wr_ref[...], preferred_element_type=F32, precision=lax.Precision.HIGHEST)
    lane = lax.broadcasted_iota(jnp.int32, logits.shape, 1)
    neg = jnp.float32(-jnp.inf)
    logits = jnp.where(lane < N_EXPERTS, logits, neg)
    m1 = jnp.max(logits, axis=-1, keepdims=True)
    i1 = jnp.min(jnp.where(logits == m1, lane, LANES), axis=-1, keepdims=True)
    rest = jnp.where(lane == i1, neg, logits)
    m2 = jnp.max(rest, axis=-1, keepdims=True)
    i2 = jnp.min(jnp.where(rest == m2, lane, LANES), axis=-1, keepdims=True)
    e2 = jnp.exp(m2 - m1)
    w1 = 1.0 / (1.0 + e2)
    w2 = e2 / (1.0 + e2)
    rec = jnp.where(lane == 0, i1.astype(F32), jnp.where(lane == 1, i2.astype(F32), jnp.where(lane == 2, w1, w2)))
    gate_ref[...] = jnp.where(lane < 4, rec, 0.0)


def _final_norm_kernel(x_ref, g_ref, o_ref):
    o_ref[...] = _rms(x_ref[...], g_ref[...])


def _mm_kernel(x_ref, w_ref, *rest, epilogue):
    o_ref = rest[-1]
    acc = jnp.dot(x_ref[...], w_ref[...], preferred_element_type=F32)
    o_ref[...] = epilogue(acc, *[r[...] for r in rest[:-1]]).astype(o_ref.dtype)


def _rope(y, cos, sin):
    return y * cos + pltpu.roll(y, HEAD_DIM // 2, 1) * sin


def _ep_plain(acc):
    return acc


def _ep_sigmoid(acc):
    return jax.nn.sigmoid(acc)


def _ep_head_norm_rope(acc, g, cos, sin):
    outs = []
    for h in range(acc.shape[1] // HEAD_DIM):
        y = _rms(acc[:, h * HEAD_DIM:(h + 1) * HEAD_DIM], g)
        outs.append(_rope(y, cos, sin))
    return jnp.concatenate(outs, axis=1)


def _ep_rope(acc, cos, sin, *, scale):
    outs = []
    for h in range(acc.shape[1] // RET_DK):
        y = _rope(acc[:, h * RET_DK:(h + 1) * RET_DK], cos, sin)
        outs.append(y * scale if scale != 1.0 else y)
    return jnp.concatenate(outs, axis=1)


def proj(h, w, col0, ncols, out_dtype, epilogue, extras=(), *, tm, tn=512, name):
    m, k = h.shape
    assert col0 % tn == 0 and ncols % tn == 0 and m % tm == 0
    jb = col0 // tn
    in_specs = [
        pl.BlockSpec((tm, k), lambda i, j: (i, 0)),
        pl.BlockSpec((k, tn), lambda i, j: (0, jb + j)),
    ]
    args = [h, w]
    for arr, bs, im in extras:
        in_specs.append(pl.BlockSpec(bs, im))
        args.append(arr)
    return pl.pallas_call(
        functools.partial(_mm_kernel, epilogue=epilogue),
        out_shape=jax.ShapeDtypeStruct((m, ncols), out_dtype),
        grid=(m // tm, ncols // tn),
        in_specs=in_specs,
        out_specs=pl.BlockSpec((tm, tn), lambda i, j: (i, j)),
        compiler_params=_cparams("parallel", "parallel"),
        name=name,
    )(*args)


def _attn_kernel(q_ref, k_ref, v_ref, o_ref, *, seq, n_lat_tiles, with_ctx):
    qi = pl.program_id(2)
    scale = HEAD_DIM ** -0.5

    def run(k, v):
        for g in range(KV_GROUP):
            q = q_ref[:, g * HEAD_DIM:(g + 1) * HEAD_DIM]
            s = lax.dot_general(q, k, (((1,), (1,)), ((), ())), preferred_element_type=F32) * scale
            m = jnp.max(s, axis=-1, keepdims=True)
            p = jnp.exp(s - m)
            l = jnp.sum(p, axis=-1, keepdims=True)
            o = jnp.dot(p.astype(BF16), v, preferred_element_type=F32) / l
            o_ref[:, g * HEAD_DIM:(g + 1) * HEAD_DIM] = o.astype(o_ref.dtype)

    @pl.when(qi < n_lat_tiles)
    def _():
        run(k_ref[...], v_ref[...])

    @pl.when(qi >= n_lat_tiles)
    def _():
        if with_ctx:
            run(k_ref[seq:, :], v_ref[seq:, :])
        else:
            o_ref[...] = jnp.zeros_like(o_ref)


def attention(q, k, v, *, seq, with_ctx):
    b, nt, _ = q.shape
    n_lat = seq // Q_TILE
    n_tiles = nt // Q_TILE
    gw = KV_GROUP * HEAD_DIM
    return pl.pallas_call(
        functools.partial(_attn_kernel, seq=seq, n_lat_tiles=n_lat, with_ctx=with_ctx),
        out_shape=jax.ShapeDtypeStruct((b, nt, ATT_WIDTH), BF16),
        grid=(b, N_KV_HEADS, n_tiles),
        in_specs=[
            pl.BlockSpec((None, Q_TILE, gw), lambda bi, h, i: (bi, i, h)),
            pl.BlockSpec((None, nt, HEAD_DIM), lambda bi, h, i: (bi, 0, h)),
            pl.BlockSpec((None, nt, HEAD_DIM), lambda bi, h, i: (bi, 0, h)),
        ],
        out_specs=pl.BlockSpec((None, Q_TILE, gw), lambda bi, h, i: (bi, i, h)),
        compiler_params=_cparams("parallel", "parallel", "arbitrary"),
        name="attention",
    )(q, k, v)


def _retention_kernel(*refs, has_prev):
    if has_prev:
        q_ref, k_ref, v_ref, g_ref, dec_ref, xi_ref, zeta_ref, cd_ref, prev_ref, o_ref, r_ref = refs
    else:
        q_ref, k_ref, v_ref, g_ref, dec_ref, xi_ref, zeta_ref, cd_ref, o_ref, r_ref = refs

    @pl.when(pl.program_id(1) == 0)
    def _():
        r_ref[...] = jnp.zeros_like(r_ref)

    for h in range(RET_HEADS):
        ks = slice(h * RET_DK, (h + 1) * RET_DK)
        vs = slice(h * RET_DV, (h + 1) * RET_DV)
        q = q_ref[:, ks]
        k = k_ref[:, ks]
        v = v_ref[:, vs]
        state = r_ref[h]
        inner = lax.dot_general(q, k, (((1,), (1,)), ((), ())), preferred_element_type=F32) * dec_ref[h]
        o = jnp.dot(inner.astype(BF16), v, preferred_element_type=F32)
        o = o + jnp.dot(q, state.astype(BF16), preferred_element_type=F32) * xi_ref[h]
        kz = (k.astype(F32) * zeta_ref[h]).astype(BF16)
        r_ref[h] = state * cd_ref[h] + lax.dot_general(kz, v, (((0,), (0,)), ((), ())), preferred_element_type=F32)
        mu = jnp.mean(o, axis=-1, keepdims=True)
        d = o - mu
        var = jnp.mean(d * d, axis=-1, keepdims=True)
        y = _silu(g_ref[:, vs]) * (d * lax.rsqrt(var + GN_EPS))
        if has_prev:
            y = prev_ref[:, vs] + y
        o_ref[:, vs] = y.astype(o_ref.dtype)


def retention_dir(rq, rk, rv, rg, gate_col, tables, chunk_of, prev, out_dtype, *, name):
    b, nt, _ = rq.shape
    nc = nt // RET_CHUNK
    L = RET_CHUNK
    tok = lambda w: pl.BlockSpec((None, L, w), lambda bi, t: (bi, chunk_of(t), 0))
    full = lambda a: pl.BlockSpec(a.shape, lambda bi, t: (0,) * a.ndim)
    in_specs = [tok(RET_QK_WIDTH), tok(RET_QK_WIDTH), tok(RET_V_WIDTH),
                pl.BlockSpec((None, L, RET_V_WIDTH), lambda bi, t: (bi, chunk_of(t), gate_col))]
    in_specs += [full(a) for a in tables]
    args = [rq, rk, rv, rg, *tables]
    if prev is not None:
        in_specs.append(tok(RET_V_WIDTH))
        args.append(prev)
    return pl.pallas_call(
        functools.partial(_retention_kernel, has_prev=prev is not None),
        out_shape=jax.ShapeDtypeStruct((b, nt, RET_V_WIDTH), out_dtype),
        grid=(b, nc),
        in_specs=in_specs,
        out_specs=tok(RET_V_WIDTH),
        scratch_shapes=[pltpu.VMEM((RET_HEADS, RET_DK, RET_DV), F32)],
        compiler_params=_cparams("parallel", "arbitrary"),
        name=name,
    )(*args)


def retention_tables(decay_logit, backward):
    L = RET_CHUNK
    lg = jax.nn.log_sigmoid(decay_logit.astype(F32))
    pos = jnp.arange(L, dtype=F32)
    diff = pos[:, None] - pos[None, :]
    if backward:
        diff = -diff
        xi_e = L - pos
        zeta_e = pos
    else:
        xi_e = pos + 1.0
        zeta_e = L - 1.0 - pos
    decay = jnp.where(diff >= 0, jnp.exp(jnp.maximum(diff, 0.0) * lg[:, None, None]), 0.0)
    xi = jnp.broadcast_to(jnp.exp(xi_e * lg[:, None])[:, :, None], (RET_HEADS, L, RET_DV))
    zeta = jnp.broadcast_to(jnp.exp(zeta_e * lg[:, None])[:, :, None], (RET_HEADS, L, RET_DK))
    cd = jnp.broadcast_to(jnp.exp(L * lg)[:, None, None], (RET_HEADS, 1, RET_DV))
    return decay, xi, zeta, cd


def _merge_kernel(a_ref, r_ref, wa_ref, wr_ref, ga_ref, gr_ref, o_ref):
    za = jnp.dot(a_ref[...], wa_ref[...], preferred_element_type=F32)
    zr = jnp.dot(r_ref[...], wr_ref[...], preferred_element_type=F32)
    o_ref[...] = (ga_ref[...] * za + gr_ref[...] * zr).astype(o_ref.dtype)


def branch_merge(att, ret, w_att_o, w_ret_o, gsig, *, tm, tn=512):
    m, ka = att.shape
    kr = ret.shape[1]
    d = w_att_o.shape[1]
    nj = d // tn
    return pl.pallas_call(
        _merge_kernel,
        out_shape=jax.ShapeDtypeStruct((m, d), BF16),
        grid=(m // tm, nj),
        in_specs=[
            pl.BlockSpec((tm, ka), lambda i, j: (i, 0)),
            pl.BlockSpec((tm, kr), lambda i, j: (i, 0)),
            pl.BlockSpec((ka, tn), lambda i, j: (0, j)),
            pl.BlockSpec((kr, tn), lambda i, j: (0, j)),
            pl.BlockSpec((tm, tn), lambda i, j: (i, j)),
            pl.BlockSpec((tm, tn), lambda i, j: (i, nj + j)),
        ],
        out_specs=pl.BlockSpec((tm, tn), lambda i, j: (i, j)),
        compiler_params=_cparams("parallel", "parallel"),
        name="branch_merge",
    )(att, ret, w_att_o, w_ret_o, gsig, gsig)


def _resid_mm_kernel(z_ref, w_ref, x_ref, gb_ref, gc_ref, o_ref, *, tm, tpb, seq):
    i = pl.program_id(0)
    gate = jnp.where(_ctx_rows(i, tm, tpb, seq), gc_ref[...], gb_ref[...])
    o_ref[...] = x_ref[...] + gate * jnp.dot(z_ref[...], w_ref[...], preferred_element_type=F32)


def _mod_specs(vec, tn, tpb, ctx_row, two_d_grid=True):
    if two_d_grid:
        return (pl.BlockSpec((None, 1, tn), lambda i, j: (i // tpb, 0, j)),
                pl.BlockSpec((None, 1, tn), lambda i, j: (ctx_row, 0, j)))
    return (pl.BlockSpec((None, 1, tn), lambda i: (i // tpb, 0, 0)),
            pl.BlockSpec((None, 1, tn), lambda i: (ctx_row, 0, 0)))


def out_proj_residual(z, w_out, x, gate, *, tm, tpb, seq, ctx_row, tn=512):
    m, k = z.shape
    d = w_out.shape[1]
    sb, sc = _mod_specs(gate, tn, tpb, ctx_row)
    return pl.pallas_call(
        functools.partial(_resid_mm_kernel, tm=tm, tpb=tpb, seq=seq),
        out_shape=jax.ShapeDtypeStruct((m, d), F32),
        grid=(m // tm, d // tn),
        in_specs=[
            pl.BlockSpec((tm, k), lambda i, j: (i, 0)),
            pl.BlockSpec((k, tn), lambda i, j: (0, j)),
            pl.BlockSpec((tm, tn), lambda i, j: (i, j)),
            sb, sc,
        ],
        out_specs=pl.BlockSpec((tm, tn), lambda i, j: (i, j)),
        compiler_params=_cparams("parallel", "parallel"),
        name="out_proj",
    )(z, w_out, x, gate, gate)


def _ffn_kernel(h_ref, wg_ref, wu_ref, wd_ref, x_ref, gb_ref, gc_ref, o_ref, acc_ref, *, tm, tpb, seq):
    i, f = pl.program_id(0), pl.program_id(1)

    @pl.when(f == 0)
    def _():
        acc_ref[...] = jnp.zeros_like(acc_ref)

    h = h_ref[...]
    g = jnp.dot(h, wg_ref[...], preferred_element_type=F32)
    u = jnp.dot(h, wu_ref[...], preferred_element_type=F32)
    a = (_silu(g) * u).astype(BF16)
    acc_ref[...] += jnp.dot(a, wd_ref[...], preferred_element_type=F32)

    @pl.when(f == pl.num_programs(1) - 1)
    def _():
        gate = jnp.where(_ctx_rows(i, tm, tpb, seq), gc_ref[...], gb_ref[...])
        o_ref[...] = x_ref[...] + gate * acc_ref[...]


def ffn_residual(h, wg, wu, wd, x, gate, *, tm, tpb, seq, ctx_row, tf=512):
    m, d = h.shape
    ff = wg.shape[1]
    return pl.pallas_call(
        functools.partial(_ffn_kernel, tm=tm, tpb=tpb, seq=seq),
        out_shape=jax.ShapeDtypeStruct((m, d), F32),
        grid=(m // tm, ff // tf),
        in_specs=[
            pl.BlockSpec((tm, d), lambda i, f: (i, 0)),
            pl.BlockSpec((d, tf), lambda i, f: (0, f)),
            pl.BlockSpec((d, tf), lambda i, f: (0, f)),
            pl.BlockSpec((tf, d), lambda i, f: (f, 0)),
            pl.BlockSpec((tm, d), lambda i, f: (i, 0)),
            pl.BlockSpec((None, 1, d), lambda i, f: (i // tpb, 0, 0)),
            pl.BlockSpec((None, 1, d), lambda i, f: (ctx_row, 0, 0)),
        ],
        out_specs=pl.BlockSpec((tm, d), lambda i, f: (i, 0)),
        scratch_shapes=[pltpu.VMEM((tm, d), F32)],
        compiler_params=_cparams("parallel", "arbitrary"),
        name="ffn",
    )(h, wg, wu, wd, x, gate, gate)


def moe_plan(e1, e2, w1, w2, src_rows, tm):
    t = e1.shape[0]
    p = 2 * t
    e = jnp.stack([e1, e2], axis=1).reshape(p)
    onehot = (e[:, None] == jnp.arange(N_EXPERTS, dtype=jnp.int32)[None, :]).astype(jnp.int32)
    csum = jnp.cumsum(onehot, axis=0)
    rank = jnp.take_along_axis(csum, e[:, None], axis=1)[:, 0] - 1
    cnt = csum[-1]
    gsz = (cnt + tm - 1) // tm * tm
    gend = jnp.cumsum(gsz)
    gstart = gend - gsz
    pos = gstart[e] + rank
    n_rows = p + N_EXPERTS * tm
    n_tiles = n_rows // tm
    pair_of = jnp.full((n_rows,), -1, jnp.int32).at[pos].set(jnp.arange(p, dtype=jnp.int32))
    real = pair_of >= 0
    pr = jnp.maximum(pair_of, 0)
    row_src = jnp.where(real, src_rows[pr // 2], 0)
    row_dst = jnp.where(real, (pr % 2) * t + pr // 2, 0)
    row_w = jnp.where(real, jnp.stack([w1, w2], axis=1).reshape(p)[pr], 0.0)
    tile_start = jnp.arange(n_tiles, dtype=jnp.int32) * tm
    tile_e = jnp.minimum(jnp.searchsorted(gend, tile_start, side="right").astype(jnp.int32), N_EXPERTS - 1)
    tile_valid = jnp.clip(gstart[tile_e] + cnt[tile_e] - tile_start, 0, tm).astype(jnp.int32)
    used = gend[-1] // tm
    tile_e = jnp.where(tile_start < gend[-1], tile_e, tile_e[jnp.maximum(used - 1, 0)])
    return (row_src.reshape(n_tiles, 1, tm), row_dst.reshape(n_tiles, 1, tm), row_w.reshape(n_tiles, tm, 1), tile_e, tile_valid)


def _moe_kernel(te_ref, tv_ref, src_ref, nsrc_ref, dst_ref, h_hbm, w_ref, wg_ref, wu_ref, wd_ref, pairs_hbm,
                xg_ref, xb_ref, acc_ref, ys_ref, pend_ref, gsem, ssem, *, tm):
    i, f = pl.program_id(0), pl.program_id(1)
    n_i, n_f = pl.num_programs(0), pl.num_programs(1)
    valid = tv_ref[i]
    slot = i % 2

    def gather_copy(s, r, row):
        return pltpu.make_async_copy(h_hbm.at[pl.ds(row, 1)], xg_ref.at[s, pl.ds(r, 1)], gsem.at[s])

    def scatter_copy(r, row):
        return pltpu.make_async_copy(ys_ref.at[pl.ds(r, 1)], pairs_hbm.at[pl.ds(row, 1)], ssem)

    def start_gather(idx_ref, s):
        def body(r, carry):
            gather_copy(s, r, idx_ref[0, r]).start()
            return carry
        lax.fori_loop(0, tm, body, 0)

    def wait_gather(s):
        def body(r, carry):
            gather_copy(s, r, 0).wait()
            return carry
        lax.fori_loop(0, tm, body, 0)

    def drain_scatter():
        def body(r, carry):
            scatter_copy(r, 0).wait()
            return carry
        lax.fori_loop(0, pend_ref[0], body, 0)
        pend_ref[0] = 0

    @pl.when((i == 0) & (f == 0))
    def _():
        pend_ref[0] = 0

        @pl.when(valid > 0)
        def _():
            start_gather(src_ref, 0)

    @pl.when(valid > 0)
    def _():
        @pl.when(f == 0)
        def _():
            wait_gather(slot)
            xb_ref[...] = xg_ref[slot].astype(BF16)
            acc_ref[...] = jnp.zeros_like(acc_ref)

            @pl.when((i + 1 < n_i) & (tv_ref[jnp.minimum(i + 1, n_i - 1)] > 0))
            def _():
                start_gather(nsrc_ref, 1 - slot)

        x = xb_ref[...]
        g = jnp.dot(x, wg_ref[...], preferred_element_type=F32)
        u = jnp.dot(x, wu_ref[...], preferred_element_type=F32)
        a = (_silu(g) * u).astype(BF16)
        acc_ref[...] += jnp.dot(a, wd_ref[...], preferred_element_type=F32)

        @pl.when(f == n_f - 1)
        def _():
            drain_scatter()
            ys_ref[...] = w_ref[...] * acc_ref[...]

            def body(r, carry):
                scatter_copy(r, dst_ref[0, r]).start()
                return carry
            lax.fori_loop(0, valid, body, 0)
            pend_ref[0] = valid

    @pl.when((i == n_i - 1) & (f == n_f - 1))
    def _():
        drain_scatter()


def moe_experts(h, wg, wu, wd, plan, n_pairs, *, tm, tf=512):
    row_src, row_dst, row_w, tile_e, tile_valid = plan
    n_tiles = row_src.shape[0]
    _, d, ff = wg.shape
    nf = ff // tf
    fe = lambda i, f, te, tv: jnp.where(tv[i] > 0, f, nf - 1)
    smem_rows = pl.BlockSpec((None, 1, tm), lambda i, f, te, tv: (i, 0, 0), memory_space=pltpu.SMEM)
    smem_next = pl.BlockSpec((None, 1, tm), lambda i, f, te, tv: (jnp.minimum(i + 1, n_tiles - 1), 0, 0), memory_space=pltpu.SMEM)
    grid_spec = pltpu.PrefetchScalarGridSpec(
        num_scalar_prefetch=2,
        grid=(n_tiles, nf),
        in_specs=[
            smem_rows, smem_next, smem_rows,
            pl.BlockSpec(memory_space=pl.ANY),
            pl.BlockSpec((None, tm, 1), lambda i, f, te, tv: (i, 0, 0)),
            pl.BlockSpec((None, d, tf), lambda i, f, te, tv: (te[i], 0, fe(i, f, te, tv))),
            pl.BlockSpec((None, d, tf), lambda i, f, te, tv: (te[i], 0, fe(i, f, te, tv))),
            pl.BlockSpec((None, tf, d), lambda i, f, te, tv: (te[i], fe(i, f, te, tv), 0)),
        ],
        out_specs=pl.BlockSpec(memory_space=pl.ANY),
        scratch_shapes=[pltpu.VMEM((2, tm, d), F32), pltpu.VMEM((tm, d), BF16), pltpu.VMEM((tm, d), F32),
                        pltpu.VMEM((tm, d), F32), pltpu.SMEM((1,), jnp.int32),
                        pltpu.SemaphoreType.DMA((2,)), pltpu.SemaphoreType.DMA(())],
    )
    return pl.pallas_call(
        functools.partial(_moe_kernel, tm=tm),
        out_shape=jax.ShapeDtypeStruct((n_pairs, d), F32),
        grid_spec=grid_spec,
        compiler_params=_cparams("arbitrary", "arbitrary"),
        name="moe_experts",
    )(tile_e, tile_valid, row_src, row_src, row_dst, h, row_w, wg, wu, wd)


def _combine_final_kernel(p0_ref, p1_ref, x_ref, g_ref, fg_ref, o_ref):
    xn = x_ref[...] + g_ref[...] * (p0_ref[...] + p1_ref[...])
    o_ref[...] = _rms(xn, fg_ref[...])


def _rope_tables(ang):
    c, s = jnp.cos(ang), jnp.sin(ang)
    return jnp.concatenate([c, c], axis=-1), jnp.concatenate([-s, s], axis=-1)


def _axial_angles(seq):
    rows = seq // GRID_W
    row = jnp.repeat(jnp.arange(rows), GRID_W).astype(F32)
    col = jnp.tile(jnp.arange(GRID_W), rows).astype(F32)
    n_freq = HEAD_DIM // 4
    inv = ROPE_THETA ** (-jnp.arange(n_freq, dtype=F32) / n_freq)
    return jnp.concatenate([row[:, None] * inv, col[:, None] * inv], axis=-1)


def _linear_angles(pos, dim):
    n_freq = dim // 2
    inv = ROPE_THETA ** (-jnp.arange(n_freq, dtype=F32) / n_freq)
    return pos.astype(F32)[:, None] * inv


def kernel(x, c, ctx, c_ctx, w_ada, b_ada, norm1_g, norm2_g, w_in, q_norm_g, k_norm_g, ret_decay_f, ret_decay_b, w_att_o, w_ret_o, w_out, ffn_w_gate, ffn_w_up, ffn_w_down, moe_router, moe_w_gate, moe_w_up, moe_w_down, final_g):
    b, seq, d = x.shape
    ctx_len = ctx.shape[1]
    depth = w_ada.shape[0]
    assert ctx_len == Q_TILE and seq % Q_TILE == 0 and seq % GRID_W == 0
    nt = seq + ctx_len
    m = b * nt
    tpb_big, tpb_ffn = 4, 8
    tm_big, tm_ffn = nt // tpb_big, nt // tpb_ffn
    tm_norm, tpb_norm = Q_TILE, nt // Q_TILE
    ctx_row = b
    nchunks = nt // RET_CHUNK
    nc_lat = seq // RET_CHUNK

    cos_a, sin_a = _rope_tables(_axial_angles(seq))
    cos_a = jnp.concatenate([cos_a, jnp.ones((ctx_len, HEAD_DIM), F32)], axis=0)
    sin_a = jnp.concatenate([sin_a, jnp.zeros((ctx_len, HEAD_DIM), F32)], axis=0)
    pos = jnp.concatenate([ctx_len + jnp.arange(seq), jnp.arange(ctx_len)])
    cos_r, sin_r = _rope_tables(_linear_angles(pos, RET_DK))

    n_rows = -(-(b + 1) // 16) * 16
    cc = jnp.zeros((n_rows, d), F32).at[:b].set(c).at[b].set(c_ctx)
    mods = adaln_all(cc, w_ada, b_ada)
    mods = mods.reshape(depth, n_rows, N_MOD, d).transpose(0, 2, 1, 3).reshape(depth, N_MOD, n_rows, 1, d)

    xa = jnp.concatenate([x, ctx], axis=1).reshape(m, d)

    def norm_mod(xcur, g, sc, sh):
        sb = pl.BlockSpec((None, 1, d), lambda i: (i // tpb_norm, 0, 0))
        scx = pl.BlockSpec((None, 1, d), lambda i: (ctx_row, 0, 0))
        row = pl.BlockSpec((tm_norm, d), lambda i: (i, 0))
        return pl.pallas_call(
            functools.partial(_norm_mod_kernel, tm=tm_norm, tpb=tpb_norm, seq=seq),
            out_shape=jax.ShapeDtypeStruct((m, d), BF16),
            grid=(m // tm_norm,),
            in_specs=[row, pl.BlockSpec((1, d), lambda i: (0, 0)), sb, scx, sb, scx],
            out_specs=row,
            compiler_params=_cparams("parallel"),
            name="norm_mod",
        )(xcur, g.reshape(1, d), sc, sc, sh, sh)

    tok_tab = lambda t: (t, (tm_big, HEAD_DIM), lambda i, j: (i % tpb_big, 0))
    row_vec = lambda v: (v.reshape(1, -1), (1, v.shape[-1]), lambda i, j: (0, 0))
    rope_scaled = functools.partial(_ep_rope, scale=RET_DK ** -0.5)
    rope_plain = functools.partial(_ep_rope, scale=1.0)
    fwd_chunk = lambda t: (t + nc_lat) % nchunks
    bwd_chunk = lambda t: nchunks - 1 - t

    c0 = 0
    c_q, c0 = c0, c0 + ATT_WIDTH
    c_k, c0 = c0, c0 + KV_WIDTH
    c_v, c0 = c0, c0 + KV_WIDTH
    c_rq, c0 = c0, c0 + RET_QK_WIDTH
    c_rk, c0 = c0, c0 + RET_QK_WIDTH
    c_rv, c0 = c0, c0 + RET_V_WIDTH
    c_rg, c0 = c0, c0 + 2 * RET_V_WIDTH
    c_bg, c0 = c0, c0 + 2 * d

    for layer in range(depth):
        last = layer == depth - 1
        sh1, sc1, g1, sh2, sc2, g2 = [mods[layer, n] for n in range(N_MOD)]
        w = w_in[layer].astype(BF16)
        tn_in = 512 if d % 512 == 0 else 256

        h = norm_mod(xa, norm1_g[layer], sc1, sh1)
        pj = functools.partial(proj, h, w, tm=tm_big, tn=tn_in)
        q = pj(c_q, ATT_WIDTH, BF16, _ep_head_norm_rope, [row_vec(q_norm_g[layer]), tok_tab(cos_a), tok_tab(sin_a)], name="proj_q")
        k = pj(c_k, KV_WIDTH, BF16, _ep_head_norm_rope, [row_vec(k_norm_g[layer]), tok_tab(cos_a), tok_tab(sin_a)], name="proj_k")
        v = pj(c_v, KV_WIDTH, BF16, _ep_plain, name="proj_v")
        rq = pj(c_rq, RET_QK_WIDTH, BF16, rope_plain, [tok_tab(cos_r), tok_tab(sin_r)], name="proj_rq")
        rk = pj(c_rk, RET_QK_WIDTH, BF16, rope_scaled, [tok_tab(cos_r), tok_tab(sin_r)], name="proj_rk")
        rv = pj(c_rv, RET_V_WIDTH, BF16, _ep_plain, name="proj_rv")
        rg = pj(c_rg, 2 * RET_V_WIDTH, F32, _ep_plain, name="proj_rg")
        gsig = pj(c_bg, 2 * d, F32, _ep_sigmoid, name="proj_gates")

        r3 = lambda a: a.reshape(b, nt, a.shape[-1])
        att = attention(r3(q), r3(k), r3(v), seq=seq, with_ctx=not last).reshape(m, ATT_WIDTH)
        tf_ = retention_tables(ret_decay_f[layer], backward=False)
        tb_ = retention_tables(ret_decay_b[layer], backward=True)
        yf = retention_dir(r3(rq), r3(rk), r3(rv), r3(rg), 0, tf_, fwd_chunk, None, F32, name="retention_fwd")
        ret = retention_dir(r3(rq), r3(rk), r3(rv), r3(rg), 1, tb_, bwd_chunk, yf, BF16, name="retention_bwd").reshape(m, RET_V_WIDTH)

        z = branch_merge(att, ret, w_att_o[layer].astype(BF16), w_ret_o[layer].astype(BF16), gsig, tm=tm_big, tn=tn_in)
        xa = out_proj_residual(z, w_out[layer].astype(BF16), xa, g1, tm=tm_big, tpb=tpb_big, seq=seq, ctx_row=ctx_row, tn=tn_in)

        j = layer // 2
        if layer % 2 == 0:
            h2 = norm_mod(xa, norm2_g[layer], sc2, sh2)
            xa = ffn_residual(h2, ffn_w_gate[j].astype(BF16), ffn_w_up[j].astype(BF16), ffn_w_down[j].astype(BF16),
                              xa, g2, tm=tm_ffn, tpb=tpb_ffn, seq=seq, ctx_row=ctx_row)
        else:
            wr = jnp.zeros((d, LANES), F32).at[:, :N_EXPERTS].set(moe_router[j])
            sb = pl.BlockSpec((None, 1, d), lambda i: (i // tpb_norm, 0, 0))
            scx = pl.BlockSpec((None, 1, d), lambda i: (ctx_row, 0, 0))
            row = pl.BlockSpec((tm_norm, d), lambda i: (i, 0))
            assert last, "routed-expert layers before the last layer are not supported"
            h2, rec = pl.pallas_call(
                functools.partial(_norm_mod_router_kernel, tm=tm_norm, tpb=tpb_norm, seq=seq),
                out_shape=(jax.ShapeDtypeStruct((m, d), F32), jax.ShapeDtypeStruct((m, LANES), F32)),
                grid=(m // tm_norm,),
                in_specs=[row, pl.BlockSpec((1, d), lambda i: (0, 0)), sb, scx, sb, scx, pl.BlockSpec((d, LANES), lambda i: (0, 0))],
                out_specs=(row, pl.BlockSpec((tm_norm, LANES), lambda i: (i, 0))),
                compiler_params=_cparams("parallel"),
                name="norm_mod_router",
            )(xa, norm2_g[layer].reshape(1, d), sc2, sc2, sh2, sh2, wr)
            rec = rec.reshape(b, nt, LANES)[:, :seq, :4].reshape(b * seq, 4)
            src_rows = (jnp.arange(b, dtype=jnp.int32)[:, None] * nt + jnp.arange(seq, dtype=jnp.int32)[None, :]).reshape(b * seq)
            tm_moe = 512 if (2 * b * seq) % 512 == 0 else 128
            plan = moe_plan(rec[:, 0].astype(jnp.int32), rec[:, 1].astype(jnp.int32), rec[:, 2], rec[:, 3], src_rows, tm_moe)
            pairs = moe_experts(h2, moe_w_gate[j].astype(BF16), moe_w_up[j].astype(BF16), moe_w_down[j].astype(BF16),
                                plan, 2 * b * seq, tm=tm_moe)
            tmc = Q_TILE
            return pl.pallas_call(
                _combine_final_kernel,
                out_shape=jax.ShapeDtypeStruct((b, seq, d), F32),
                grid=(b, seq // tmc),
                in_specs=[
                    pl.BlockSpec((None, None, tmc, d), lambda bi, i: (0, bi, i, 0)),
                    pl.BlockSpec((None, None, tmc, d), lambda bi, i: (1, bi, i, 0)),
                    pl.BlockSpec((None, tmc, d), lambda bi, i: (bi, i, 0)),
                    pl.BlockSpec((None, 1, d), lambda bi, i: (bi, 0, 0)),
                    pl.BlockSpec((1, d), lambda bi, i: (0, 0)),
                ],
                out_specs=pl.BlockSpec((None, tmc, d), lambda bi, i: (bi, i, 0)),
                compiler_params=_cparams("parallel", "parallel"),
                name="combine_final",
            )(pairs.reshape(2, b, seq, d), pairs.reshape(2, b, seq, d), xa.reshape(b, nt, d), g2, final_g.reshape(1, d))

    tm_f = Q_TILE
    return pl.pallas_call(
        _final_norm_kernel,
        out_shape=jax.ShapeDtypeStruct((b, seq, d), F32),
        grid=(b, seq // tm_f),
        in_specs=[pl.BlockSpec((None, tm_f, d), lambda bi, i: (bi, i, 0)), pl.BlockSpec((1, d), lambda bi, i: (0, 0))],
        out_specs=pl.BlockSpec((None, tm_f, d), lambda bi, i: (bi, i, 0)),
        compiler_params=_cparams("parallel", "parallel"),
        name="final_norm",
    )(xa.reshape(b, nt, d), final_g.reshape(1, d))
```

```python
import functools

import jax
import jax.numpy as jnp
import numpy as np
from jax import lax
from jax.experimental import pallas as pl
from jax.experimental.pallas import tpu as pltpu

F32 = jnp.float32
BF16 = jnp.bfloat16

GRID_W = 64
N_HEADS = 16
N_KV_HEADS = 4
HEAD_DIM = 128
KV_GROUP = N_HEADS // N_KV_HEADS
ATT_WIDTH = N_HEADS * HEAD_DIM
KV_WIDTH = N_KV_HEADS * HEAD_DIM
ROPE_THETA = 10000.0
RET_HEADS = 8
RET_DK = 128
RET_DV = 256
RET_QK_WIDTH = RET_HEADS * RET_DK
RET_V_WIDTH = RET_HEADS * RET_DV
RET_CHUNK = 128
N_EXPERTS = 8
N_MOD = 6
EPS = 1e-6
GN_EPS = 1e-5

Q_TILE = 256
LANES = 128
VMEM_LIMIT = 56 * 2**20


def _cparams(*sem):
    return pltpu.CompilerParams(dimension_semantics=sem, vmem_limit_bytes=VMEM_LIMIT)


def _silu(v):
    return v * jax.nn.sigmoid(v)


def _adaln_kernel(c_ref, w_ref, b_ref, o_ref):
    s = _silu(c_ref[...]).astype(BF16)
    o_ref[...] = jnp.dot(s, w_ref[...].astype(BF16), preferred_element_type=F32) + b_ref[...]


def adaln_all(cc, w_ada, b_ada):
    depth, d, n = w_ada.shape
    r = cc.shape[0]
    tn = 1024 if n % 1024 == 0 else n
    return pl.pallas_call(
        _adaln_kernel,
        out_shape=jax.ShapeDtypeStruct((depth, r, n), F32),
        grid=(depth, n // tn),
        in_specs=[
            pl.BlockSpec((r, d), lambda l, j: (0, 0)),
            pl.BlockSpec((None, d, tn), lambda l, j: (l, 0, j)),
            pl.BlockSpec((None, 1, tn), lambda l, j: (l, 0, j)),
        ],
        out_specs=pl.BlockSpec((None, r, tn), lambda l, j: (l, 0, j)),
        compiler_params=_cparams("parallel", "parallel"),
        name="adaln",
    )(cc, w_ada, b_ada.reshape(depth, 1, n))


def _ctx_rows(i, tm, tiles_per_batch, seq):
    rows = lax.broadcasted_iota(jnp.int32, (tm, 1), 0) + (i % tiles_per_batch) * tm
    return rows >= seq


def _rms(x, g):
    return x * lax.rsqrt(jnp.mean(x * x, axis=-1, keepdims=True) + EPS) * g


def _norm_mod_kernel(x_ref, g_ref, scb_ref, scc_ref, shb_ref, shc_ref, o_ref, *, tm, tpb, seq):
    i = pl.program_id(0)
    is_ctx = _ctx_rows(i, tm, tpb, seq)
    y = _rms(x_ref[...], g_ref[...])
    scale = jnp.where(is_ctx, scc_ref[...], scb_ref[...])
    shift = jnp.where(is_ctx, shc_ref[...], shb_ref[...])
    o_ref[...] = (y * (1 + scale) + shift).astype(o_ref.dtype)


def _norm_mod_router_kernel(x_ref, g_ref, scb_ref, scc_ref, shb_ref, shc_ref, wr_ref, o_ref, gate_ref, *, tm, tpb, seq):
    is_ctx = _ctx_rows(pl.program_id(0), tm, tpb, seq)
    y = _rms(x_ref[...], g_ref[...])
    scale = jnp.where(is_ctx, scc_ref[...], scb_ref[...])
    shift = jnp.where(is_ctx, shc_ref[...], shb_ref[...])
    t = y * (1 + scale) + shift
    o_ref[...] = t.astype(o_ref.dtype)
    logits = jnp.dot(t, wr_ref[...], preferred_element_type=F32, precision=lax.Precision.HIGHEST)
    lane = lax.broadcasted_iota(jnp.int32, logits.shape, 1)
    neg = jnp.float32(-jnp.inf)
    logits = jnp.where(lane < N_EXPERTS, logits, neg)
    m1 = jnp.max(logits, axis=-1, keepdims=True)
    i1 = jnp.min(jnp.where(logits == m1, lane, LANES), axis=-1, keepdims=True)
    rest = jnp.where(lane == i1, neg, logits)
    m2 = jnp.max(rest, axis=-1, keepdims=True)
    i2 = jnp.min(jnp.where(rest == m2, lane, LANES), axis=-1, keepdims=True)
    e2 = jnp.exp(m2 - m1)
    w1 = 1.0 / (1.0 + e2)
    w2 = e2 / (1.0 + e2)
    rec = jnp.where(lane == 0, i1.astype(F32), jnp.where(lane == 1, i2.astype(F32), jnp.where(lane == 2, w1, w2)))
    gate_ref[...] = jnp.where(lane < 4, rec, 0.0)


def _final_norm_kernel(x_ref, g_ref, o_ref):
    o_ref[...] = _rms(x_ref[...], g_ref[...])


def _mm_kernel(x_ref, w_ref, *rest, epilogue):
    o_ref = rest[-1]
    acc = jnp.dot(x_ref[...], w_ref[...], preferred_element_type=F32)
    o_ref[...] = epilogue(acc, *[r[...] for r in rest[:-1]]).astype(o_ref.dtype)


def _rope(y, cos, sin):
    return y * cos + pltpu.roll(y, HEAD_DIM // 2, 1) * sin


def _ep_plain(acc):
    return acc


def _ep_sigmoid(acc):
    return jax.nn.sigmoid(acc)


def _ep_head_norm_rope(acc, g, cos, sin):
    outs = []
    for h in range(acc.shape[1] // HEAD_DIM):
        y = _rms(acc[:, h * HEAD_DIM:(h + 1) * HEAD_DIM], g)
        outs.append(_rope(y, cos, sin))
    return jnp.concatenate(outs, axis=1)


def _ep_rope(acc, cos, sin, *, scale):
    outs = []
    for h in range(acc.shape[1] // RET_DK):
        y = _rope(acc[:, h * RET_DK:(h + 1) * RET_DK], cos, sin)
        outs.append(y * scale if scale != 1.0 else y)
    return jnp.concatenate(outs, axis=1)


def proj(h, w, col0, ncols, out_dtype, epilogue, extras=(), *, tm, tn=512, name):
    m, k = h.shape
    assert col0 % tn == 0 and ncols % tn == 0 and m % tm == 0
    jb = col0 // tn
    in_specs = [
        pl.BlockSpec((tm, k), lambda i, j: (i, 0)),
        pl.BlockSpec((k, tn), lambda i, j: (0, jb + j)),
    ]
    args = [h, w]
    for arr, bs, im in extras:
        in_specs.append(pl.BlockSpec(bs, im))
        args.append(arr)
    return pl.pallas_call(
        functools.partial(_mm_kernel, epilogue=epilogue),
        out_shape=jax.ShapeDtypeStruct((m, ncols), out_dtype),
        grid=(m // tm, ncols // tn),
        in_specs=in_specs,
        out_specs=pl.BlockSpec((tm, tn), lambda i, j: (i, j)),
        compiler_params=_cparams("parallel", "parallel"),
        name=name,
    )(*args)


def _attn_kernel(q_ref, k_ref, v_ref, o_ref, v1_ref, *, seq, n_lat_tiles, with_ctx):
    qi = pl.program_id(2)
    c = HEAD_DIM ** -0.5 * np.log2(np.e)

    @pl.when(qi == 0)
    def _():
        v1_ref[:, :HEAD_DIM] = v_ref[...]
        v1_ref[:, HEAD_DIM:] = jnp.ones(v_ref.shape, v1_ref.dtype)

    def run(k, v1):
        for g in range(KV_GROUP):
            q = q_ref[:, g * HEAD_DIM:(g + 1) * HEAD_DIM]
            s = lax.dot_general(q, k, (((1,), (1,)), ((), ())), preferred_element_type=F32)
            m = jnp.max(s, axis=-1, keepdims=True)
            p = jnp.exp2((s - m) * c).astype(BF16)
            o = jnp.dot(p, v1, preferred_element_type=F32)
            o_ref[:, g * HEAD_DIM:(g + 1) * HEAD_DIM] = (o[:, :HEAD_DIM] / o[:, HEAD_DIM:]).astype(o_ref.dtype)

    @pl.when(qi < n_lat_tiles)
    def _():
        run(k_ref[...], v1_ref[...])

    @pl.when(qi >= n_lat_tiles)
    def _():
        if with_ctx:
            run(k_ref[seq:, :], v1_ref[seq:, :])
        else:
            o_ref[...] = jnp.zeros_like(o_ref)


def attention(q, k, v, *, seq, with_ctx):
    b, nt, _ = q.shape
    n_lat = seq // Q_TILE
    n_tiles = nt // Q_TILE
    gw = KV_GROUP * HEAD_DIM
    return pl.pallas_call(
        functools.partial(_attn_kernel, seq=seq, n_lat_tiles=n_lat, with_ctx=with_ctx),
        out_shape=jax.ShapeDtypeStruct((b, nt, ATT_WIDTH), BF16),
        grid=(b, N_KV_HEADS, n_tiles),
        in_specs=[
            pl.BlockSpec((None, Q_TILE, gw), lambda bi, h, i: (bi, i, h)),
            pl.BlockSpec((None, nt, HEAD_DIM), lambda bi, h, i: (bi, 0, h)),
            pl.BlockSpec((None, nt, HEAD_DIM), lambda bi, h, i: (bi, 0, h)),
        ],
        out_specs=pl.BlockSpec((None, Q_TILE, gw), lambda bi, h, i: (bi, i, h)),
        scratch_shapes=[pltpu.VMEM((nt, 2 * HEAD_DIM), BF16)],
        compiler_params=_cparams("parallel", "parallel", "arbitrary"),
        name="attention",
    )(q, k, v)


def _retention_kernel(*refs, has_prev):
    if has_prev:
        q_ref, k_ref, v_ref, g_ref, dec_ref, xi_ref, zeta_ref, cd_ref, prev_ref, o_ref, r_ref = refs
    else:
        q_ref, k_ref, v_ref, g_ref, dec_ref, xi_ref, zeta_ref, cd_ref, o_ref, r_ref = refs

    @pl.when(pl.program_id(1) == 0)
    def _():
        r_ref[...] = jnp.zeros_like(r_ref)

    for h in range(RET_HEADS):
        ks = slice(h * RET_DK, (h + 1) * RET_DK)
        vs = slice(h * RET_DV, (h + 1) * RET_DV)
        q = q_ref[:, ks]
        k = k_ref[:, ks]
        v = v_ref[:, vs]
        state = r_ref[h]
        inner = lax.dot_general(q, k, (((1,), (1,)), ((), ())), preferred_element_type=F32) * dec_ref[h]
        o = jnp.dot(inner.astype(BF16), v, preferred_element_type=F32)
        o = o + jnp.dot(q, state.astype(BF16), preferred_element_type=F32) * xi_ref[h]
        kz = (k.astype(F32) * zeta_ref[h]).astype(BF16)
        r_ref[h] = state * cd_ref[h] + lax.dot_general(kz, v, (((0,), (0,)), ((), ())), preferred_element_type=F32)
        mu = jnp.mean(o, axis=-1, keepdims=True)
        d = o - mu
        var = jnp.mean(d * d, axis=-1, keepdims=True)
        y = _silu(g_ref[:, vs]) * (d * lax.rsqrt(var + GN_EPS))
        if has_prev:
            y = prev_ref[:, vs] + y
        o_ref[:, vs] = y.astype(o_ref.dtype)


def retention_dir(rq, rk, rv, rg, gate_col, tables, chunk_of, prev, out_dtype, *, name):
    b, nt, _ = rq.shape
    nc = nt // RET_CHUNK
    L = RET_CHUNK
    tok = lambda w: pl.BlockSpec((None, L, w), lambda bi, t: (bi, chunk_of(t), 0))
    full = lambda a: pl.BlockSpec(a.shape, lambda bi, t: (0,) * a.ndim)
    in_specs = [tok(RET_QK_WIDTH), tok(RET_QK_WIDTH), tok(RET_V_WIDTH),
                pl.BlockSpec((None, L, RET_V_WIDTH), lambda bi, t: (bi, chunk_of(t), gate_col))]
    in_specs += [full(a) for a in tables]
    args = [rq, rk, rv, rg, *tables]
    if prev is not None:
        in_specs.append(tok(RET_V_WIDTH))
        args.append(prev)
    return pl.pallas_call(
        functools.partial(_retention_kernel, has_prev=prev is not None),
        out_shape=jax.ShapeDtypeStruct((b, nt, RET_V_WIDTH), out_dtype),
        grid=(b, nc),
        in_specs=in_specs,
        out_specs=tok(RET_V_WIDTH),
        scratch_shapes=[pltpu.VMEM((RET_HEADS, RET_DK, RET_DV), F32)],
        compiler_params=_cparams("parallel", "arbitrary"),
        name=name,
    )(*args)


def retention_tables(decay_logit, backward):
    L = RET_CHUNK
    lg = jax.nn.log_sigmoid(decay_logit.astype(F32))
    pos = jnp.arange(L, dtype=F32)
    diff = pos[:, None] - pos[None, :]
    if backward:
        diff = -diff
        xi_e = L - pos
        zeta_e = pos
    else:
        xi_e = pos + 1.0
        zeta_e = L - 1.0 - pos
    decay = jnp.where(diff >= 0, jnp.exp(jnp.maximum(diff, 0.0) * lg[:, None, None]), 0.0)
    xi = jnp.broadcast_to(jnp.exp(xi_e * lg[:, None])[:, :, None], (RET_HEADS, L, RET_DV))
    zeta = jnp.broadcast_to(jnp.exp(zeta_e * lg[:, None])[:, :, None], (RET_HEADS, L, RET_DK))
    cd = jnp.broadcast_to(jnp.exp(L * lg)[:, None, None], (RET_HEADS, 1, RET_DV))
    return decay, xi, zeta, cd


def _merge_kernel(a_ref, r_ref, wa_ref, wr_ref, ga_ref, gr_ref, o_ref):
    za = jnp.dot(a_ref[...], wa_ref[...], preferred_element_type=F32)
    zr = jnp.dot(r_ref[...], wr_ref[...], preferred_element_type=F32)
    o_ref[...] = (ga_ref[...] * za + gr_ref[...] * zr).astype(o_ref.dtype)


def branch_merge(att, ret, w_att_o, w_ret_o, gsig, *, tm, tn=512):
    m, ka = att.shape
    kr = ret.shape[1]
    d = w_att_o.shape[1]
    nj = d // tn
    return pl.pallas_call(
        _merge_kernel,
        out_shape=jax.ShapeDtypeStruct((m, d), BF16),
        grid=(m // tm, nj),
        in_specs=[
            pl.BlockSpec((tm, ka), lambda i, j: (i, 0)),
            pl.BlockSpec((tm, kr), lambda i, j: (i, 0)),
            pl.BlockSpec((ka, tn), lambda i, j: (0, j)),
            pl.BlockSpec((kr, tn), lambda i, j: (0, j)),
            pl.BlockSpec((tm, tn), lambda i, j: (i, j)),
            pl.BlockSpec((tm, tn), lambda i, j: (i, nj + j)),
        ],
        out_specs=pl.BlockSpec((tm, tn), lambda i, j: (i, j)),
        compiler_params=_cparams("parallel", "parallel"),
        name="branch_merge",
    )(att, ret, w_att_o, w_ret_o, gsig, gsig)


def _resid_mm_kernel(z_ref, w_ref, x_ref, gb_ref, gc_ref, o_ref, *, tm, tpb, seq):
    i = pl.program_id(0)
    gate = jnp.where(_ctx_rows(i, tm, tpb, seq), gc_ref[...], gb_ref[...])
    o_ref[...] = x_ref[...] + gate * jnp.dot(z_ref[...], w_ref[...], preferred_element_type=F32)


def out_proj_residual(z, w_out, x, gate, *, tm, tpb, seq, ctx_row, tn=512):
    m, k = z.shape
    d = w_out.shape[1]
    return pl.pallas_call(
        functools.partial(_resid_mm_kernel, tm=tm, tpb=tpb, seq=seq),
        out_shape=jax.ShapeDtypeStruct((m, d), F32),
        grid=(m // tm, d // tn),
        in_specs=[
            pl.BlockSpec((tm, k), lambda i, j: (i, 0)),
            pl.BlockSpec((k, tn), lambda i, j: (0, j)),
            pl.BlockSpec((tm, tn), lambda i, j: (i, j)),
            pl.BlockSpec((None, 1, tn), lambda i, j: (i // tpb, 0, j)),
            pl.BlockSpec((None, 1, tn), lambda i, j: (ctx_row, 0, j)),
        ],
        out_specs=pl.BlockSpec((tm, tn), lambda i, j: (i, j)),
        compiler_params=_cparams("parallel", "parallel"),
        name="out_proj",
    )(z, w_out, x, gate, gate)


def _ffn_kernel(h_ref, wg_ref, wu_ref, wd_ref, x_ref, gb_ref, gc_ref, o_ref, acc_ref, *, tm, tpb, seq):
    i, f = pl.program_id(0), pl.program_id(1)

    @pl.when(f == 0)
    def _():
        acc_ref[...] = jnp.zeros_like(acc_ref)

    h = h_ref[...]
    g = jnp.dot(h, wg_ref[...], preferred_element_type=F32)
    u = jnp.dot(h, wu_ref[...], preferred_element_type=F32)
    a = (_silu(g) * u).astype(BF16)
    acc_ref[...] += jnp.dot(a, wd_ref[...], preferred_element_type=F32)

    @pl.when(f == pl.num_programs(1) - 1)
    def _():
        gate = jnp.where(_ctx_rows(i, tm, tpb, seq), gc_ref[...], gb_ref[...])
        o_ref[...] = x_ref[...] + gate * acc_ref[...]


def ffn_residual(h, wg, wu, wd, x, gate, *, tm, tpb, seq, ctx_row, tf=512):
    m, d = h.shape
    ff = wg.shape[1]
    return pl.pallas_call(
        functools.partial(_ffn_kernel, tm=tm, tpb=tpb, seq=seq),
        out_shape=jax.ShapeDtypeStruct((m, d), F32),
        grid=(m // tm, ff // tf),
        in_specs=[
            pl.BlockSpec((tm, d), lambda i, f: (i, 0)),
            pl.BlockSpec((d, tf), lambda i, f: (0, f)),
            pl.BlockSpec((d, tf), lambda i, f: (0, f)),
            pl.BlockSpec((tf, d), lambda i, f: (f, 0)),
            pl.BlockSpec((tm, d), lambda i, f: (i, 0)),
            pl.BlockSpec((None, 1, d), lambda i, f: (i // tpb, 0, 0)),
            pl.BlockSpec((None, 1, d), lambda i, f: (ctx_row, 0, 0)),
        ],
        out_specs=pl.BlockSpec((tm, d), lambda i, f: (i, 0)),
        scratch_shapes=[pltpu.VMEM((tm, d), F32)],
        compiler_params=_cparams("parallel", "arbitrary"),
        name="ffn",
    )(h, wg, wu, wd, x, gate, gate)


def moe_plan(e1, e2, tm):
    t = e1.shape[0]
    p = 2 * t
    e = jnp.stack([e1, e2], axis=1).reshape(p)
    onehot = (e[:, None] == jnp.arange(N_EXPERTS, dtype=jnp.int32)[None, :]).astype(jnp.int32)
    csum = jnp.cumsum(onehot, axis=0)
    rank = jnp.take_along_axis(csum, e[:, None], axis=1)[:, 0] - 1
    cnt = csum[-1]
    gsz = (cnt + tm - 1) // tm * tm
    gend = jnp.cumsum(gsz)
    gstart = gend - gsz
    pos = gstart[e] + rank
    n_tiles = -(-p // tm) + N_EXPERTS
    pair_of = jnp.full((n_tiles * tm,), -1, jnp.int32).at[pos].set(jnp.arange(p, dtype=jnp.int32))
    tile_start = jnp.arange(n_tiles, dtype=jnp.int32) * tm
    tile_e = jnp.minimum(jnp.searchsorted(gend, tile_start, side="right").astype(jnp.int32), N_EXPERTS - 1)
    tile_valid = jnp.clip(gstart[tile_e] + cnt[tile_e] - tile_start, 0, tm).astype(jnp.int32)
    used = gend[-1] // tm
    tile_e = jnp.where(tile_start < gend[-1], tile_e, tile_e[jnp.maximum(used - 1, 0)])
    return pair_of, tile_e, tile_valid


def _moe_kernel(te_ref, tv_ref, src_ref, nsrc_ref, pdst_ref, h_hbm, wg_ref, wu_ref, wd_ref, pairs_hbm,
                xg_ref, xb_ref, acc_ref, ys_ref, gsem, ssem, *, tm, n_f, dump_row):
    i, f = pl.program_id(0), pl.program_id(1)
    n_i = pl.num_programs(0)
    has_rows = tv_ref[i] > 0
    slot = i % 2
    rows_per_step = tm // n_f

    def gather_copy(s, r, row):
        return pltpu.make_async_copy(h_hbm.at[pl.ds(row, 1)], xg_ref.at[s, pl.ds(r, 1)], gsem.at[s])

    def scatter_copy(r, row):
        return pltpu.make_async_copy(ys_ref.at[pl.ds(r, 1)], pairs_hbm.at[pl.ds(row, 1)], ssem)

    def wait_gather(s):
        def body(r, carry):
            gather_copy(s, r, 0).wait()
            return carry
        lax.fori_loop(0, tm, body, 0)

    def wait_scatter():
        def body(r, carry):
            scatter_copy(r, 0).wait()
            return carry
        lax.fori_loop(0, tm, body, 0)

    def start_step_dmas():
        base = f * rows_per_step
        for j in range(rows_per_step):
            r = base + j
            gather_copy(1 - slot, r, nsrc_ref[0, r]).start()
            scatter_copy(r, jnp.where(i == 0, dump_row + r, pdst_ref[0, r])).start()

    @pl.when((i == 0) & (f == 0))
    def _():
        ys_ref[...] = jnp.zeros_like(ys_ref)

        def body(r, carry):
            gather_copy(0, r, src_ref[0, r]).start()
            return carry
        lax.fori_loop(0, tm, body, 0)

    @pl.when(f == 0)
    def _():
        wait_gather(slot)

    @pl.when(has_rows)
    def _():
        @pl.when(f == 0)
        def _():
            xb_ref[...] = xg_ref[slot].astype(BF16)
            acc_ref[...] = jnp.zeros_like(acc_ref)

        start_step_dmas()
        x = xb_ref[...]
        g = jnp.dot(x, wg_ref[...], preferred_element_type=F32)
        u = jnp.dot(x, wu_ref[...], preferred_element_type=F32)
        a = (_silu(g) * u).astype(BF16)
        acc_ref[...] += jnp.dot(a, wd_ref[...], preferred_element_type=F32)

    @pl.when(jnp.logical_not(has_rows))
    def _():
        start_step_dmas()

    @pl.when(f == n_f - 1)
    def _():
        wait_scatter()

        @pl.when(has_rows)
        def _():
            ys_ref[...] = acc_ref[...]

        @pl.when(i == n_i - 1)
        def _():
            wait_gather(1 - slot)


def moe_experts(h, wg, wu, wd, plan, t, src_of_token, *, tm, tf):
    pair_of, tile_e, tile_valid = plan
    n_tiles = tile_e.shape[0]
    _, d, ff = wg.shape
    nf = ff // tf
    assert tm % nf == 0
    real = pair_of >= 0
    pr = jnp.maximum(pair_of, 0)
    dump = 2 * t + jnp.arange(n_tiles * tm, dtype=jnp.int32) % tm
    row_src = jnp.where(real, src_of_token(pr // 2), 0).reshape(n_tiles, 1, tm)
    row_dst = jnp.where(real, (pr % 2) * t + pr // 2, dump).reshape(n_tiles, 1, tm)
    fe = lambda i, f, te, tv: jnp.where(tv[i] > 0, f, nf - 1)
    rows_at = lambda off: pl.BlockSpec((None, 1, tm), lambda i, f, te, tv: (jnp.clip(i + off, 0, n_tiles - 1), 0, 0),
                                       memory_space=pltpu.SMEM)
    grid_spec = pltpu.PrefetchScalarGridSpec(
        num_scalar_prefetch=2,
        grid=(n_tiles, nf),
        in_specs=[
            rows_at(0), rows_at(1), rows_at(-1),
            pl.BlockSpec(memory_space=pl.ANY),
            pl.BlockSpec((None, d, tf), lambda i, f, te, tv: (te[i], 0, fe(i, f, te, tv))),
            pl.BlockSpec((None, d, tf), lambda i, f, te, tv: (te[i], 0, fe(i, f, te, tv))),
            pl.BlockSpec((None, tf, d), lambda i, f, te, tv: (te[i], fe(i, f, te, tv), 0)),
        ],
        out_specs=pl.BlockSpec(memory_space=pl.ANY),
        scratch_shapes=[pltpu.VMEM((2, tm, d), F32), pltpu.VMEM((tm, d), BF16), pltpu.VMEM((tm, d), F32),
                        pltpu.VMEM((tm, d), F32), pltpu.SemaphoreType.DMA((2,)), pltpu.SemaphoreType.DMA(())],
    )
    return pl.pallas_call(
        functools.partial(_moe_kernel, tm=tm, n_f=nf, dump_row=2 * t),
        out_shape=jax.ShapeDtypeStruct((2 * t + tm, d), F32),
        grid_spec=grid_spec,
        compiler_params=pltpu.CompilerParams(dimension_semantics=("arbitrary", "arbitrary"),
                                             vmem_limit_bytes=VMEM_LIMIT, disable_bounds_checks=True),
        name="moe_experts",
    )(tile_e, tile_valid, row_src, row_src, row_dst, h, wg, wu, wd)


def _combine_final_kernel(p0_ref, p1_ref, rec_ref, x_ref, g_ref, fg_ref, o_ref):
    rec = rec_ref[...]
    y = rec[:, 2:3] * p0_ref[...] + rec[:, 3:4] * p1_ref[...]
    o_ref[...] = _rms(x_ref[...] + g_ref[...] * y, fg_ref[...])


def _rope_tables(ang):
    c, s = jnp.cos(ang), jnp.sin(ang)
    return jnp.concatenate([c, c], axis=-1), jnp.concatenate([-s, s], axis=-1)


def _axial_angles(seq):
    rows = seq // GRID_W
    row = jnp.repeat(jnp.arange(rows), GRID_W).astype(F32)
    col = jnp.tile(jnp.arange(GRID_W), rows).astype(F32)
    n_freq = HEAD_DIM // 4
    inv = ROPE_THETA ** (-jnp.arange(n_freq, dtype=F32) / n_freq)
    return jnp.concatenate([row[:, None] * inv, col[:, None] * inv], axis=-1)


def _linear_angles(pos, dim):
    n_freq = dim // 2
    inv = ROPE_THETA ** (-jnp.arange(n_freq, dtype=F32) / n_freq)
    return pos.astype(F32)[:, None] * inv


def kernel(x, c, ctx, c_ctx, w_ada, b_ada, norm1_g, norm2_g, w_in, q_norm_g, k_norm_g, ret_decay_f, ret_decay_b, w_att_o, w_ret_o, w_out, ffn_w_gate, ffn_w_up, ffn_w_down, moe_router, moe_w_gate, moe_w_up, moe_w_down, final_g):
    b, seq, d = x.shape
    ctx_len = ctx.shape[1]
    depth = w_ada.shape[0]
    assert ctx_len == Q_TILE and seq % Q_TILE == 0 and seq % GRID_W == 0
    nt = seq + ctx_len
    m = b * nt
    tpb_big, tpb_ffn = 4, 8
    tm_big, tm_ffn = nt // tpb_big, nt // tpb_ffn
    tm_norm, tpb_norm = Q_TILE, nt // Q_TILE
    ctx_row = b
    nchunks = nt // RET_CHUNK
    nc_lat = seq // RET_CHUNK

    cos_a, sin_a = _rope_tables(_axial_angles(seq))
    cos_a = jnp.concatenate([cos_a, jnp.ones((ctx_len, HEAD_DIM), F32)], axis=0)
    sin_a = jnp.concatenate([sin_a, jnp.zeros((ctx_len, HEAD_DIM), F32)], axis=0)
    pos = jnp.concatenate([ctx_len + jnp.arange(seq), jnp.arange(ctx_len)])
    cos_r, sin_r = _rope_tables(_linear_angles(pos, RET_DK))

    n_rows = -(-(b + 1) // 16) * 16
    cc = jnp.zeros((n_rows, d), F32).at[:b].set(c).at[b].set(c_ctx)
    mods = adaln_all(cc, w_ada, b_ada)
    mods = mods.reshape(depth, n_rows, N_MOD, d).transpose(0, 2, 1, 3).reshape(depth, N_MOD, n_rows, 1, d)

    xa = jnp.concatenate([x, ctx], axis=1).reshape(m, d)

    def norm_mod(xcur, g, sc, sh):
        sb = pl.BlockSpec((None, 1, d), lambda i: (i // tpb_norm, 0, 0))
        scx = pl.BlockSpec((None, 1, d), lambda i: (ctx_row, 0, 0))
        row = pl.BlockSpec((tm_norm, d), lambda i: (i, 0))
        return pl.pallas_call(
            functools.partial(_norm_mod_kernel, tm=tm_norm, tpb=tpb_norm, seq=seq),
            out_shape=jax.ShapeDtypeStruct((m, d), BF16),
            grid=(m // tm_norm,),
            in_specs=[row, pl.BlockSpec((1, d), lambda i: (0, 0)), sb, scx, sb, scx],
            out_specs=row,
            compiler_params=_cparams("parallel"),
            name="norm_mod",
        )(xcur, g.reshape(1, d), sc, sc, sh, sh)

    tok_tab = lambda t: (t, (tm_big, HEAD_DIM), lambda i, j: (i % tpb_big, 0))
    row_vec = lambda v: (v.reshape(1, -1), (1, v.shape[-1]), lambda i, j: (0, 0))
    rope_scaled = functools.partial(_ep_rope, scale=RET_DK ** -0.5)
    rope_plain = functools.partial(_ep_rope, scale=1.0)
    fwd_chunk = lambda t: (t + nc_lat) % nchunks
    bwd_chunk = lambda t: nchunks - 1 - t

    c0 = 0
    c_q, c0 = c0, c0 + ATT_WIDTH
    c_k, c0 = c0, c0 + KV_WIDTH
    c_v, c0 = c0, c0 + KV_WIDTH
    c_rq, c0 = c0, c0 + RET_QK_WIDTH
    c_rk, c0 = c0, c0 + RET_QK_WIDTH
    c_rv, c0 = c0, c0 + RET_V_WIDTH
    c_rg, c0 = c0, c0 + 2 * RET_V_WIDTH
    c_bg, c0 = c0, c0 + 2 * d

    for layer in range(depth):
        last = layer == depth - 1
        sh1, sc1, g1, sh2, sc2, g2 = [mods[layer, n] for n in range(N_MOD)]
        w = w_in[layer].astype(BF16)
        tn_in = 512 if d % 512 == 0 else 256

        h = norm_mod(xa, norm1_g[layer], sc1, sh1)
        pj = functools.partial(proj, h, w, tm=tm_big, tn=tn_in)
        q = pj(c_q, ATT_WIDTH, BF16, _ep_head_norm_rope, [row_vec(q_norm_g[layer]), tok_tab(cos_a), tok_tab(sin_a)], name="proj_q")
        k = pj(c_k, KV_WIDTH, BF16, _ep_head_norm_rope, [row_vec(k_norm_g[layer]), tok_tab(cos_a), tok_tab(sin_a)], name="proj_k")
        v = pj(c_v, KV_WIDTH, BF16, _ep_plain, name="proj_v")
        rq = pj(c_rq, RET_QK_WIDTH, BF16, rope_plain, [tok_tab(cos_r), tok_tab(sin_r)], name="proj_rq")
        rk = pj(c_rk, RET_QK_WIDTH, BF16, rope_scaled, [tok_tab(cos_r), tok_tab(sin_r)], name="proj_rk")
        rv = pj(c_rv, RET_V_WIDTH, BF16, _ep_plain, name="proj_rv")
        rg = pj(c_rg, 2 * RET_V_WIDTH, F32, _ep_plain, name="proj_rg")
        gsig = pj(c_bg, 2 * d, F32, _ep_sigmoid, name="proj_gates")

        r3 = lambda a: a.reshape(b, nt, a.shape[-1])
        att = attention(r3(q), r3(k), r3(v), seq=seq, with_ctx=not last).reshape(m, ATT_WIDTH)
        tf_ = retention_tables(ret_decay_f[layer], backward=False)
        tb_ = retention_tables(ret_decay_b[layer], backward=True)
        yf = retention_dir(r3(rq), r3(rk), r3(rv), r3(rg), 0, tf_, fwd_chunk, None, F32, name="retention_fwd")
        ret = retention_dir(r3(rq), r3(rk), r3(rv), r3(rg), 1, tb_, bwd_chunk, yf, BF16, name="retention_bwd").reshape(m, RET_V_WIDTH)

        z = branch_merge(att, ret, w_att_o[layer].astype(BF16), w_ret_o[layer].astype(BF16), gsig, tm=tm_big, tn=tn_in)
        xa = out_proj_residual(z, w_out[layer].astype(BF16), xa, g1, tm=tm_big, tpb=tpb_big, seq=seq, ctx_row=ctx_row, tn=tn_in)

        j = layer // 2
        if layer % 2 == 0:
            h2 = norm_mod(xa, norm2_g[layer], sc2, sh2)
            xa = ffn_residual(h2, ffn_w_gate[j].astype(BF16), ffn_w_up[j].astype(BF16), ffn_w_down[j].astype(BF16),
                              xa, g2, tm=tm_ffn, tpb=tpb_ffn, seq=seq, ctx_row=ctx_row)
        else:
            assert last, "routed-expert layers before the last layer are not supported"
            wr = jnp.zeros((d, LANES), F32).at[:, :N_EXPERTS].set(moe_router[j])
            sb = pl.BlockSpec((None, 1, d), lambda i: (i // tpb_norm, 0, 0))
            scx = pl.BlockSpec((None, 1, d), lambda i: (ctx_row, 0, 0))
            row = pl.BlockSpec((tm_norm, d), lambda i: (i, 0))
            h2, rec = pl.pallas_call(
                functools.partial(_norm_mod_router_kernel, tm=tm_norm, tpb=tpb_norm, seq=seq),
                out_shape=(jax.ShapeDtypeStruct((m, d), F32), jax.ShapeDtypeStruct((m, LANES), F32)),
                grid=(m // tm_norm,),
                in_specs=[row, pl.BlockSpec((1, d), lambda i: (0, 0)), sb, scx, sb, scx, pl.BlockSpec((d, LANES), lambda i: (0, 0))],
                out_specs=(row, pl.BlockSpec((tm_norm, LANES), lambda i: (i, 0))),
                compiler_params=_cparams("parallel"),
                name="norm_mod_router",
            )(xa, norm2_g[layer].reshape(1, d), sc2, sc2, sh2, sh2, wr)
            e12 = rec.reshape(b, nt, LANES)[:, :seq, :2].reshape(b * seq, 2).astype(jnp.int32)
            t_tok = b * seq
            eff = moe_w_gate.shape[-1]
            tf_moe = 512 if eff % 512 == 0 else eff
            row_step = int(np.lcm(eff // tf_moe, 16))
            tm_moe = 512 // row_step * row_step
            plan = moe_plan(e12[:, 0], e12[:, 1], tm_moe)
            pairs = moe_experts(h2, moe_w_gate[j].astype(BF16), moe_w_up[j].astype(BF16), moe_w_down[j].astype(BF16),
                                plan, t_tok, lambda tok: (tok // seq) * nt + tok % seq, tm=tm_moe, tf=tf_moe)
            tmc = Q_TILE
            spb = seq // tmc
            return pl.pallas_call(
                _combine_final_kernel,
                out_shape=jax.ShapeDtypeStruct((b, seq, d), F32),
                grid=(b, spb),
                in_specs=[
                    pl.BlockSpec((tmc, d), lambda bi, i: (bi * spb + i, 0)),
                    pl.BlockSpec((tmc, d), lambda bi, i: (t_tok // tmc + bi * spb + i, 0)),
                    pl.BlockSpec((tmc, LANES), lambda bi, i: (bi * (nt // tmc) + i, 0)),
                    pl.BlockSpec((None, tmc, d), lambda bi, i: (bi, i, 0)),
                    pl.BlockSpec((None, 1, d), lambda bi, i: (bi, 0, 0)),
                    pl.BlockSpec((1, d), lambda bi, i: (0, 0)),
                ],
                out_specs=pl.BlockSpec((None, tmc, d), lambda bi, i: (bi, i, 0)),
                compiler_params=_cparams("parallel", "parallel"),
                name="combine_final",
            )(pairs, pairs, rec, xa.reshape(b, nt, d), g2, final_g.reshape(1, d))

    tm_f = Q_TILE
    return pl.pallas_call(
        _final_norm_kernel,
        out_shape=jax.ShapeDtypeStruct((b, seq, d), F32),
        grid=(b, seq // tm_f),
        in_specs=[pl.BlockSpec((None, tm_f, d), lambda bi, i: (bi, i, 0)), pl.BlockSpec((1, d), lambda bi, i: (0, 0))],
        out_specs=pl.BlockSpec((None, tm_f, d), lambda bi, i: (bi, i, 0)),
        compiler_params=_cparams("parallel", "parallel"),
        name="final_norm",
    )(xa.reshape(b, nt, d), final_g.reshape(1, d))
```

```python
import functools

import jax
import jax.numpy as jnp
import numpy as np
from jax import lax
from jax.experimental import pallas as pl
from jax.experimental.pallas import tpu as pltpu

F32 = jnp.float32
BF16 = jnp.bfloat16

GRID_W = 64
N_HEADS = 16
N_KV_HEADS = 4
HEAD_DIM = 128
KV_GROUP = N_HEADS // N_KV_HEADS
ATT_WIDTH = N_HEADS * HEAD_DIM
KV_WIDTH = N_KV_HEADS * HEAD_DIM
ROPE_THETA = 10000.0
RET_HEADS = 8
RET_DK = 128
RET_DV = 256
RET_QK_WIDTH = RET_HEADS * RET_DK
RET_V_WIDTH = RET_HEADS * RET_DV
RET_CHUNK = 128
N_EXPERTS = 8
N_MOD = 6
EPS = 1e-6
GN_EPS = 1e-5

Q_TILE = 256
LANES = 128
VMEM_LIMIT = 56 * 2**20


def _cparams(*sem):
    return pltpu.CompilerParams(dimension_semantics=sem, vmem_limit_bytes=VMEM_LIMIT)


def _silu(v):
    return v * jax.nn.sigmoid(v)


def _adaln_kernel(c_ref, w_ref, b_ref, o_ref):
    s = _silu(c_ref[...]).astype(BF16)
    o_ref[...] = jnp.dot(s, w_ref[...].astype(BF16), preferred_element_type=F32) + b_ref[...]


def adaln_all(cc, w_ada, b_ada):
    depth, d, n = w_ada.shape
    r = cc.shape[0]
    tn = 1024 if n % 1024 == 0 else n
    return pl.pallas_call(
        _adaln_kernel,
        out_shape=jax.ShapeDtypeStruct((depth, r, n), F32),
        grid=(depth, n // tn),
        in_specs=[
            pl.BlockSpec((r, d), lambda l, j: (0, 0)),
            pl.BlockSpec((None, d, tn), lambda l, j: (l, 0, j)),
            pl.BlockSpec((None, 1, tn), lambda l, j: (l, 0, j)),
        ],
        out_specs=pl.BlockSpec((None, r, tn), lambda l, j: (l, 0, j)),
        compiler_params=_cparams("parallel", "parallel"),
        name="adaln",
    )(cc, w_ada, b_ada.reshape(depth, 1, n))


def _ctx_rows(i, tm, tiles_per_batch, seq):
    rows = lax.broadcasted_iota(jnp.int32, (tm, 1), 0) + (i % tiles_per_batch) * tm
    return rows >= seq


def _rms(x, g):
    return x * lax.rsqrt(jnp.mean(x * x, axis=-1, keepdims=True) + EPS) * g


def _norm_mod_kernel(x_ref, g_ref, scb_ref, scc_ref, shb_ref, shc_ref, o_ref, *, tm, tpb, seq):
    i = pl.program_id(0)
    is_ctx = _ctx_rows(i, tm, tpb, seq)
    y = _rms(x_ref[...], g_ref[...])
    scale = jnp.where(is_ctx, scc_ref[...], scb_ref[...])
    shift = jnp.where(is_ctx, shc_ref[...], shb_ref[...])
    o_ref[...] = (y * (1 + scale) + shift).astype(o_ref.dtype)


def _norm_mod_router_kernel(x_ref, g_ref, scb_ref, scc_ref, shb_ref, shc_ref, wr_ref, o_ref, gate_ref, *, tm, tpb, seq):
    is_ctx = _ctx_rows(pl.program_id(0), tm, tpb, seq)
    y = _rms(x_ref[...], g_ref[...])
    scale = jnp.where(is_ctx, scc_ref[...], scb_ref[...])
    shift = jnp.where(is_ctx, shc_ref[...], shb_ref[...])
    t = y * (1 + scale) + shift
    o_ref[...] = t.astype(o_ref.dtype)
    logits = jnp.dot(t, wr_ref[...], preferred_element_type=F32, precision=lax.Precision.HIGHEST)
    lane = lax.broadcasted_iota(jnp.int32, logits.shape, 1)
    neg = jnp.float32(-jnp.inf)
    logits = jnp.where(lane < N_EXPERTS, logits, neg)
    m1 = jnp.max(logits, axis=-1, keepdims=True)
    i1 = jnp.min(jnp.where(logits == m1, lane, LANES), axis=-1, keepdims=True)
    rest = jnp.where(lane == i1, neg, logits)
    m2 = jnp.max(rest, axis=-1, keepdims=True)
    i2 = jnp.min(jnp.where(rest == m2, lane, LANES), axis=-1, keepdims=True)
    e2 = jnp.exp(m2 - m1)
    w1 = 1.0 / (1.0 + e2)
    w2 = e2 / (1.0 + e2)
    rec = jnp.where(lane == 0, i1.astype(F32), jnp.where(lane == 1, i2.astype(F32), jnp.where(lane == 2, w1, w2)))
    gate_ref[...] = jnp.where(lane < 4, rec, 0.0)


def _final_norm_kernel(x_ref, g_ref, o_ref):
    o_ref[...] = _rms(x_ref[...], g_ref[...])


def _mm_kernel(x_ref, w_ref, *rest, epilogue, row_chunks):
    o_ref = rest[-1]
    tm = x_ref.shape[0]
    rows = tm // row_chunks
    for c in range(row_chunks):
        rs = slice(c * rows, (c + 1) * rows)
        acc = jnp.dot(x_ref[rs, :], w_ref[...], preferred_element_type=F32)
        extras = [r[rs, :] if r.shape[0] == tm else r[...] for r in rest[:-1]]
        o_ref[rs, :] = epilogue(acc, *extras).astype(o_ref.dtype)


def _rope(y, cos, sin):
    return y * cos + pltpu.roll(y, HEAD_DIM // 2, 1) * sin


def _ep_plain(acc):
    return acc


def _ep_sigmoid(acc):
    return jax.nn.sigmoid(acc)


def _ep_head_norm_rope(acc, g, cos, sin):
    outs = []
    for h in range(acc.shape[1] // HEAD_DIM):
        y = _rms(acc[:, h * HEAD_DIM:(h + 1) * HEAD_DIM], g)
        outs.append(_rope(y, cos, sin))
    return jnp.concatenate(outs, axis=1)


def _ep_rope(acc, cos, sin, *, scale):
    outs = []
    for h in range(acc.shape[1] // RET_DK):
        y = _rope(acc[:, h * RET_DK:(h + 1) * RET_DK], cos, sin)
        outs.append(y * scale if scale != 1.0 else y)
    return jnp.concatenate(outs, axis=1)


def proj(h, w, col0, ncols, out_dtype, epilogue, extras=(), *, tm, tn=512, row_chunks=1, name):
    m, k = h.shape
    assert col0 % tn == 0 and ncols % tn == 0 and m % tm == 0 and tm % (16 * row_chunks) == 0
    jb = col0 // tn
    in_specs = [
        pl.BlockSpec((tm, k), lambda i, j: (i, 0)),
        pl.BlockSpec((k, tn), lambda i, j: (0, jb + j)),
    ]
    args = [h, w]
    for arr, bs, im in extras:
        in_specs.append(pl.BlockSpec(bs, im))
        args.append(arr)
    return pl.pallas_call(
        functools.partial(_mm_kernel, epilogue=epilogue, row_chunks=row_chunks),
        out_shape=jax.ShapeDtypeStruct((m, ncols), out_dtype),
        grid=(m // tm, ncols // tn),
        in_specs=in_specs,
        out_specs=pl.BlockSpec((tm, tn), lambda i, j: (i, j)),
        compiler_params=_cparams("parallel", "parallel"),
        name=name,
    )(*args)


def _attn_kernel(q_ref, k_ref, v_ref, o_ref, v1_ref, *, seq, n_lat_tiles, with_ctx):
    qi = pl.program_id(2)
    c = HEAD_DIM ** -0.5 * np.log2(np.e)

    @pl.when(qi == 0)
    def _():
        v1_ref[:, :HEAD_DIM] = v_ref[...]
        v1_ref[:, HEAD_DIM:] = jnp.ones(v_ref.shape, v1_ref.dtype)

    def run(k, v1):
        for g in range(KV_GROUP):
            q = q_ref[:, g * HEAD_DIM:(g + 1) * HEAD_DIM]
            s = lax.dot_general(q, k, (((1,), (1,)), ((), ())), preferred_element_type=F32)
            m = jnp.max(s, axis=-1, keepdims=True)
            p = jnp.exp2((s - m) * c).astype(BF16)
            o = jnp.dot(p, v1, preferred_element_type=F32)
            o_ref[:, g * HEAD_DIM:(g + 1) * HEAD_DIM] = (o[:, :HEAD_DIM] / o[:, HEAD_DIM:]).astype(o_ref.dtype)

    @pl.when(qi < n_lat_tiles)
    def _():
        run(k_ref[...], v1_ref[...])

    @pl.when(qi >= n_lat_tiles)
    def _():
        if with_ctx:
            run(k_ref[seq:, :], v1_ref[seq:, :])
        else:
            o_ref[...] = jnp.zeros_like(o_ref)


def attention(q, k, v, *, seq, with_ctx):
    b, nt, _ = q.shape
    n_lat = seq // Q_TILE
    n_tiles = nt // Q_TILE
    gw = KV_GROUP * HEAD_DIM
    return pl.pallas_call(
        functools.partial(_attn_kernel, seq=seq, n_lat_tiles=n_lat, with_ctx=with_ctx),
        out_shape=jax.ShapeDtypeStruct((b, nt, ATT_WIDTH), BF16),
        grid=(b, N_KV_HEADS, n_tiles),
        in_specs=[
            pl.BlockSpec((None, Q_TILE, gw), lambda bi, h, i: (bi, i, h)),
            pl.BlockSpec((None, nt, HEAD_DIM), lambda bi, h, i: (bi, 0, h)),
            pl.BlockSpec((None, nt, HEAD_DIM), lambda bi, h, i: (bi, 0, h)),
        ],
        out_specs=pl.BlockSpec((None, Q_TILE, gw), lambda bi, h, i: (bi, i, h)),
        scratch_shapes=[pltpu.VMEM((nt, 2 * HEAD_DIM), BF16)],
        compiler_params=_cparams("parallel", "parallel", "arbitrary"),
        name="attention",
    )(q, k, v)


def _retention_kernel(*refs, has_prev):
    if has_prev:
        q_ref, k_ref, v_ref, g_ref, dec_ref, xi_ref, zeta_ref, cd_ref, prev_ref, o_ref, r_ref = refs
    else:
        q_ref, k_ref, v_ref, g_ref, dec_ref, xi_ref, zeta_ref, cd_ref, o_ref, r_ref = refs

    @pl.when(pl.program_id(1) == 0)
    def _():
        r_ref[...] = jnp.zeros_like(r_ref)

    for h in range(RET_HEADS):
        ks = slice(h * RET_DK, (h + 1) * RET_DK)
        vs = slice(h * RET_DV, (h + 1) * RET_DV)
        q = q_ref[:, ks]
        k = k_ref[:, ks]
        v = v_ref[:, vs]
        state = r_ref[h]
        inner = lax.dot_general(q, k, (((1,), (1,)), ((), ())), preferred_element_type=F32) * dec_ref[h]
        o = jnp.dot(inner.astype(BF16), v, preferred_element_type=F32)
        o = o + jnp.dot(q, state.astype(BF16), preferred_element_type=F32) * xi_ref[h]
        kz = (k.astype(F32) * zeta_ref[h]).astype(BF16)
        r_ref[h] = state * cd_ref[h] + lax.dot_general(kz, v, (((0,), (0,)), ((), ())), preferred_element_type=F32)
        mu = jnp.mean(o, axis=-1, keepdims=True)
        d = o - mu
        var = jnp.mean(d * d, axis=-1, keepdims=True)
        y = _silu(g_ref[:, vs]) * (d * lax.rsqrt(var + GN_EPS))
        if has_prev:
            y = prev_ref[:, vs] + y
        o_ref[:, vs] = y.astype(o_ref.dtype)


def retention_dir(rq, rk, rv, rg, gate_col, tables, chunk_of, prev, out_dtype, *, name):
    b, nt, _ = rq.shape
    nc = nt // RET_CHUNK
    L = RET_CHUNK
    tok = lambda w: pl.BlockSpec((None, L, w), lambda bi, t: (bi, chunk_of(t), 0))
    full = lambda a: pl.BlockSpec(a.shape, lambda bi, t: (0,) * a.ndim)
    in_specs = [tok(RET_QK_WIDTH), tok(RET_QK_WIDTH), tok(RET_V_WIDTH),
                pl.BlockSpec((None, L, RET_V_WIDTH), lambda bi, t: (bi, chunk_of(t), gate_col))]
    in_specs += [full(a) for a in tables]
    args = [rq, rk, rv, rg, *tables]
    if prev is not None:
        in_specs.append(tok(RET_V_WIDTH))
        args.append(prev)
    return pl.pallas_call(
        functools.partial(_retention_kernel, has_prev=prev is not None),
        out_shape=jax.ShapeDtypeStruct((b, nt, RET_V_WIDTH), out_dtype),
        grid=(b, nc),
        in_specs=in_specs,
        out_specs=tok(RET_V_WIDTH),
        scratch_shapes=[pltpu.VMEM((RET_HEADS, RET_DK, RET_DV), F32)],
        compiler_params=_cparams("parallel", "arbitrary"),
        name=name,
    )(*args)


def retention_tables(decay_logit, backward):
    L = RET_CHUNK
    lg = jax.nn.log_sigmoid(decay_logit.astype(F32))
    pos = jnp.arange(L, dtype=F32)
    diff = pos[:, None] - pos[None, :]
    if backward:
        diff = -diff
        xi_e = L - pos
        zeta_e = pos
    else:
        xi_e = pos + 1.0
        zeta_e = L - 1.0 - pos
    decay = jnp.where(diff >= 0, jnp.exp(jnp.maximum(diff, 0.0) * lg[:, None, None]), 0.0)
    xi = jnp.broadcast_to(jnp.exp(xi_e * lg[:, None])[:, :, None], (RET_HEADS, L, RET_DV))
    zeta = jnp.broadcast_to(jnp.exp(zeta_e * lg[:, None])[:, :, None], (RET_HEADS, L, RET_DK))
    cd = jnp.broadcast_to(jnp.exp(L * lg)[:, None, None], (RET_HEADS, 1, RET_DV))
    return decay, xi, zeta, cd


def _merge_kernel(a_ref, r_ref, wa_ref, wr_ref, ga_ref, gr_ref, o_ref):
    za = jnp.dot(a_ref[...], wa_ref[...], preferred_element_type=F32)
    zr = jnp.dot(r_ref[...], wr_ref[...], preferred_element_type=F32)
    o_ref[...] = (ga_ref[...] * za + gr_ref[...] * zr).astype(o_ref.dtype)


def branch_merge(att, ret, w_att_o, w_ret_o, gsig, *, tm, tn=512):
    m, ka = att.shape
    kr = ret.shape[1]
    d = w_att_o.shape[1]
    nj = d // tn
    return pl.pallas_call(
        _merge_kernel,
        out_shape=jax.ShapeDtypeStruct((m, d), BF16),
        grid=(m // tm, nj),
        in_specs=[
            pl.BlockSpec((tm, ka), lambda i, j: (i, 0)),
            pl.BlockSpec((tm, kr), lambda i, j: (i, 0)),
            pl.BlockSpec((ka, tn), lambda i, j: (0, j)),
            pl.BlockSpec((kr, tn), lambda i, j: (0, j)),
            pl.BlockSpec((tm, tn), lambda i, j: (i, j)),
            pl.BlockSpec((tm, tn), lambda i, j: (i, nj + j)),
        ],
        out_specs=pl.BlockSpec((tm, tn), lambda i, j: (i, j)),
        compiler_params=_cparams("parallel", "parallel"),
        name="branch_merge",
    )(att, ret, w_att_o, w_ret_o, gsig, gsig)


def _resid_mm_kernel(z_ref, w_ref, x_ref, gb_ref, gc_ref, o_ref, *, tm, tpb, seq):
    i = pl.program_id(0)
    gate = jnp.where(_ctx_rows(i, tm, tpb, seq), gc_ref[...], gb_ref[...])
    o_ref[...] = x_ref[...] + gate * jnp.dot(z_ref[...], w_ref[...], preferred_element_type=F32)


def out_proj_residual(z, w_out, x, gate, *, tm, tpb, seq, ctx_row, tn=512):
    m, k = z.shape
    d = w_out.shape[1]
    return pl.pallas_call(
        functools.partial(_resid_mm_kernel, tm=tm, tpb=tpb, seq=seq),
        out_shape=jax.ShapeDtypeStruct((m, d), F32),
        grid=(m // tm, d // tn),
        in_specs=[
            pl.BlockSpec((tm, k), lambda i, j: (i, 0)),
            pl.BlockSpec((k, tn), lambda i, j: (0, j)),
            pl.BlockSpec((tm, tn), lambda i, j: (i, j)),
            pl.BlockSpec((None, 1, tn), lambda i, j: (i // tpb, 0, j)),
            pl.BlockSpec((None, 1, tn), lambda i, j: (ctx_row, 0, j)),
        ],
        out_specs=pl.BlockSpec((tm, tn), lambda i, j: (i, j)),
        compiler_params=_cparams("parallel", "parallel"),
        name="out_proj",
    )(z, w_out, x, gate, gate)


def _ffn_kernel(h_ref, wg_ref, wu_ref, wd_ref, x_ref, gb_ref, gc_ref, o_ref, acc_ref, *, tm, tpb, seq):
    i, f = pl.program_id(0), pl.program_id(1)

    @pl.when(f == 0)
    def _():
        acc_ref[...] = jnp.zeros_like(acc_ref)

    h = h_ref[...]
    g = jnp.dot(h, wg_ref[...], preferred_element_type=F32)
    u = jnp.dot(h, wu_ref[...], preferred_element_type=F32)
    a = (_silu(g) * u).astype(BF16)
    acc_ref[...] += jnp.dot(a, wd_ref[...], preferred_element_type=F32)

    @pl.when(f == pl.num_programs(1) - 1)
    def _():
        gate = jnp.where(_ctx_rows(i, tm, tpb, seq), gc_ref[...], gb_ref[...])
        o_ref[...] = x_ref[...] + gate * acc_ref[...]


def ffn_residual(h, wg, wu, wd, x, gate, *, tm, tpb, seq, ctx_row, tf=512):
    m, d = h.shape
    ff = wg.shape[1]
    return pl.pallas_call(
        functools.partial(_ffn_kernel, tm=tm, tpb=tpb, seq=seq),
        out_shape=jax.ShapeDtypeStruct((m, d), F32),
        grid=(m // tm, ff // tf),
        in_specs=[
            pl.BlockSpec((tm, d), lambda i, f: (i, 0)),
            pl.BlockSpec((d, tf), lambda i, f: (0, f)),
            pl.BlockSpec((d, tf), lambda i, f: (0, f)),
            pl.BlockSpec((tf, d), lambda i, f: (f, 0)),
            pl.BlockSpec((tm, d), lambda i, f: (i, 0)),
            pl.BlockSpec((None, 1, d), lambda i, f: (i // tpb, 0, 0)),
            pl.BlockSpec((None, 1, d), lambda i, f: (ctx_row, 0, 0)),
        ],
        out_specs=pl.BlockSpec((tm, d), lambda i, f: (i, 0)),
        scratch_shapes=[pltpu.VMEM((tm, d), F32)],
        compiler_params=_cparams("parallel", "arbitrary"),
        name="ffn",
    )(h, wg, wu, wd, x, gate, gate)


def moe_plan(e1, e2, tm):
    t = e1.shape[0]
    p = 2 * t
    e = jnp.stack([e1, e2], axis=1).reshape(p)
    onehot = (e[:, None] == jnp.arange(N_EXPERTS, dtype=jnp.int32)[None, :]).astype(jnp.int32)
    csum = jnp.cumsum(onehot, axis=0)
    rank = jnp.take_along_axis(csum, e[:, None], axis=1)[:, 0] - 1
    cnt = csum[-1]
    gsz = (cnt + tm - 1) // tm * tm
    gend = jnp.cumsum(gsz)
    gstart = gend - gsz
    pos = gstart[e] + rank
    n_tiles = -(-p // tm) + N_EXPERTS
    pair_of = jnp.full((n_tiles * tm,), -1, jnp.int32).at[pos].set(jnp.arange(p, dtype=jnp.int32))
    tile_start = jnp.arange(n_tiles, dtype=jnp.int32) * tm
    tile_e = jnp.minimum(jnp.searchsorted(gend, tile_start, side="right").astype(jnp.int32), N_EXPERTS - 1)
    tile_valid = jnp.clip(gstart[tile_e] + cnt[tile_e] - tile_start, 0, tm).astype(jnp.int32)
    used = gend[-1] // tm
    tile_e = jnp.where(tile_start < gend[-1], tile_e, tile_e[jnp.maximum(used - 1, 0)])
    return pair_of, tile_e, tile_valid


def _moe_kernel(te_ref, tv_ref, src_ref, nsrc_ref, pdst_ref, h_hbm, wg_ref, wu_ref, wd_ref, pairs_hbm,
                xg_ref, xb_ref, acc_ref, ys_ref, gsem, ssem, *, tm, n_f, dump_row):
    i, f = pl.program_id(0), pl.program_id(1)
    n_i = pl.num_programs(0)
    has_rows = tv_ref[i] > 0
    slot = i % 2
    rows_per_step = tm // n_f

    def gather_copy(s, r, row):
        return pltpu.make_async_copy(h_hbm.at[pl.ds(row, 1)], xg_ref.at[s, pl.ds(r, 1)], gsem.at[s])

    def scatter_copy(r, row):
        return pltpu.make_async_copy(ys_ref.at[pl.ds(r, 1)], pairs_hbm.at[pl.ds(row, 1)], ssem)

    def wait_gather(s):
        def body(r, carry):
            gather_copy(s, r, 0).wait()
            return carry
        lax.fori_loop(0, tm, body, 0, unroll=8)

    def wait_scatter():
        def body(r, carry):
            scatter_copy(r, 0).wait()
            return carry
        lax.fori_loop(0, tm, body, 0, unroll=8)

    def start_step_dmas():
        base = f * rows_per_step
        for j in range(rows_per_step):
            r = base + j
            gather_copy(1 - slot, r, nsrc_ref[0, r]).start()
            scatter_copy(r, jnp.where(i == 0, dump_row + r, pdst_ref[0, r])).start()

    @pl.when((i == 0) & (f == 0))
    def _():
        ys_ref[...] = jnp.zeros_like(ys_ref)

        def body(r, carry):
            gather_copy(0, r, src_ref[0, r]).start()
            return carry
        lax.fori_loop(0, tm, body, 0, unroll=8)

    @pl.when(f == 0)
    def _():
        wait_gather(slot)

    @pl.when(has_rows)
    def _():
        @pl.when(f == 0)
        def _():
            xb_ref[...] = xg_ref[slot].astype(BF16)
            acc_ref[...] = jnp.zeros_like(acc_ref)

        start_step_dmas()
        x = xb_ref[...]
        g = jnp.dot(x, wg_ref[...], preferred_element_type=F32)
        u = jnp.dot(x, wu_ref[...], preferred_element_type=F32)
        a = (_silu(g) * u).astype(BF16)
        acc_ref[...] += jnp.dot(a, wd_ref[...], preferred_element_type=F32)

    @pl.when(jnp.logical_not(has_rows))
    def _():
        start_step_dmas()

    @pl.when(f == n_f - 1)
    def _():
        wait_scatter()

        @pl.when(has_rows)
        def _():
            ys_ref[...] = acc_ref[...]

        @pl.when(i == n_i - 1)
        def _():
            wait_gather(1 - slot)


def moe_experts(h, wg, wu, wd, plan, t, src_of_token, *, tm, tf):
    pair_of, tile_e, tile_valid = plan
    n_tiles = tile_e.shape[0]
    _, d, ff = wg.shape
    nf = ff // tf
    assert tm % nf == 0
    real = pair_of >= 0
    pr = jnp.maximum(pair_of, 0)
    dump = 2 * t + jnp.arange(n_tiles * tm, dtype=jnp.int32) % tm
    row_src = jnp.where(real, src_of_token(pr // 2), 0).reshape(n_tiles, 1, tm)
    row_dst = jnp.where(real, (pr % 2) * t + pr // 2, dump).reshape(n_tiles, 1, tm)
    fe = lambda i, f, te, tv: jnp.where(tv[i] > 0, f, nf - 1)
    rows_at = lambda off: pl.BlockSpec((None, 1, tm), lambda i, f, te, tv: (jnp.clip(i + off, 0, n_tiles - 1), 0, 0),
                                       memory_space=pltpu.SMEM)
    grid_spec = pltpu.PrefetchScalarGridSpec(
        num_scalar_prefetch=2,
        grid=(n_tiles, nf),
        in_specs=[
            rows_at(0), rows_at(1), rows_at(-1),
            pl.BlockSpec(memory_space=pl.ANY),
            pl.BlockSpec((None, d, tf), lambda i, f, te, tv: (te[i], 0, fe(i, f, te, tv))),
            pl.BlockSpec((None, d, tf), lambda i, f, te, tv: (te[i], 0, fe(i, f, te, tv))),
            pl.BlockSpec((None, tf, d), lambda i, f, te, tv: (te[i], fe(i, f, te, tv), 0)),
        ],
        out_specs=pl.BlockSpec(memory_space=pl.ANY),
        scratch_shapes=[pltpu.VMEM((2, tm, d), F32), pltpu.VMEM((tm, d), BF16), pltpu.VMEM((tm, d), F32),
                        pltpu.VMEM((tm, d), F32), pltpu.SemaphoreType.DMA((2,)), pltpu.SemaphoreType.DMA(())],
    )
    return pl.pallas_call(
        functools.partial(_moe_kernel, tm=tm, n_f=nf, dump_row=2 * t),
        out_shape=jax.ShapeDtypeStruct((2 * t + tm, d), F32),
        grid_spec=grid_spec,
        compiler_params=pltpu.CompilerParams(dimension_semantics=("arbitrary", "arbitrary"),
                                             vmem_limit_bytes=VMEM_LIMIT, disable_bounds_checks=True),
        name="moe_experts",
    )(tile_e, tile_valid, row_src, row_src, row_dst, h, wg, wu, wd)


def _combine_final_kernel(p0_ref, p1_ref, rec_ref, x_ref, g_ref, fg_ref, o_ref):
    rec = rec_ref[...]
    y = rec[:, 2:3] * p0_ref[...] + rec[:, 3:4] * p1_ref[...]
    o_ref[...] = _rms(x_ref[...] + g_ref[...] * y, fg_ref[...])


def _rope_tables(ang):
    c, s = jnp.cos(ang), jnp.sin(ang)
    return jnp.concatenate([c, c], axis=-1), jnp.concatenate([-s, s], axis=-1)


def _axial_angles(seq):
    rows = seq // GRID_W
    row = jnp.repeat(jnp.arange(rows), GRID_W).astype(F32)
    col = jnp.tile(jnp.arange(GRID_W), rows).astype(F32)
    n_freq = HEAD_DIM // 4
    inv = ROPE_THETA ** (-jnp.arange(n_freq, dtype=F32) / n_freq)
    return jnp.concatenate([row[:, None] * inv, col[:, None] * inv], axis=-1)


def _linear_angles(pos, dim):
    n_freq = dim // 2
    inv = ROPE_THETA ** (-jnp.arange(n_freq, dtype=F32) / n_freq)
    return pos.astype(F32)[:, None] * inv


def kernel(x, c, ctx, c_ctx, w_ada, b_ada, norm1_g, norm2_g, w_in, q_norm_g, k_norm_g, ret_decay_f, ret_decay_b, w_att_o, w_ret_o, w_out, ffn_w_gate, ffn_w_up, ffn_w_down, moe_router, moe_w_gate, moe_w_up, moe_w_down, final_g):
    b, seq, d = x.shape
    ctx_len = ctx.shape[1]
    depth = w_ada.shape[0]
    assert ctx_len == Q_TILE and seq % Q_TILE == 0 and seq % GRID_W == 0
    nt = seq + ctx_len
    m = b * nt
    tpb_big, tpb_ffn = 4, 8
    tm_big, tm_ffn = nt // tpb_big, nt // tpb_ffn
    tm_norm, tpb_norm = Q_TILE, nt // Q_TILE
    ctx_row = b
    nchunks = nt // RET_CHUNK
    nc_lat = seq // RET_CHUNK

    cos_a, sin_a = _rope_tables(_axial_angles(seq))
    cos_a = jnp.concatenate([cos_a, jnp.ones((ctx_len, HEAD_DIM), F32)], axis=0)
    sin_a = jnp.concatenate([sin_a, jnp.zeros((ctx_len, HEAD_DIM), F32)], axis=0)
    pos = jnp.concatenate([ctx_len + jnp.arange(seq), jnp.arange(ctx_len)])
    cos_r, sin_r = _rope_tables(_linear_angles(pos, RET_DK))

    n_rows = -(-(b + 1) // 16) * 16
    cc = jnp.zeros((n_rows, d), F32).at[:b].set(c).at[b].set(c_ctx)
    mods = adaln_all(cc, w_ada, b_ada)
    mods = mods.reshape(depth, n_rows, N_MOD, d).transpose(0, 2, 1, 3).reshape(depth, N_MOD, n_rows, 1, d)

    xa = jnp.concatenate([x, ctx], axis=1).reshape(m, d)

    def norm_mod(xcur, g, sc, sh):
        sb = pl.BlockSpec((None, 1, d), lambda i: (i // tpb_norm, 0, 0))
        scx = pl.BlockSpec((None, 1, d), lambda i: (ctx_row, 0, 0))
        row = pl.BlockSpec((tm_norm, d), lambda i: (i, 0))
        return pl.pallas_call(
            functools.partial(_norm_mod_kernel, tm=tm_norm, tpb=tpb_norm, seq=seq),
            out_shape=jax.ShapeDtypeStruct((m, d), BF16),
            grid=(m // tm_norm,),
            in_specs=[row, pl.BlockSpec((1, d), lambda i: (0, 0)), sb, scx, sb, scx],
            out_specs=row,
            compiler_params=_cparams("parallel"),
            name="norm_mod",
        )(xcur, g.reshape(1, d), sc, sc, sh, sh)

    tok_tab = lambda t: (t, (tm_big, HEAD_DIM), lambda i, j: (i % tpb_big, 0))
    row_vec = lambda v: (v.reshape(1, -1), (1, v.shape[-1]), lambda i, j: (0, 0))
    rope_scaled = functools.partial(_ep_rope, scale=RET_DK ** -0.5)
    rope_plain = functools.partial(_ep_rope, scale=1.0)
    fwd_chunk = lambda t: (t + nc_lat) % nchunks
    bwd_chunk = lambda t: nchunks - 1 - t

    c0 = 0
    c_q, c0 = c0, c0 + ATT_WIDTH
    c_k, c0 = c0, c0 + KV_WIDTH
    c_v, c0 = c0, c0 + KV_WIDTH
    c_rq, c0 = c0, c0 + RET_QK_WIDTH
    c_rk, c0 = c0, c0 + RET_QK_WIDTH
    c_rv, c0 = c0, c0 + RET_V_WIDTH
    c_rg, c0 = c0, c0 + 2 * RET_V_WIDTH
    c_bg, c0 = c0, c0 + 2 * d

    for layer in range(depth):
        last = layer == depth - 1
        sh1, sc1, g1, sh2, sc2, g2 = [mods[layer, n] for n in range(N_MOD)]
        w = w_in[layer].astype(BF16)
        tn_in = 512 if d % 512 == 0 else 256

        h = norm_mod(xa, norm1_g[layer], sc1, sh1)
        pj = functools.partial(proj, h, w, tm=tm_big, tn=tn_in)
        rc = 4 if tm_big % 64 == 0 else 1
        q = pj(c_q, ATT_WIDTH, BF16, _ep_head_norm_rope, [row_vec(q_norm_g[layer]), tok_tab(cos_a), tok_tab(sin_a)], row_chunks=rc, name="proj_q")
        k = pj(c_k, KV_WIDTH, BF16, _ep_head_norm_rope, [row_vec(k_norm_g[layer]), tok_tab(cos_a), tok_tab(sin_a)], row_chunks=rc, name="proj_k")
        v = pj(c_v, KV_WIDTH, BF16, _ep_plain, name="proj_v")
        rq = pj(c_rq, RET_QK_WIDTH, BF16, rope_plain, [tok_tab(cos_r), tok_tab(sin_r)], row_chunks=rc, name="proj_rq")
        rk = pj(c_rk, RET_QK_WIDTH, BF16, rope_scaled, [tok_tab(cos_r), tok_tab(sin_r)], row_chunks=rc, name="proj_rk")
        rv = pj(c_rv, RET_V_WIDTH, BF16, _ep_plain, name="proj_rv")
        rg = pj(c_rg, 2 * RET_V_WIDTH, F32, _ep_plain, name="proj_rg")
        gsig = pj(c_bg, 2 * d, F32, _ep_sigmoid, row_chunks=rc, name="proj_gates")

        r3 = lambda a: a.reshape(b, nt, a.shape[-1])
        att = attention(r3(q), r3(k), r3(v), seq=seq, with_ctx=not last).reshape(m, ATT_WIDTH)
        tf_ = retention_tables(ret_decay_f[layer], backward=False)
        tb_ = retention_tables(ret_decay_b[layer], backward=True)
        yf = retention_dir(r3(rq), r3(rk), r3(rv), r3(rg), 0, tf_, fwd_chunk, None, F32, name="retention_fwd")
        ret = retention_dir(r3(rq), r3(rk), r3(rv), r3(rg), 1, tb_, bwd_chunk, yf, BF16, name="retention_bwd").reshape(m, RET_V_WIDTH)

        z = branch_merge(att, ret, w_att_o[layer].astype(BF16), w_ret_o[layer].astype(BF16), gsig, tm=tm_big, tn=tn_in)
        xa = out_proj_residual(z, w_out[layer].astype(BF16), xa, g1, tm=tm_big, tpb=tpb_big, seq=seq, ctx_row=ctx_row, tn=tn_in)

        j = layer // 2
        if layer % 2 == 0:
            h2 = norm_mod(xa, norm2_g[layer], sc2, sh2)
            xa = ffn_residual(h2, ffn_w_gate[j].astype(BF16), ffn_w_up[j].astype(BF16), ffn_w_down[j].astype(BF16),
                              xa, g2, tm=tm_ffn, tpb=tpb_ffn, seq=seq, ctx_row=ctx_row)
        else:
            assert last, "routed-expert layers before the last layer are not supported"
            wr = jnp.zeros((d, LANES), F32).at[:, :N_EXPERTS].set(moe_router[j])
            sb = pl.BlockSpec((None, 1, d), lambda i: (i // tpb_norm, 0, 0))
            scx = pl.BlockSpec((None, 1, d), lambda i: (ctx_row, 0, 0))
            row = pl.BlockSpec((tm_norm, d), lambda i: (i, 0))
            h2, rec = pl.pallas_call(
                functools.partial(_norm_mod_router_kernel, tm=tm_norm, tpb=tpb_norm, seq=seq),
                out_shape=(jax.ShapeDtypeStruct((m, d), F32), jax.ShapeDtypeStruct((m, LANES), F32)),
                grid=(m // tm_norm,),
                in_specs=[row, pl.BlockSpec((1, d), lambda i: (0, 0)), sb, scx, sb, scx, pl.BlockSpec((d, LANES), lambda i: (0, 0))],
                out_specs=(row, pl.BlockSpec((tm_norm, LANES), lambda i: (i, 0))),
                compiler_params=_cparams("parallel"),
                name="norm_mod_router",
            )(xa, norm2_g[layer].reshape(1, d), sc2, sc2, sh2, sh2, wr)
            e12 = rec.reshape(b, nt, LANES)[:, :seq, :2].reshape(b * seq, 2).astype(jnp.int32)
            t_tok = b * seq
            eff = moe_w_gate.shape[-1]
            tf_moe = 1024 if eff % 1024 == 0 else eff
            row_step = int(np.lcm(eff // tf_moe, 16))
            tm_moe = 512 // row_step * row_step
            plan = moe_plan(e12[:, 0], e12[:, 1], tm_moe)
            pairs = moe_experts(h2, moe_w_gate[j].astype(BF16), moe_w_up[j].astype(BF16), moe_w_down[j].astype(BF16),
                                plan, t_tok, lambda tok: (tok // seq) * nt + tok % seq, tm=tm_moe, tf=tf_moe)
            tmc = Q_TILE
            spb = seq // tmc
            return pl.pallas_call(
                _combine_final_kernel,
                out_shape=jax.ShapeDtypeStruct((b, seq, d), F32),
                grid=(b, spb),
                in_specs=[
                    pl.BlockSpec((tmc, d), lambda bi, i: (bi * spb + i, 0)),
                    pl.BlockSpec((tmc, d), lambda bi, i: (t_tok // tmc + bi * spb + i, 0)),
                    pl.BlockSpec((tmc, LANES), lambda bi, i: (bi * (nt // tmc) + i, 0)),
                    pl.BlockSpec((None, tmc, d), lambda bi, i: (bi, i, 0)),
                    pl.BlockSpec((None, 1, d), lambda bi, i: (bi, 0, 0)),
                    pl.BlockSpec((1, d), lambda bi, i: (0, 0)),
                ],
                out_specs=pl.BlockSpec((None, tmc, d), lambda bi, i: (bi, i, 0)),
                compiler_params=_cparams("parallel", "parallel"),
                name="combine_final",
            )(pairs, pairs, rec, xa.reshape(b, nt, d), g2, final_g.reshape(1, d))

    tm_f = Q_TILE
    return pl.pallas_call(
        _final_norm_kernel,
        out_shape=jax.ShapeDtypeStruct((b, seq, d), F32),
        grid=(b, seq // tm_f),
        in_specs=[pl.BlockSpec((None, tm_f, d), lambda bi, i: (bi, i, 0)), pl.BlockSpec((1, d), lambda bi, i: (0, 0))],
        out_specs=pl.BlockSpec((None, tm_f, d), lambda bi, i: (bi, i, 0)),
        compiler_params=_cparams("parallel", "parallel"),
        name="final_norm",
    )(xa.reshape(b, nt, d), final_g.reshape(1, d))
```

```python
import functools

import jax
import jax.numpy as jnp
import numpy as np
from jax import lax
from jax.experimental import pallas as pl
from jax.experimental.pallas import tpu as pltpu

F32 = jnp.float32
BF16 = jnp.bfloat16

GRID_W = 64
N_HEADS = 16
N_KV_HEADS = 4
HEAD_DIM = 128
KV_GROUP = N_HEADS // N_KV_HEADS
ATT_WIDTH = N_HEADS * HEAD_DIM
KV_WIDTH = N_KV_HEADS * HEAD_DIM
ROPE_THETA = 10000.0
RET_HEADS = 8
RET_DK = 128
RET_DV = 256
RET_QK_WIDTH = RET_HEADS * RET_DK
RET_V_WIDTH = RET_HEADS * RET_DV
RET_CHUNK = 128
N_EXPERTS = 8
N_MOD = 6
EPS = 1e-6
GN_EPS = 1e-5

Q_TILE = 256
LANES = 128
VMEM_LIMIT = 56 * 2**20


def _cparams(*sem):
    return pltpu.CompilerParams(dimension_semantics=sem, vmem_limit_bytes=VMEM_LIMIT)


def _silu(v):
    return v * jax.nn.sigmoid(v)


def _adaln_kernel(c_ref, w_ref, b_ref, o_ref):
    s = _silu(c_ref[...]).astype(BF16)
    o_ref[...] = jnp.dot(s, w_ref[...].astype(BF16), preferred_element_type=F32) + b_ref[...]


def adaln_all(cc, w_ada, b_ada):
    depth, d, n = w_ada.shape
    r = cc.shape[0]
    tn = 1024 if n % 1024 == 0 else n
    return pl.pallas_call(
        _adaln_kernel,
        out_shape=jax.ShapeDtypeStruct((depth, r, n), F32),
        grid=(depth, n // tn),
        in_specs=[
            pl.BlockSpec((r, d), lambda l, j: (0, 0)),
            pl.BlockSpec((None, d, tn), lambda l, j: (l, 0, j)),
            pl.BlockSpec((None, 1, tn), lambda l, j: (l, 0, j)),
        ],
        out_specs=pl.BlockSpec((None, r, tn), lambda l, j: (l, 0, j)),
        compiler_params=_cparams("parallel", "parallel"),
        name="adaln",
    )(cc, w_ada, b_ada.reshape(depth, 1, n))


def _ctx_rows(i, tm, tiles_per_batch, seq):
    rows = lax.broadcasted_iota(jnp.int32, (tm, 1), 0) + (i % tiles_per_batch) * tm
    return rows >= seq


def _rms(x, g):
    return x * lax.rsqrt(jnp.mean(x * x, axis=-1, keepdims=True) + EPS) * g


def _norm_mod_kernel(x_ref, g_ref, scb_ref, scc_ref, shb_ref, shc_ref, o_ref, *, tm, tpb, seq):
    i = pl.program_id(0)
    is_ctx = _ctx_rows(i, tm, tpb, seq)
    y = _rms(x_ref[...], g_ref[...])
    scale = jnp.where(is_ctx, scc_ref[...], scb_ref[...])
    shift = jnp.where(is_ctx, shc_ref[...], shb_ref[...])
    o_ref[...] = (y * (1 + scale) + shift).astype(o_ref.dtype)


def _norm_mod_router_kernel(x_ref, g_ref, scb_ref, scc_ref, shb_ref, shc_ref, wr_ref, o_ref, gate_ref, *, tm, tpb, seq):
    is_ctx = _ctx_rows(pl.program_id(0), tm, tpb, seq)
    y = _rms(x_ref[...], g_ref[...])
    scale = jnp.where(is_ctx, scc_ref[...], scb_ref[...])
    shift = jnp.where(is_ctx, shc_ref[...], shb_ref[...])
    t = y * (1 + scale) + shift
    o_ref[...] = t.astype(o_ref.dtype)
    logits = jnp.dot(t, wr_ref[...], preferred_element_type=F32, precision=lax.Precision.HIGHEST)
    lane = lax.broadcasted_iota(jnp.int32, logits.shape, 1)
    neg = jnp.float32(-jnp.inf)
    logits = jnp.where(lane < N_EXPERTS, logits, neg)
    m1 = jnp.max(logits, axis=-1, keepdims=True)
    i1 = jnp.min(jnp.where(logits == m1, lane, LANES), axis=-1, keepdims=True)
    rest = jnp.where(lane == i1, neg, logits)
    m2 = jnp.max(rest, axis=-1, keepdims=True)
    i2 = jnp.min(jnp.where(rest == m2, lane, LANES), axis=-1, keepdims=True)
    e2 = jnp.exp(m2 - m1)
    w1 = 1.0 / (1.0 + e2)
    w2 = e2 / (1.0 + e2)
    rec = jnp.where(lane == 0, i1.astype(F32), jnp.where(lane == 1, i2.astype(F32), jnp.where(lane == 2, w1, w2)))
    gate_ref[...] = jnp.where(lane < 4, rec, 0.0)


def _final_norm_kernel(x_ref, g_ref, o_ref):
    o_ref[...] = _rms(x_ref[...], g_ref[...])


def _mm_kernel(x_ref, w_ref, *rest, epilogue, row_chunks):
    o_ref = rest[-1]
    tm = x_ref.shape[0]
    rows = tm // row_chunks
    for c in range(row_chunks):
        rs = slice(c * rows, (c + 1) * rows)
        acc = jnp.dot(x_ref[rs, :], w_ref[...], preferred_element_type=F32)
        extras = [r[rs, :] if r.shape[0] == tm else r[...] for r in rest[:-1]]
        o_ref[rs, :] = epilogue(acc, *extras).astype(o_ref.dtype)


def _rope(y, cos, sin):
    return y * cos + pltpu.roll(y, HEAD_DIM // 2, 1) * sin


def _ep_plain(acc):
    return acc


def _ep_sigmoid(acc):
    return jax.nn.sigmoid(acc)


def _ep_head_norm_rope(acc, g, cos, sin):
    outs = []
    for h in range(acc.shape[1] // HEAD_DIM):
        y = _rms(acc[:, h * HEAD_DIM:(h + 1) * HEAD_DIM], g)
        outs.append(_rope(y, cos, sin))
    return jnp.concatenate(outs, axis=1)


def _ep_rope(acc, cos, sin, *, scale):
    outs = []
    for h in range(acc.shape[1] // RET_DK):
        y = _rope(acc[:, h * RET_DK:(h + 1) * RET_DK], cos, sin)
        outs.append(y * scale if scale != 1.0 else y)
    return jnp.concatenate(outs, axis=1)


def proj(h, w, col0, ncols, out_dtype, epilogue, extras=(), *, tm, tn=512, row_chunks=1, name):
    m, k = h.shape
    assert col0 % tn == 0 and ncols % tn == 0 and m % tm == 0 and tm % (16 * row_chunks) == 0
    jb = col0 // tn
    in_specs = [
        pl.BlockSpec((tm, k), lambda i, j: (i, 0)),
        pl.BlockSpec((k, tn), lambda i, j: (0, jb + j)),
    ]
    args = [h, w]
    for arr, bs, im in extras:
        in_specs.append(pl.BlockSpec(bs, im))
        args.append(arr)
    return pl.pallas_call(
        functools.partial(_mm_kernel, epilogue=epilogue, row_chunks=row_chunks),
        out_shape=jax.ShapeDtypeStruct((m, ncols), out_dtype),
        grid=(m // tm, ncols // tn),
        in_specs=in_specs,
        out_specs=pl.BlockSpec((tm, tn), lambda i, j: (i, j)),
        compiler_params=_cparams("parallel", "parallel"),
        name=name,
    )(*args)


def _attn_kernel(q_ref, k_ref, v_ref, o_ref, v1_ref, *, seq, n_lat_tiles, with_ctx):
    qi = pl.program_id(2)
    c = HEAD_DIM ** -0.5 * np.log2(np.e)

    @pl.when(qi == 0)
    def _():
        v1_ref[:, :HEAD_DIM] = v_ref[...]
        v1_ref[:, HEAD_DIM:] = jnp.ones(v_ref.shape, v1_ref.dtype)

    def run(k, v1):
        for g in range(KV_GROUP):
            q = q_ref[:, g * HEAD_DIM:(g + 1) * HEAD_DIM]
            s = lax.dot_general(q, k, (((1,), (1,)), ((), ())), preferred_element_type=F32)
            m = jnp.max(s, axis=-1, keepdims=True)
            p = jnp.exp2((s - m) * c).astype(BF16)
            o = jnp.dot(p, v1, preferred_element_type=F32)
            o_ref[:, g * HEAD_DIM:(g + 1) * HEAD_DIM] = (o[:, :HEAD_DIM] / o[:, HEAD_DIM:]).astype(o_ref.dtype)

    @pl.when(qi < n_lat_tiles)
    def _():
        run(k_ref[...], v1_ref[...])

    @pl.when(qi >= n_lat_tiles)
    def _():
        if with_ctx:
            run(k_ref[seq:, :], v1_ref[seq:, :])
        else:
            o_ref[...] = jnp.zeros_like(o_ref)


def attention(q, k, v, *, seq, with_ctx):
    b, nt, _ = q.shape
    n_lat = seq // Q_TILE
    n_tiles = nt // Q_TILE
    gw = KV_GROUP * HEAD_DIM
    return pl.pallas_call(
        functools.partial(_attn_kernel, seq=seq, n_lat_tiles=n_lat, with_ctx=with_ctx),
        out_shape=jax.ShapeDtypeStruct((b, nt, ATT_WIDTH), BF16),
        grid=(b, N_KV_HEADS, n_tiles),
        in_specs=[
            pl.BlockSpec((None, Q_TILE, gw), lambda bi, h, i: (bi, i, h)),
            pl.BlockSpec((None, nt, HEAD_DIM), lambda bi, h, i: (bi, 0, h)),
            pl.BlockSpec((None, nt, HEAD_DIM), lambda bi, h, i: (bi, 0, h)),
        ],
        out_specs=pl.BlockSpec((None, Q_TILE, gw), lambda bi, h, i: (bi, i, h)),
        scratch_shapes=[pltpu.VMEM((nt, 2 * HEAD_DIM), BF16)],
        compiler_params=_cparams("parallel", "parallel", "arbitrary"),
        name="attention",
    )(q, k, v)


def _retention_kernel(*refs, has_prev):
    if has_prev:
        q_ref, k_ref, v_ref, g_ref, dec_ref, xi_ref, zeta_ref, cd_ref, prev_ref, o_ref, r_ref = refs
    else:
        q_ref, k_ref, v_ref, g_ref, dec_ref, xi_ref, zeta_ref, cd_ref, o_ref, r_ref = refs

    @pl.when(pl.program_id(1) == 0)
    def _():
        r_ref[...] = jnp.zeros_like(r_ref)

    for bi in range(q_ref.shape[0]):
        for h in range(RET_HEADS):
            ks = slice(h * RET_DK, (h + 1) * RET_DK)
            vs = slice(h * RET_DV, (h + 1) * RET_DV)
            q = q_ref[bi, :, ks]
            k = k_ref[bi, :, ks]
            v = v_ref[bi, :, vs]
            state = r_ref[bi, h]
            inner = lax.dot_general(q, k, (((1,), (1,)), ((), ())), preferred_element_type=F32) * dec_ref[h]
            o = jnp.dot(inner.astype(BF16), v, preferred_element_type=F32)
            o = o + jnp.dot(q, state.astype(BF16), preferred_element_type=F32) * xi_ref[h]
            kz = (k.astype(F32) * zeta_ref[h]).astype(BF16)
            r_ref[bi, h] = state * cd_ref[h] + lax.dot_general(kz, v, (((0,), (0,)), ((), ())), preferred_element_type=F32)
            mu = jnp.mean(o, axis=-1, keepdims=True)
            d = o - mu
            var = jnp.mean(d * d, axis=-1, keepdims=True)
            y = _silu(g_ref[bi, :, vs]) * (d * lax.rsqrt(var + GN_EPS))
            if has_prev:
                y = prev_ref[bi, :, vs] + y
            o_ref[bi, :, vs] = y.astype(o_ref.dtype)


def retention_dir(rq, rk, rv, rg, gate_col, tables, chunk_of, prev, out_dtype, *, name):
    b, nt, _ = rq.shape
    nc = nt // RET_CHUNK
    L = RET_CHUNK
    bb = next(n for n in (4, 2, 1) if b % n == 0)
    tok = lambda w: pl.BlockSpec((bb, L, w), lambda bi, t: (bi, chunk_of(t), 0))
    full = lambda a: pl.BlockSpec(a.shape, lambda bi, t: (0,) * a.ndim)
    in_specs = [tok(RET_QK_WIDTH), tok(RET_QK_WIDTH), tok(RET_V_WIDTH),
                pl.BlockSpec((bb, L, RET_V_WIDTH), lambda bi, t: (bi, chunk_of(t), gate_col))]
    in_specs += [full(a) for a in tables]
    args = [rq, rk, rv, rg, *tables]
    if prev is not None:
        in_specs.append(tok(RET_V_WIDTH))
        args.append(prev)
    return pl.pallas_call(
        functools.partial(_retention_kernel, has_prev=prev is not None),
        out_shape=jax.ShapeDtypeStruct((b, nt, RET_V_WIDTH), out_dtype),
        grid=(b // bb, nc),
        in_specs=in_specs,
        out_specs=tok(RET_V_WIDTH),
        scratch_shapes=[pltpu.VMEM((bb, RET_HEADS, RET_DK, RET_DV), F32)],
        compiler_params=_cparams("parallel", "arbitrary"),
        name=name,
    )(*args)


def retention_tables(decay_logit, backward):
    L = RET_CHUNK
    lg = jax.nn.log_sigmoid(decay_logit.astype(F32))
    pos = jnp.arange(L, dtype=F32)
    diff = pos[:, None] - pos[None, :]
    if backward:
        diff = -diff
        xi_e = L - pos
        zeta_e = pos
    else:
        xi_e = pos + 1.0
        zeta_e = L - 1.0 - pos
    decay = jnp.where(diff >= 0, jnp.exp(jnp.maximum(diff, 0.0) * lg[:, None, None]), 0.0)
    xi = jnp.broadcast_to(jnp.exp(xi_e * lg[:, None])[:, :, None], (RET_HEADS, L, RET_DV))
    zeta = jnp.broadcast_to(jnp.exp(zeta_e * lg[:, None])[:, :, None], (RET_HEADS, L, RET_DK))
    cd = jnp.broadcast_to(jnp.exp(L * lg)[:, None, None], (RET_HEADS, 1, RET_DV))
    return decay, xi, zeta, cd


def _merge_kernel(a_ref, r_ref, wa_ref, wr_ref, ga_ref, gr_ref, o_ref):
    za = jnp.dot(a_ref[...], wa_ref[...], preferred_element_type=F32)
    zr = jnp.dot(r_ref[...], wr_ref[...], preferred_element_type=F32)
    o_ref[...] = (ga_ref[...] * za + gr_ref[...] * zr).astype(o_ref.dtype)


def branch_merge(att, ret, w_att_o, w_ret_o, gsig, *, tm, tn=512):
    m, ka = att.shape
    kr = ret.shape[1]
    d = w_att_o.shape[1]
    nj = d // tn
    return pl.pallas_call(
        _merge_kernel,
        out_shape=jax.ShapeDtypeStruct((m, d), BF16),
        grid=(m // tm, nj),
        in_specs=[
            pl.BlockSpec((tm, ka), lambda i, j: (i, 0)),
            pl.BlockSpec((tm, kr), lambda i, j: (i, 0)),
            pl.BlockSpec((ka, tn), lambda i, j: (0, j)),
            pl.BlockSpec((kr, tn), lambda i, j: (0, j)),
            pl.BlockSpec((tm, tn), lambda i, j: (i, j)),
            pl.BlockSpec((tm, tn), lambda i, j: (i, nj + j)),
        ],
        out_specs=pl.BlockSpec((tm, tn), lambda i, j: (i, j)),
        compiler_params=_cparams("parallel", "parallel"),
        name="branch_merge",
    )(att, ret, w_att_o, w_ret_o, gsig, gsig)


def _resid_mm_kernel(z_ref, w_ref, x_ref, gb_ref, gc_ref, o_ref, *, tm, tpb, seq):
    i = pl.program_id(0)
    gate = jnp.where(_ctx_rows(i, tm, tpb, seq), gc_ref[...], gb_ref[...])
    o_ref[...] = x_ref[...] + gate * jnp.dot(z_ref[...], w_ref[...], preferred_element_type=F32)


def out_proj_residual(z, w_out, x, gate, *, tm, tpb, seq, ctx_row, tn=512):
    m, k = z.shape
    d = w_out.shape[1]
    return pl.pallas_call(
        functools.partial(_resid_mm_kernel, tm=tm, tpb=tpb, seq=seq),
        out_shape=jax.ShapeDtypeStruct((m, d), F32),
        grid=(m // tm, d // tn),
        in_specs=[
            pl.BlockSpec((tm, k), lambda i, j: (i, 0)),
            pl.BlockSpec((k, tn), lambda i, j: (0, j)),
            pl.BlockSpec((tm, tn), lambda i, j: (i, j)),
            pl.BlockSpec((None, 1, tn), lambda i, j: (i // tpb, 0, j)),
            pl.BlockSpec((None, 1, tn), lambda i, j: (ctx_row, 0, j)),
        ],
        out_specs=pl.BlockSpec((tm, tn), lambda i, j: (i, j)),
        compiler_params=_cparams("parallel", "parallel"),
        name="out_proj",
    )(z, w_out, x, gate, gate)


def _ffn_kernel(h_ref, wg_ref, wu_ref, wd_ref, x_ref, gb_ref, gc_ref, o_ref, acc_ref, *, tm, tpb, seq):
    i, f = pl.program_id(0), pl.program_id(1)

    @pl.when(f == 0)
    def _():
        acc_ref[...] = jnp.zeros_like(acc_ref)

    h = h_ref[...]
    g = jnp.dot(h, wg_ref[...], preferred_element_type=F32)
    u = jnp.dot(h, wu_ref[...], preferred_element_type=F32)
    a = (_silu(g) * u).astype(BF16)
    acc_ref[...] += jnp.dot(a, wd_ref[...], preferred_element_type=F32)

    @pl.when(f == pl.num_programs(1) - 1)
    def _():
        gate = jnp.where(_ctx_rows(i, tm, tpb, seq), gc_ref[...], gb_ref[...])
        o_ref[...] = x_ref[...] + gate * acc_ref[...]


def ffn_residual(h, wg, wu, wd, x, gate, *, tm, tpb, seq, ctx_row, tf=512):
    m, d = h.shape
    ff = wg.shape[1]
    return pl.pallas_call(
        functools.partial(_ffn_kernel, tm=tm, tpb=tpb, seq=seq),
        out_shape=jax.ShapeDtypeStruct((m, d), F32),
        grid=(m // tm, ff // tf),
        in_specs=[
            pl.BlockSpec((tm, d), lambda i, f: (i, 0)),
            pl.BlockSpec((d, tf), lambda i, f: (0, f)),
            pl.BlockSpec((d, tf), lambda i, f: (0, f)),
            pl.BlockSpec((tf, d), lambda i, f: (f, 0)),
            pl.BlockSpec((tm, d), lambda i, f: (i, 0)),
            pl.BlockSpec((None, 1, d), lambda i, f: (i // tpb, 0, 0)),
            pl.BlockSpec((None, 1, d), lambda i, f: (ctx_row, 0, 0)),
        ],
        out_specs=pl.BlockSpec((tm, d), lambda i, f: (i, 0)),
        scratch_shapes=[pltpu.VMEM((tm, d), F32)],
        compiler_params=_cparams("parallel", "arbitrary"),
        name="ffn",
    )(h, wg, wu, wd, x, gate, gate)


def moe_plan(e1, e2, tm):
    t = e1.shape[0]
    p = 2 * t
    e = jnp.stack([e1, e2], axis=1).reshape(p)
    onehot = (e[:, None] == jnp.arange(N_EXPERTS, dtype=jnp.int32)[None, :]).astype(jnp.int32)
    csum = jnp.cumsum(onehot, axis=0)
    rank = jnp.take_along_axis(csum, e[:, None], axis=1)[:, 0] - 1
    cnt = csum[-1]
    gsz = (cnt + tm - 1) // tm * tm
    gend = jnp.cumsum(gsz)
    gstart = gend - gsz
    pos = gstart[e] + rank
    n_tiles = -(-p // tm) + N_EXPERTS
    pair_of = jnp.full((n_tiles * tm,), -1, jnp.int32).at[pos].set(jnp.arange(p, dtype=jnp.int32))
    tile_start = jnp.arange(n_tiles, dtype=jnp.int32) * tm
    tile_e = jnp.minimum(jnp.searchsorted(gend, tile_start, side="right").astype(jnp.int32), N_EXPERTS - 1)
    tile_valid = jnp.clip(gstart[tile_e] + cnt[tile_e] - tile_start, 0, tm).astype(jnp.int32)
    used = gend[-1] // tm
    tile_e = jnp.where(tile_start < gend[-1], tile_e, tile_e[jnp.maximum(used - 1, 0)])
    return pair_of, tile_e, tile_valid


def _moe_kernel(te_ref, tv_ref, src_ref, nsrc_ref, pdst_ref, h_hbm, wg_ref, wu_ref, wd_ref, pairs_hbm,
                xg_ref, xb_ref, acc_ref, ys_ref, gsem, ssem, *, tm, n_f, dump_row):
    i, f = pl.program_id(0), pl.program_id(1)
    n_i = pl.num_programs(0)
    has_rows = tv_ref[i] > 0
    slot = i % 2
    rows_per_step = tm // n_f

    def gather_copy(s, r, row):
        return pltpu.make_async_copy(h_hbm.at[pl.ds(row, 1)], xg_ref.at[s, pl.ds(r, 1)], gsem.at[s])

    def scatter_copy(r, row):
        return pltpu.make_async_copy(ys_ref.at[pl.ds(r, 1)], pairs_hbm.at[pl.ds(row, 1)], ssem)

    def wait_gather(s):
        def body(r, carry):
            gather_copy(s, r, 0).wait()
            return carry
        lax.fori_loop(0, tm, body, 0, unroll=8)

    def wait_scatter():
        def body(r, carry):
            scatter_copy(r, 0).wait()
            return carry
        lax.fori_loop(0, tm, body, 0, unroll=8)

    def start_step_dmas():
        base = f * rows_per_step
        for j in range(rows_per_step):
            r = base + j
            gather_copy(1 - slot, r, nsrc_ref[0, r]).start()
            scatter_copy(r, jnp.where(i == 0, dump_row + r, pdst_ref[0, r])).start()

    @pl.when((i == 0) & (f == 0))
    def _():
        ys_ref[...] = jnp.zeros_like(ys_ref)

        def body(r, carry):
            gather_copy(0, r, src_ref[0, r]).start()
            return carry
        lax.fori_loop(0, tm, body, 0, unroll=8)

    @pl.when(f == 0)
    def _():
        wait_gather(slot)

    @pl.when(has_rows)
    def _():
        @pl.when(f == 0)
        def _():
            xb_ref[...] = xg_ref[slot].astype(BF16)
            acc_ref[...] = jnp.zeros_like(acc_ref)

        start_step_dmas()
        x = xb_ref[...]
        g = jnp.dot(x, wg_ref[...].astype(BF16), preferred_element_type=F32)
        u = jnp.dot(x, wu_ref[...].astype(BF16), preferred_element_type=F32)
        a = (_silu(g) * u).astype(BF16)
        acc_ref[...] += jnp.dot(a, wd_ref[...].astype(BF16), preferred_element_type=F32)

    @pl.when(jnp.logical_not(has_rows))
    def _():
        start_step_dmas()

    @pl.when(f == n_f - 1)
    def _():
        wait_scatter()

        @pl.when(has_rows)
        def _():
            ys_ref[...] = acc_ref[...]

        @pl.when(i == n_i - 1)
        def _():
            wait_gather(1 - slot)


def moe_experts(h, wg, wu, wd, plan, t, src_of_token, *, tm, tf):
    pair_of, tile_e, tile_valid = plan
    n_tiles = tile_e.shape[0]
    _, d, ff = wg.shape
    nf = ff // tf
    assert tm % nf == 0
    real = pair_of >= 0
    pr = jnp.maximum(pair_of, 0)
    dump = 2 * t + jnp.arange(n_tiles * tm, dtype=jnp.int32) % tm
    row_src = jnp.where(real, src_of_token(pr // 2), 0).reshape(n_tiles, 1, tm)
    row_dst = jnp.where(real, (pr % 2) * t + pr // 2, dump).reshape(n_tiles, 1, tm)
    fe = lambda i, f, te, tv: jnp.where(tv[i] > 0, f, nf - 1)
    rows_at = lambda off: pl.BlockSpec((None, 1, tm), lambda i, f, te, tv: (jnp.clip(i + off, 0, n_tiles - 1), 0, 0),
                                       memory_space=pltpu.SMEM)
    grid_spec = pltpu.PrefetchScalarGridSpec(
        num_scalar_prefetch=2,
        grid=(n_tiles, nf),
        in_specs=[
            rows_at(0), rows_at(1), rows_at(-1),
            pl.BlockSpec(memory_space=pl.ANY),
            pl.BlockSpec((None, d, tf), lambda i, f, te, tv: (te[i], 0, fe(i, f, te, tv))),
            pl.BlockSpec((None, d, tf), lambda i, f, te, tv: (te[i], 0, fe(i, f, te, tv))),
            pl.BlockSpec((None, tf, d), lambda i, f, te, tv: (te[i], fe(i, f, te, tv), 0)),
        ],
        out_specs=pl.BlockSpec(memory_space=pl.ANY),
        scratch_shapes=[pltpu.VMEM((2, tm, d), F32), pltpu.VMEM((tm, d), BF16), pltpu.VMEM((tm, d), F32),
                        pltpu.VMEM((tm, d), F32), pltpu.SemaphoreType.DMA((2,)), pltpu.SemaphoreType.DMA(())],
    )
    return pl.pallas_call(
        functools.partial(_moe_kernel, tm=tm, n_f=nf, dump_row=2 * t),
        out_shape=jax.ShapeDtypeStruct((2 * t + tm, d), F32),
        grid_spec=grid_spec,
        compiler_params=pltpu.CompilerParams(dimension_semantics=("arbitrary", "arbitrary"),
                                             vmem_limit_bytes=VMEM_LIMIT, disable_bounds_checks=True),
        name="moe_experts",
    )(tile_e, tile_valid, row_src, row_src, row_dst, h, wg, wu, wd)


def _combine_final_kernel(p0_ref, p1_ref, rec_ref, x_ref, g_ref, fg_ref, o_ref):
    rec = rec_ref[...]
    y = rec[:, 2:3] * p0_ref[...] + rec[:, 3:4] * p1_ref[...]
    o_ref[...] = _rms(x_ref[...] + g_ref[...] * y, fg_ref[...])


def _rope_tables(ang):
    c, s = jnp.cos(ang), jnp.sin(ang)
    return jnp.concatenate([c, c], axis=-1), jnp.concatenate([-s, s], axis=-1)


def _axial_angles(seq):
    rows = seq // GRID_W
    row = jnp.repeat(jnp.arange(rows), GRID_W).astype(F32)
    col = jnp.tile(jnp.arange(GRID_W), rows).astype(F32)
    n_freq = HEAD_DIM // 4
    inv = ROPE_THETA ** (-jnp.arange(n_freq, dtype=F32) / n_freq)
    return jnp.concatenate([row[:, None] * inv, col[:, None] * inv], axis=-1)


def _linear_angles(pos, dim):
    n_freq = dim // 2
    inv = ROPE_THETA ** (-jnp.arange(n_freq, dtype=F32) / n_freq)
    return pos.astype(F32)[:, None] * inv


def kernel(x, c, ctx, c_ctx, w_ada, b_ada, norm1_g, norm2_g, w_in, q_norm_g, k_norm_g, ret_decay_f, ret_decay_b, w_att_o, w_ret_o, w_out, ffn_w_gate, ffn_w_up, ffn_w_down, moe_router, moe_w_gate, moe_w_up, moe_w_down, final_g):
    b, seq, d = x.shape
    ctx_len = ctx.shape[1]
    depth = w_ada.shape[0]
    assert ctx_len == Q_TILE and seq % Q_TILE == 0 and seq % GRID_W == 0
    nt = seq + ctx_len
    m = b * nt
    tpb_big, tpb_ffn = 4, 8
    tm_big, tm_ffn = nt // tpb_big, nt // tpb_ffn
    tm_norm, tpb_norm = Q_TILE, nt // Q_TILE
    ctx_row = b
    nchunks = nt // RET_CHUNK
    nc_lat = seq // RET_CHUNK

    cos_a, sin_a = _rope_tables(_axial_angles(seq))
    cos_a = jnp.concatenate([cos_a, jnp.ones((ctx_len, HEAD_DIM), F32)], axis=0)
    sin_a = jnp.concatenate([sin_a, jnp.zeros((ctx_len, HEAD_DIM), F32)], axis=0)
    pos = jnp.concatenate([ctx_len + jnp.arange(seq), jnp.arange(ctx_len)])
    cos_r, sin_r = _rope_tables(_linear_angles(pos, RET_DK))

    n_rows = -(-(b + 1) // 16) * 16
    cc = jnp.zeros((n_rows, d), F32).at[:b].set(c).at[b].set(c_ctx)
    mods = adaln_all(cc, w_ada, b_ada)
    mods = mods.reshape(depth, n_rows, N_MOD, d).transpose(0, 2, 1, 3).reshape(depth, N_MOD, n_rows, 1, d)

    xa = jnp.concatenate([x, ctx], axis=1).reshape(m, d)

    def norm_mod(xcur, g, sc, sh):
        sb = pl.BlockSpec((None, 1, d), lambda i: (i // tpb_norm, 0, 0))
        scx = pl.BlockSpec((None, 1, d), lambda i: (ctx_row, 0, 0))
        row = pl.BlockSpec((tm_norm, d), lambda i: (i, 0))
        return pl.pallas_call(
            functools.partial(_norm_mod_kernel, tm=tm_norm, tpb=tpb_norm, seq=seq),
            out_shape=jax.ShapeDtypeStruct((m, d), BF16),
            grid=(m // tm_norm,),
            in_specs=[row, pl.BlockSpec((1, d), lambda i: (0, 0)), sb, scx, sb, scx],
            out_specs=row,
            compiler_params=_cparams("parallel"),
            name="norm_mod",
        )(xcur, g.reshape(1, d), sc, sc, sh, sh)

    tok_tab = lambda t: (t, (tm_big, HEAD_DIM), lambda i, j: (i % tpb_big, 0))
    row_vec = lambda v: (v.reshape(1, -1), (1, v.shape[-1]), lambda i, j: (0, 0))
    rope_scaled = functools.partial(_ep_rope, scale=RET_DK ** -0.5)
    rope_plain = functools.partial(_ep_rope, scale=1.0)
    fwd_chunk = lambda t: (t + nc_lat) % nchunks
    bwd_chunk = lambda t: nchunks - 1 - t

    c0 = 0
    c_q, c0 = c0, c0 + ATT_WIDTH
    c_k, c0 = c0, c0 + KV_WIDTH
    c_v, c0 = c0, c0 + KV_WIDTH
    c_rq, c0 = c0, c0 + RET_QK_WIDTH
    c_rk, c0 = c0, c0 + RET_QK_WIDTH
    c_rv, c0 = c0, c0 + RET_V_WIDTH
    c_rg, c0 = c0, c0 + 2 * RET_V_WIDTH
    c_bg, c0 = c0, c0 + 2 * d

    for layer in range(depth):
        last = layer == depth - 1
        sh1, sc1, g1, sh2, sc2, g2 = [mods[layer, n] for n in range(N_MOD)]
        w = w_in[layer].astype(BF16)
        tn_in = 512 if d % 512 == 0 else 256

        h = norm_mod(xa, norm1_g[layer], sc1, sh1)
        pj = functools.partial(proj, h, w, tm=tm_big, tn=tn_in)
        rc = 4 if tm_big % 64 == 0 else 1
        q = pj(c_q, ATT_WIDTH, BF16, _ep_head_norm_rope, [row_vec(q_norm_g[layer]), tok_tab(cos_a), tok_tab(sin_a)], row_chunks=rc, name="proj_q")
        k = pj(c_k, KV_WIDTH, BF16, _ep_head_norm_rope, [row_vec(k_norm_g[layer]), tok_tab(cos_a), tok_tab(sin_a)], row_chunks=rc, name="proj_k")
        v = pj(c_v, KV_WIDTH, BF16, _ep_plain, name="proj_v")
        rq = pj(c_rq, RET_QK_WIDTH, BF16, rope_plain, [tok_tab(cos_r), tok_tab(sin_r)], row_chunks=rc, name="proj_rq")
        rk = pj(c_rk, RET_QK_WIDTH, BF16, rope_scaled, [tok_tab(cos_r), tok_tab(sin_r)], row_chunks=rc, name="proj_rk")
        rv = pj(c_rv, RET_V_WIDTH, BF16, _ep_plain, name="proj_rv")
        rg = pj(c_rg, 2 * RET_V_WIDTH, F32, _ep_plain, name="proj_rg")
        gsig = pj(c_bg, 2 * d, F32, _ep_sigmoid, row_chunks=rc, name="proj_gates")

        r3 = lambda a: a.reshape(b, nt, a.shape[-1])
        att = attention(r3(q), r3(k), r3(v), seq=seq, with_ctx=not last).reshape(m, ATT_WIDTH)
        tf_ = retention_tables(ret_decay_f[layer], backward=False)
        tb_ = retention_tables(ret_decay_b[layer], backward=True)
        yf = retention_dir(r3(rq), r3(rk), r3(rv), r3(rg), 0, tf_, fwd_chunk, None, F32, name="retention_fwd")
        ret = retention_dir(r3(rq), r3(rk), r3(rv), r3(rg), 1, tb_, bwd_chunk, yf, BF16, name="retention_bwd").reshape(m, RET_V_WIDTH)

        z = branch_merge(att, ret, w_att_o[layer].astype(BF16), w_ret_o[layer].astype(BF16), gsig, tm=tm_big, tn=tn_in)
        xa = out_proj_residual(z, w_out[layer].astype(BF16), xa, g1, tm=tm_big, tpb=tpb_big, seq=seq, ctx_row=ctx_row, tn=tn_in)

        j = layer // 2
        if layer % 2 == 0:
            h2 = norm_mod(xa, norm2_g[layer], sc2, sh2)
            xa = ffn_residual(h2, ffn_w_gate[j].astype(BF16), ffn_w_up[j].astype(BF16), ffn_w_down[j].astype(BF16),
                              xa, g2, tm=tm_ffn, tpb=tpb_ffn, seq=seq, ctx_row=ctx_row)
        else:
            assert last, "routed-expert layers before the last layer are not supported"
            wr = jnp.zeros((d, LANES), F32).at[:, :N_EXPERTS].set(moe_router[j])
            sb = pl.BlockSpec((None, 1, d), lambda i: (i // tpb_norm, 0, 0))
            scx = pl.BlockSpec((None, 1, d), lambda i: (ctx_row, 0, 0))
            row = pl.BlockSpec((tm_norm, d), lambda i: (i, 0))
            h2, rec = pl.pallas_call(
                functools.partial(_norm_mod_router_kernel, tm=tm_norm, tpb=tpb_norm, seq=seq),
                out_shape=(jax.ShapeDtypeStruct((m, d), F32), jax.ShapeDtypeStruct((m, LANES), F32)),
                grid=(m // tm_norm,),
                in_specs=[row, pl.BlockSpec((1, d), lambda i: (0, 0)), sb, scx, sb, scx, pl.BlockSpec((d, LANES), lambda i: (0, 0))],
                out_specs=(row, pl.BlockSpec((tm_norm, LANES), lambda i: (i, 0))),
                compiler_params=_cparams("parallel"),
                name="norm_mod_router",
            )(xa, norm2_g[layer].reshape(1, d), sc2, sc2, sh2, sh2, wr)
            e12 = rec.reshape(b, nt, LANES)[:, :seq, :2].reshape(b * seq, 2).astype(jnp.int32)
            t_tok = b * seq
            eff = moe_w_gate.shape[-1]
            tf_moe = 256 if eff % 256 == 0 else eff
            row_step = int(np.lcm(eff // tf_moe, 16))
            tm_moe = max(896 // row_step, 1) * row_step
            plan = moe_plan(e12[:, 0], e12[:, 1], tm_moe)
            pairs = moe_experts(h2, moe_w_gate[j], moe_w_up[j], moe_w_down[j],
                                plan, t_tok, lambda tok: (tok // seq) * nt + tok % seq, tm=tm_moe, tf=tf_moe)
            tmc = Q_TILE
            spb = seq // tmc
            return pl.pallas_call(
                _combine_final_kernel,
                out_shape=jax.ShapeDtypeStruct((b, seq, d), F32),
                grid=(b, spb),
                in_specs=[
                    pl.BlockSpec((tmc, d), lambda bi, i: (bi * spb + i, 0)),
                    pl.BlockSpec((tmc, d), lambda bi, i: (t_tok // tmc + bi * spb + i, 0)),
                    pl.BlockSpec((tmc, LANES), lambda bi, i: (bi * (nt // tmc) + i, 0)),
                    pl.BlockSpec((None, tmc, d), lambda bi, i: (bi, i, 0)),
                    pl.BlockSpec((None, 1, d), lambda bi, i: (bi, 0, 0)),
                    pl.BlockSpec((1, d), lambda bi, i: (0, 0)),
                ],
                out_specs=pl.BlockSpec((None, tmc, d), lambda bi, i: (bi, i, 0)),
                compiler_params=_cparams("parallel", "parallel"),
                name="combine_final",
            )(pairs, pairs, rec, xa.reshape(b, nt, d), g2, final_g.reshape(1, d))

    tm_f = Q_TILE
    return pl.pallas_call(
        _final_norm_kernel,
        out_shape=jax.ShapeDtypeStruct((b, seq, d), F32),
        grid=(b, seq // tm_f),
        in_specs=[pl.BlockSpec((None, tm_f, d), lambda bi, i: (bi, i, 0)), pl.BlockSpec((1, d), lambda bi, i: (0, 0))],
        out_specs=pl.BlockSpec((None, tm_f, d), lambda bi, i: (bi, i, 0)),
        compiler_params=_cparams("parallel", "parallel"),
        name="final_norm",
    )(xa.reshape(b, nt, d), final_g.reshape(1, d))
```

```python
import functools

import jax
import jax.numpy as jnp
import numpy as np
from jax import lax
from jax.experimental import pallas as pl
from jax.experimental.pallas import tpu as pltpu

F32 = jnp.float32
BF16 = jnp.bfloat16

GRID_W = 64
N_HEADS = 16
N_KV_HEADS = 4
HEAD_DIM = 128
KV_GROUP = N_HEADS // N_KV_HEADS
ATT_WIDTH = N_HEADS * HEAD_DIM
KV_WIDTH = N_KV_HEADS * HEAD_DIM
ROPE_THETA = 10000.0
RET_HEADS = 8
RET_DK = 128
RET_DV = 256
RET_QK_WIDTH = RET_HEADS * RET_DK
RET_V_WIDTH = RET_HEADS * RET_DV
RET_CHUNK = 128
N_EXPERTS = 8
N_MOD = 6
EPS = 1e-6
GN_EPS = 1e-5

Q_TILE = 256
ATTN_ROW_CHUNKS = 2
LANES = 128
VMEM_LIMIT = 56 * 2**20


def _cparams(*sem):
    return pltpu.CompilerParams(dimension_semantics=sem, vmem_limit_bytes=VMEM_LIMIT)


def _silu(v):
    return v * jax.nn.sigmoid(v)


def _adaln_kernel(c_ref, w_ref, b_ref, o_ref):
    s = _silu(c_ref[...]).astype(BF16)
    o_ref[...] = jnp.dot(s, w_ref[...].astype(BF16), preferred_element_type=F32) + b_ref[...]


def adaln_all(cc, w_ada, b_ada):
    depth, d, n = w_ada.shape
    r = cc.shape[0]
    tn = 1024 if n % 1024 == 0 else n
    return pl.pallas_call(
        _adaln_kernel,
        out_shape=jax.ShapeDtypeStruct((depth, r, n), F32),
        grid=(depth, n // tn),
        in_specs=[
            pl.BlockSpec((r, d), lambda l, j: (0, 0)),
            pl.BlockSpec((None, d, tn), lambda l, j: (l, 0, j)),
            pl.BlockSpec((None, 1, tn), lambda l, j: (l, 0, j)),
        ],
        out_specs=pl.BlockSpec((None, r, tn), lambda l, j: (l, 0, j)),
        compiler_params=_cparams("parallel", "parallel"),
        name="adaln",
    )(cc, w_ada, b_ada.reshape(depth, 1, n))


def _ctx_rows(i, tm, tiles_per_batch, seq):
    rows = lax.broadcasted_iota(jnp.int32, (tm, 1), 0) + (i % tiles_per_batch) * tm
    return rows >= seq


def _rms(x, g):
    return x * lax.rsqrt(jnp.mean(x * x, axis=-1, keepdims=True) + EPS) * g


def _norm_mod_kernel(x_ref, g_ref, scb_ref, scc_ref, shb_ref, shc_ref, o_ref, *, tm, tpb, seq):
    i = pl.program_id(0)
    is_ctx = _ctx_rows(i, tm, tpb, seq)
    y = _rms(x_ref[...], g_ref[...])
    scale = jnp.where(is_ctx, scc_ref[...], scb_ref[...])
    shift = jnp.where(is_ctx, shc_ref[...], shb_ref[...])
    o_ref[...] = (y * (1 + scale) + shift).astype(o_ref.dtype)


def _norm_mod_router_kernel(x_ref, g_ref, scb_ref, scc_ref, shb_ref, shc_ref, wr_ref, o_ref, gate_ref, *, tm, tpb, seq):
    is_ctx = _ctx_rows(pl.program_id(0), tm, tpb, seq)
    y = _rms(x_ref[...], g_ref[...])
    scale = jnp.where(is_ctx, scc_ref[...], scb_ref[...])
    shift = jnp.where(is_ctx, shc_ref[...], shb_ref[...])
    t = y * (1 + scale) + shift
    o_ref[...] = t.astype(o_ref.dtype)
    logits = jnp.dot(t, wr_ref[...], preferred_element_type=F32, precision=lax.Precision.HIGHEST)
    lane = lax.broadcasted_iota(jnp.int32, logits.shape, 1)
    neg = jnp.float32(-jnp.inf)
    logits = jnp.where(lane < N_EXPERTS, logits, neg)
    m1 = jnp.max(logits, axis=-1, keepdims=True)
    i1 = jnp.min(jnp.where(logits == m1, lane, LANES), axis=-1, keepdims=True)
    rest = jnp.where(lane == i1, neg, logits)
    m2 = jnp.max(rest, axis=-1, keepdims=True)
    i2 = jnp.min(jnp.where(rest == m2, lane, LANES), axis=-1, keepdims=True)
    e2 = jnp.exp(m2 - m1)
    w1 = 1.0 / (1.0 + e2)
    w2 = e2 / (1.0 + e2)
    rec = jnp.where(lane == 0, i1.astype(F32), jnp.where(lane == 1, i2.astype(F32), jnp.where(lane == 2, w1, w2)))
    gate_ref[...] = jnp.where(lane < 4, rec, 0.0)


def _final_norm_kernel(x_ref, g_ref, o_ref):
    o_ref[...] = _rms(x_ref[...], g_ref[...])


def _mm_kernel(x_ref, w_ref, *rest, epilogue, row_chunks):
    o_ref = rest[-1]
    tm = x_ref.shape[0]
    rows = tm // row_chunks
    for c in range(row_chunks):
        rs = slice(c * rows, (c + 1) * rows)
        acc = jnp.dot(x_ref[rs, :], w_ref[...], preferred_element_type=F32)
        extras = [r[rs, :] if r.shape[0] == tm else r[...] for r in rest[:-1]]
        o_ref[rs, :] = epilogue(acc, *extras).astype(o_ref.dtype)


def _rope(y, cos, sin):
    return y * cos + pltpu.roll(y, HEAD_DIM // 2, 1) * sin


def _ep_plain(acc):
    return acc


def _ep_sigmoid(acc):
    return jax.nn.sigmoid(acc)


def _ep_head_norm_rope(acc, g, cos, sin):
    outs = []
    for h in range(acc.shape[1] // HEAD_DIM):
        y = _rms(acc[:, h * HEAD_DIM:(h + 1) * HEAD_DIM], g)
        outs.append(_rope(y, cos, sin))
    return jnp.concatenate(outs, axis=1)


def _ep_rope(acc, cos, sin, *, scale):
    outs = []
    for h in range(acc.shape[1] // RET_DK):
        y = _rope(acc[:, h * RET_DK:(h + 1) * RET_DK], cos, sin)
        outs.append(y * scale if scale != 1.0 else y)
    return jnp.concatenate(outs, axis=1)


def proj(h, w, col0, ncols, out_dtype, epilogue, extras=(), *, tm, tn=512, row_chunks=1, name):
    m, k = h.shape
    assert col0 % tn == 0 and ncols % tn == 0 and m % tm == 0 and tm % (16 * row_chunks) == 0
    jb = col0 // tn
    in_specs = [
        pl.BlockSpec((tm, k), lambda i, j: (i, 0)),
        pl.BlockSpec((k, tn), lambda i, j: (0, jb + j)),
    ]
    args = [h, w]
    for arr, bs, im in extras:
        in_specs.append(pl.BlockSpec(bs, im))
        args.append(arr)
    return pl.pallas_call(
        functools.partial(_mm_kernel, epilogue=epilogue, row_chunks=row_chunks),
        out_shape=jax.ShapeDtypeStruct((m, ncols), out_dtype),
        grid=(m // tm, ncols // tn),
        in_specs=in_specs,
        out_specs=pl.BlockSpec((tm, tn), lambda i, j: (i, j)),
        compiler_params=_cparams("parallel", "parallel"),
        name=name,
    )(*args)


def _attn_kernel(q_ref, k_ref, v_ref, o_ref, v1_ref, *, seq, n_lat_tiles, with_ctx):
    qi = pl.program_id(2)
    c = HEAD_DIM ** -0.5 * np.log2(np.e)

    @pl.when(qi == 0)
    def _():
        v1_ref[:, :HEAD_DIM] = v_ref[...]
        v1_ref[:, HEAD_DIM:] = jnp.ones(v_ref.shape, v1_ref.dtype)

    def run(k, v1):
        rows = q_ref.shape[0] // ATTN_ROW_CHUNKS
        for g in range(KV_GROUP):
            for r in range(ATTN_ROW_CHUNKS):
                rs = slice(r * rows, (r + 1) * rows)
                q = q_ref[rs, g * HEAD_DIM:(g + 1) * HEAD_DIM]
                s = lax.dot_general(q, k, (((1,), (1,)), ((), ())), preferred_element_type=F32)
                m = jnp.max(s, axis=-1, keepdims=True)
                p = jnp.exp2((s - m) * c).astype(BF16)
                o = jnp.dot(p, v1, preferred_element_type=F32)
                o_ref[rs, g * HEAD_DIM:(g + 1) * HEAD_DIM] = (o[:, :HEAD_DIM] / o[:, HEAD_DIM:]).astype(o_ref.dtype)

    @pl.when(qi < n_lat_tiles)
    def _():
        run(k_ref[...], v1_ref[...])

    @pl.when(qi >= n_lat_tiles)
    def _():
        if with_ctx:
            run(k_ref[seq:, :], v1_ref[seq:, :])
        else:
            o_ref[...] = jnp.zeros_like(o_ref)


def attention(q, k, v, *, seq, with_ctx):
    b, nt, _ = q.shape
    n_lat = seq // Q_TILE
    n_tiles = nt // Q_TILE
    gw = KV_GROUP * HEAD_DIM
    return pl.pallas_call(
        functools.partial(_attn_kernel, seq=seq, n_lat_tiles=n_lat, with_ctx=with_ctx),
        out_shape=jax.ShapeDtypeStruct((b, nt, ATT_WIDTH), BF16),
        grid=(b, N_KV_HEADS, n_tiles),
        in_specs=[
            pl.BlockSpec((None, Q_TILE, gw), lambda bi, h, i: (bi, i, h)),
            pl.BlockSpec((None, nt, HEAD_DIM), lambda bi, h, i: (bi, 0, h)),
            pl.BlockSpec((None, nt, HEAD_DIM), lambda bi, h, i: (bi, 0, h)),
        ],
        out_specs=pl.BlockSpec((None, Q_TILE, gw), lambda bi, h, i: (bi, i, h)),
        scratch_shapes=[pltpu.VMEM((nt, 2 * HEAD_DIM), BF16)],
        compiler_params=_cparams("parallel", "parallel", "arbitrary"),
        name="attention",
    )(q, k, v)


def _retention_kernel(*refs, has_prev):
    if has_prev:
        q_ref, k_ref, v_ref, g_ref, dec_ref, xi_ref, zeta_ref, cd_ref, prev_ref, o_ref, r_ref = refs
    else:
        q_ref, k_ref, v_ref, g_ref, dec_ref, xi_ref, zeta_ref, cd_ref, o_ref, r_ref = refs

    @pl.when(pl.program_id(1) == 0)
    def _():
        r_ref[...] = jnp.zeros_like(r_ref)

    for bi in range(q_ref.shape[0]):
        for h in range(RET_HEADS):
            ks = slice(h * RET_DK, (h + 1) * RET_DK)
            vs = slice(h * RET_DV, (h + 1) * RET_DV)
            q = q_ref[bi, :, ks]
            k = k_ref[bi, :, ks]
            v = v_ref[bi, :, vs]
            state = r_ref[bi, h]
            inner = lax.dot_general(q, k, (((1,), (1,)), ((), ())), preferred_element_type=F32) * dec_ref[h]
            o = jnp.dot(inner.astype(BF16), v, preferred_element_type=F32)
            o = o + jnp.dot(q, state.astype(BF16), preferred_element_type=F32) * xi_ref[h]
            kz = (k.astype(F32) * zeta_ref[h]).astype(BF16)
            r_ref[bi, h] = state * cd_ref[h] + lax.dot_general(kz, v, (((0,), (0,)), ((), ())), preferred_element_type=F32)
            mu = jnp.mean(o, axis=-1, keepdims=True)
            d = o - mu
            var = jnp.mean(d * d, axis=-1, keepdims=True)
            y = _silu(g_ref[bi, :, vs]) * (d * lax.rsqrt(var + GN_EPS))
            if has_prev:
                y = prev_ref[bi, :, vs] + y
            o_ref[bi, :, vs] = y.astype(o_ref.dtype)


def retention_dir(rq, rk, rv, rg, gate_col, tables, chunk_of, prev, out_dtype, *, name):
    b, nt, _ = rq.shape
    nc = nt // RET_CHUNK
    L = RET_CHUNK
    bb = next(n for n in (4, 2, 1) if b % n == 0)
    tok = lambda w: pl.BlockSpec((bb, L, w), lambda bi, t: (bi, chunk_of(t), 0))
    full = lambda a: pl.BlockSpec(a.shape, lambda bi, t: (0,) * a.ndim)
    in_specs = [tok(RET_QK_WIDTH), tok(RET_QK_WIDTH), tok(RET_V_WIDTH),
                pl.BlockSpec((bb, L, RET_V_WIDTH), lambda bi, t: (bi, chunk_of(t), gate_col))]
    in_specs += [full(a) for a in tables]
    args = [rq, rk, rv, rg, *tables]
    if prev is not None:
        in_specs.append(tok(RET_V_WIDTH))
        args.append(prev)
    return pl.pallas_call(
        functools.partial(_retention_kernel, has_prev=prev is not None),
        out_shape=jax.ShapeDtypeStruct((b, nt, RET_V_WIDTH), out_dtype),
        grid=(b // bb, nc),
        in_specs=in_specs,
        out_specs=tok(RET_V_WIDTH),
        scratch_shapes=[pltpu.VMEM((bb, RET_HEADS, RET_DK, RET_DV), F32)],
        compiler_params=_cparams("parallel", "arbitrary"),
        name=name,
    )(*args)


def retention_tables(decay_logit, backward):
    L = RET_CHUNK
    lg = jax.nn.log_sigmoid(decay_logit.astype(F32))
    pos = jnp.arange(L, dtype=F32)
    diff = pos[:, None] - pos[None, :]
    if backward:
        diff = -diff
        xi_e = L - pos
        zeta_e = pos
    else:
        xi_e = pos + 1.0
        zeta_e = L - 1.0 - pos
    decay = jnp.where(diff >= 0, jnp.exp(jnp.maximum(diff, 0.0) * lg[:, None, None]), 0.0)
    xi = jnp.broadcast_to(jnp.exp(xi_e * lg[:, None])[:, :, None], (RET_HEADS, L, RET_DV))
    zeta = jnp.broadcast_to(jnp.exp(zeta_e * lg[:, None])[:, :, None], (RET_HEADS, L, RET_DK))
    cd = jnp.broadcast_to(jnp.exp(L * lg)[:, None, None], (RET_HEADS, 1, RET_DV))
    return decay, xi, zeta, cd


def _merge_kernel(a_ref, r_ref, wa_ref, wr_ref, ga_ref, gr_ref, o_ref):
    za = jnp.dot(a_ref[...], wa_ref[...], preferred_element_type=F32)
    zr = jnp.dot(r_ref[...], wr_ref[...], preferred_element_type=F32)
    o_ref[...] = (ga_ref[...] * za + gr_ref[...] * zr).astype(o_ref.dtype)


def branch_merge(att, ret, w_att_o, w_ret_o, gsig, *, tm, tn=512):
    m, ka = att.shape
    kr = ret.shape[1]
    d = w_att_o.shape[1]
    nj = d // tn
    return pl.pallas_call(
        _merge_kernel,
        out_shape=jax.ShapeDtypeStruct((m, d), BF16),
        grid=(m // tm, nj),
        in_specs=[
            pl.BlockSpec((tm, ka), lambda i, j: (i, 0)),
            pl.BlockSpec((tm, kr), lambda i, j: (i, 0)),
            pl.BlockSpec((ka, tn), lambda i, j: (0, j)),
            pl.BlockSpec((kr, tn), lambda i, j: (0, j)),
            pl.BlockSpec((tm, tn), lambda i, j: (i, j)),
            pl.BlockSpec((tm, tn), lambda i, j: (i, nj + j)),
        ],
        out_specs=pl.BlockSpec((tm, tn), lambda i, j: (i, j)),
        compiler_params=_cparams("parallel", "parallel"),
        name="branch_merge",
    )(att, ret, w_att_o, w_ret_o, gsig, gsig)


def _resid_mm_kernel(z_ref, w_ref, x_ref, gb_ref, gc_ref, o_ref, *, tm, tpb, seq):
    i = pl.program_id(0)
    gate = jnp.where(_ctx_rows(i, tm, tpb, seq), gc_ref[...], gb_ref[...])
    o_ref[...] = x_ref[...] + gate * jnp.dot(z_ref[...], w_ref[...], preferred_element_type=F32)


def out_proj_residual(z, w_out, x, gate, *, tm, tpb, seq, ctx_row, tn=512):
    m, k = z.shape
    d = w_out.shape[1]
    return pl.pallas_call(
        functools.partial(_resid_mm_kernel, tm=tm, tpb=tpb, seq=seq),
        out_shape=jax.ShapeDtypeStruct((m, d), F32),
        grid=(m // tm, d // tn),
        in_specs=[
            pl.BlockSpec((tm, k), lambda i, j: (i, 0)),
            pl.BlockSpec((k, tn), lambda i, j: (0, j)),
            pl.BlockSpec((tm, tn), lambda i, j: (i, j)),
            pl.BlockSpec((None, 1, tn), lambda i, j: (i // tpb, 0, j)),
            pl.BlockSpec((None, 1, tn), lambda i, j: (ctx_row, 0, j)),
        ],
        out_specs=pl.BlockSpec((tm, tn), lambda i, j: (i, j)),
        compiler_params=_cparams("parallel", "parallel"),
        name="out_proj",
    )(z, w_out, x, gate, gate)


def _ffn_kernel(h_ref, wg_ref, wu_ref, wd_ref, x_ref, gb_ref, gc_ref, o_ref, acc_ref, *, tm, tpb, seq):
    i, f = pl.program_id(0), pl.program_id(1)

    @pl.when(f == 0)
    def _():
        acc_ref[...] = jnp.zeros_like(acc_ref)

    h = h_ref[...]
    g = jnp.dot(h, wg_ref[...], preferred_element_type=F32)
    u = jnp.dot(h, wu_ref[...], preferred_element_type=F32)
    a = (_silu(g) * u).astype(BF16)
    acc_ref[...] += jnp.dot(a, wd_ref[...], preferred_element_type=F32)

    @pl.when(f == pl.num_programs(1) - 1)
    def _():
        gate = jnp.where(_ctx_rows(i, tm, tpb, seq), gc_ref[...], gb_ref[...])
        o_ref[...] = x_ref[...] + gate * acc_ref[...]


def ffn_residual(h, wg, wu, wd, x, gate, *, tm, tpb, seq, ctx_row, tf=512):
    m, d = h.shape
    ff = wg.shape[1]
    return pl.pallas_call(
        functools.partial(_ffn_kernel, tm=tm, tpb=tpb, seq=seq),
        out_shape=jax.ShapeDtypeStruct((m, d), F32),
        grid=(m // tm, ff // tf),
        in_specs=[
            pl.BlockSpec((tm, d), lambda i, f: (i, 0)),
            pl.BlockSpec((d, tf), lambda i, f: (0, f)),
            pl.BlockSpec((d, tf), lambda i, f: (0, f)),
            pl.BlockSpec((tf, d), lambda i, f: (f, 0)),
            pl.BlockSpec((tm, d), lambda i, f: (i, 0)),
            pl.BlockSpec((None, 1, d), lambda i, f: (i // tpb, 0, 0)),
            pl.BlockSpec((None, 1, d), lambda i, f: (ctx_row, 0, 0)),
        ],
        out_specs=pl.BlockSpec((tm, d), lambda i, f: (i, 0)),
        scratch_shapes=[pltpu.VMEM((tm, d), F32)],
        compiler_params=_cparams("parallel", "arbitrary"),
        name="ffn",
    )(h, wg, wu, wd, x, gate, gate)


def moe_plan(e1, e2, tm):
    t = e1.shape[0]
    p = 2 * t
    e = jnp.stack([e1, e2], axis=1).reshape(p)
    onehot = (e[:, None] == jnp.arange(N_EXPERTS, dtype=jnp.int32)[None, :]).astype(jnp.int32)
    csum = jnp.cumsum(onehot, axis=0)
    rank = jnp.take_along_axis(csum, e[:, None], axis=1)[:, 0] - 1
    cnt = csum[-1]
    gsz = (cnt + tm - 1) // tm * tm
    gend = jnp.cumsum(gsz)
    gstart = gend - gsz
    pos = gstart[e] + rank
    n_tiles = -(-p // tm) + N_EXPERTS
    pair_of = jnp.full((n_tiles * tm,), -1, jnp.int32).at[pos].set(jnp.arange(p, dtype=jnp.int32))
    tile_start = jnp.arange(n_tiles, dtype=jnp.int32) * tm
    tile_e = jnp.minimum(jnp.searchsorted(gend, tile_start, side="right").astype(jnp.int32), N_EXPERTS - 1)
    tile_valid = jnp.clip(gstart[tile_e] + cnt[tile_e] - tile_start, 0, tm).astype(jnp.int32)
    used = gend[-1] // tm
    tile_e = jnp.where(tile_start < gend[-1], tile_e, tile_e[jnp.maximum(used - 1, 0)])
    return pair_of, tile_e, tile_valid


def _moe_kernel(te_ref, tv_ref, src_ref, nsrc_ref, pdst_ref, h_hbm, wg_ref, wu_ref, wd_ref, pairs_hbm,
                xg_ref, xb_ref, acc_ref, ys_ref, gsem, ssem, *, tm, n_f, dump_row):
    i, f = pl.program_id(0), pl.program_id(1)
    n_i = pl.num_programs(0)
    has_rows = tv_ref[i] > 0
    slot = i % 2
    rows_per_step = tm // n_f

    def gather_copy(s, r, row):
        return pltpu.make_async_copy(h_hbm.at[pl.ds(row, 1)], xg_ref.at[s, pl.ds(r, 1)], gsem.at[s])

    def scatter_copy(r, row):
        return pltpu.make_async_copy(ys_ref.at[pl.ds(r, 1)], pairs_hbm.at[pl.ds(row, 1)], ssem)

    def wait_gather(s):
        def body(r, carry):
            gather_copy(s, r, 0).wait()
            return carry
        lax.fori_loop(0, tm, body, 0, unroll=8)

    def wait_scatter():
        def body(r, carry):
            scatter_copy(r, 0).wait()
            return carry
        lax.fori_loop(0, tm, body, 0, unroll=8)

    def start_step_dmas():
        base = f * rows_per_step
        for j in range(rows_per_step):
            r = base + j
            gather_copy(1 - slot, r, nsrc_ref[0, r]).start()
            scatter_copy(r, jnp.where(i == 0, dump_row + r, pdst_ref[0, r])).start()

    @pl.when((i == 0) & (f == 0))
    def _():
        ys_ref[...] = jnp.zeros_like(ys_ref)

        def body(r, carry):
            gather_copy(0, r, src_ref[0, r]).start()
            return carry
        lax.fori_loop(0, tm, body, 0, unroll=8)

    @pl.when(f == 0)
    def _():
        wait_gather(slot)

    @pl.when(has_rows)
    def _():
        @pl.when(f == 0)
        def _():
            xb_ref[...] = xg_ref[slot].astype(BF16)
            acc_ref[...] = jnp.zeros_like(acc_ref)

        start_step_dmas()
        x = xb_ref[...]
        g = jnp.dot(x, wg_ref[...].astype(BF16), preferred_element_type=F32)
        u = jnp.dot(x, wu_ref[...].astype(BF16), preferred_element_type=F32)
        a = (_silu(g) * u).astype(BF16)
        acc_ref[...] += jnp.dot(a, wd_ref[...].astype(BF16), preferred_element_type=F32)

    @pl.when(jnp.logical_not(has_rows))
    def _():
        start_step_dmas()

    @pl.when(f == n_f - 1)
    def _():
        wait_scatter()

        @pl.when(has_rows)
        def _():
            ys_ref[...] = acc_ref[...]

        @pl.when(i == n_i - 1)
        def _():
            wait_gather(1 - slot)


def moe_experts(h, wg, wu, wd, plan, t, src_of_token, *, tm, tf):
    pair_of, tile_e, tile_valid = plan
    n_tiles = tile_e.shape[0]
    _, d, ff = wg.shape
    nf = ff // tf
    assert tm % nf == 0
    real = pair_of >= 0
    pr = jnp.maximum(pair_of, 0)
    dump = 2 * t + jnp.arange(n_tiles * tm, dtype=jnp.int32) % tm
    row_src = jnp.where(real, src_of_token(pr // 2), 0).reshape(n_tiles, 1, tm)
    row_dst = jnp.where(real, (pr % 2) * t + pr // 2, dump).reshape(n_tiles, 1, tm)
    fe = lambda i, f, te, tv: jnp.where(tv[i] > 0, f, nf - 1)
    rows_at = lambda off: pl.BlockSpec((None, 1, tm), lambda i, f, te, tv: (jnp.clip(i + off, 0, n_tiles - 1), 0, 0),
                                       memory_space=pltpu.SMEM)
    grid_spec = pltpu.PrefetchScalarGridSpec(
        num_scalar_prefetch=2,
        grid=(n_tiles, nf),
        in_specs=[
            rows_at(0), rows_at(1), rows_at(-1),
            pl.BlockSpec(memory_space=pl.ANY),
            pl.BlockSpec((None, d, tf), lambda i, f, te, tv: (te[i], 0, fe(i, f, te, tv))),
            pl.BlockSpec((None, d, tf), lambda i, f, te, tv: (te[i], 0, fe(i, f, te, tv))),
            pl.BlockSpec((None, tf, d), lambda i, f, te, tv: (te[i], fe(i, f, te, tv), 0)),
        ],
        out_specs=pl.BlockSpec(memory_space=pl.ANY),
        scratch_shapes=[pltpu.VMEM((2, tm, d), F32), pltpu.VMEM((tm, d), BF16), pltpu.VMEM((tm, d), F32),
                        pltpu.VMEM((tm, d), F32), pltpu.SemaphoreType.DMA((2,)), pltpu.SemaphoreType.DMA(())],
    )
    return pl.pallas_call(
        functools.partial(_moe_kernel, tm=tm, n_f=nf, dump_row=2 * t),
        out_shape=jax.ShapeDtypeStruct((2 * t + tm, d), F32),
        grid_spec=grid_spec,
        compiler_params=pltpu.CompilerParams(dimension_semantics=("arbitrary", "arbitrary"),
                                             vmem_limit_bytes=VMEM_LIMIT, disable_bounds_checks=True),
        name="moe_experts",
    )(tile_e, tile_valid, row_src, row_src, row_dst, h, wg, wu, wd)


def _combine_final_kernel(p0_ref, p1_ref, rec_ref, x_ref, g_ref, fg_ref, o_ref):
    rec = rec_ref[...]
    y = rec[:, 2:3] * p0_ref[...] + rec[:, 3:4] * p1_ref[...]
    o_ref[...] = _rms(x_ref[...] + g_ref[...] * y, fg_ref[...])


def _rope_tables(ang):
    c, s = jnp.cos(ang), jnp.sin(ang)
    return jnp.concatenate([c, c], axis=-1), jnp.concatenate([-s, s], axis=-1)


def _axial_angles(seq):
    rows = seq // GRID_W
    row = jnp.repeat(jnp.arange(rows), GRID_W).astype(F32)
    col = jnp.tile(jnp.arange(GRID_W), rows).astype(F32)
    n_freq = HEAD_DIM // 4
    inv = ROPE_THETA ** (-jnp.arange(n_freq, dtype=F32) / n_freq)
    return jnp.concatenate([row[:, None] * inv, col[:, None] * inv], axis=-1)


def _linear_angles(pos, dim):
    n_freq = dim // 2
    inv = ROPE_THETA ** (-jnp.arange(n_freq, dtype=F32) / n_freq)
    return pos.astype(F32)[:, None] * inv


def kernel(x, c, ctx, c_ctx, w_ada, b_ada, norm1_g, norm2_g, w_in, q_norm_g, k_norm_g, ret_decay_f, ret_decay_b, w_att_o, w_ret_o, w_out, ffn_w_gate, ffn_w_up, ffn_w_down, moe_router, moe_w_gate, moe_w_up, moe_w_down, final_g):
    b, seq, d = x.shape
    ctx_len = ctx.shape[1]
    depth = w_ada.shape[0]
    assert ctx_len == Q_TILE and seq % Q_TILE == 0 and seq % GRID_W == 0
    nt = seq + ctx_len
    m = b * nt
    tpb_big, tpb_ffn = 4, 8
    tm_big, tm_ffn = nt // tpb_big, nt // tpb_ffn
    tm_norm, tpb_norm = tm_big, tpb_big
    ctx_row = b
    nchunks = nt // RET_CHUNK
    nc_lat = seq // RET_CHUNK

    cos_a, sin_a = _rope_tables(_axial_angles(seq))
    cos_a = jnp.concatenate([cos_a, jnp.ones((ctx_len, HEAD_DIM), F32)], axis=0)
    sin_a = jnp.concatenate([sin_a, jnp.zeros((ctx_len, HEAD_DIM), F32)], axis=0)
    pos = jnp.concatenate([ctx_len + jnp.arange(seq), jnp.arange(ctx_len)])
    cos_r, sin_r = _rope_tables(_linear_angles(pos, RET_DK))

    n_rows = -(-(b + 1) // 16) * 16
    cc = jnp.zeros((n_rows, d), F32).at[:b].set(c).at[b].set(c_ctx)
    mods = adaln_all(cc, w_ada, b_ada)
    mods = mods.reshape(depth, n_rows, N_MOD, d).transpose(0, 2, 1, 3).reshape(depth, N_MOD, n_rows, 1, d)

    xa = jnp.concatenate([x, ctx], axis=1).reshape(m, d)

    def norm_mod(xcur, g, sc, sh):
        sb = pl.BlockSpec((None, 1, d), lambda i: (i // tpb_norm, 0, 0))
        scx = pl.BlockSpec((None, 1, d), lambda i: (ctx_row, 0, 0))
        row = pl.BlockSpec((tm_norm, d), lambda i: (i, 0))
        return pl.pallas_call(
            functools.partial(_norm_mod_kernel, tm=tm_norm, tpb=tpb_norm, seq=seq),
            out_shape=jax.ShapeDtypeStruct((m, d), BF16),
            grid=(m // tm_norm,),
            in_specs=[row, pl.BlockSpec((1, d), lambda i: (0, 0)), sb, scx, sb, scx],
            out_specs=row,
            compiler_params=_cparams("parallel"),
            name="norm_mod",
        )(xcur, g.reshape(1, d), sc, sc, sh, sh)

    tok_tab = lambda t: (t, (tm_big, HEAD_DIM), lambda i, j: (i % tpb_big, 0))
    row_vec = lambda v: (v.reshape(1, -1), (1, v.shape[-1]), lambda i, j: (0, 0))
    rope_scaled = functools.partial(_ep_rope, scale=RET_DK ** -0.5)
    rope_plain = functools.partial(_ep_rope, scale=1.0)
    fwd_chunk = lambda t: (t + nc_lat) % nchunks
    bwd_chunk = lambda t: nchunks - 1 - t

    c0 = 0
    c_q, c0 = c0, c0 + ATT_WIDTH
    c_k, c0 = c0, c0 + KV_WIDTH
    c_v, c0 = c0, c0 + KV_WIDTH
    c_rq, c0 = c0, c0 + RET_QK_WIDTH
    c_rk, c0 = c0, c0 + RET_QK_WIDTH
    c_rv, c0 = c0, c0 + RET_V_WIDTH
    c_rg, c0 = c0, c0 + 2 * RET_V_WIDTH
    c_bg, c0 = c0, c0 + 2 * d

    for layer in range(depth):
        last = layer == depth - 1
        sh1, sc1, g1, sh2, sc2, g2 = [mods[layer, n] for n in range(N_MOD)]
        w = w_in[layer].astype(BF16)
        tn_in = 512 if d % 512 == 0 else 256

        h = norm_mod(xa, norm1_g[layer], sc1, sh1)
        pj = functools.partial(proj, h, w, tm=tm_big, tn=tn_in)
        rc = 4 if tm_big % 64 == 0 else 1
        q = pj(c_q, ATT_WIDTH, BF16, _ep_head_norm_rope, [row_vec(q_norm_g[layer]), tok_tab(cos_a), tok_tab(sin_a)], row_chunks=rc, name="proj_q")
        k = pj(c_k, KV_WIDTH, BF16, _ep_head_norm_rope, [row_vec(k_norm_g[layer]), tok_tab(cos_a), tok_tab(sin_a)], row_chunks=rc, name="proj_k")
        v = pj(c_v, KV_WIDTH, BF16, _ep_plain, name="proj_v")
        rq = pj(c_rq, RET_QK_WIDTH, BF16, rope_plain, [tok_tab(cos_r), tok_tab(sin_r)], row_chunks=rc, name="proj_rq")
        rk = pj(c_rk, RET_QK_WIDTH, BF16, rope_scaled, [tok_tab(cos_r), tok_tab(sin_r)], row_chunks=rc, name="proj_rk")
        rv = pj(c_rv, RET_V_WIDTH, BF16, _ep_plain, name="proj_rv")
        rg = pj(c_rg, 2 * RET_V_WIDTH, F32, _ep_plain, name="proj_rg")
        gsig = pj(c_bg, 2 * d, F32, _ep_sigmoid, row_chunks=rc, name="proj_gates")

        r3 = lambda a: a.reshape(b, nt, a.shape[-1])
        att = attention(r3(q), r3(k), r3(v), seq=seq, with_ctx=not last).reshape(m, ATT_WIDTH)
        tf_ = retention_tables(ret_decay_f[layer], backward=False)
        tb_ = retention_tables(ret_decay_b[layer], backward=True)
        yf = retention_dir(r3(rq), r3(rk), r3(rv), r3(rg), 0, tf_, fwd_chunk, None, F32, name="retention_fwd")
        ret = retention_dir(r3(rq), r3(rk), r3(rv), r3(rg), 1, tb_, bwd_chunk, yf, BF16, name="retention_bwd").reshape(m, RET_V_WIDTH)

        z = branch_merge(att, ret, w_att_o[layer].astype(BF16), w_ret_o[layer].astype(BF16), gsig, tm=tm_big, tn=tn_in)
        xa = out_proj_residual(z, w_out[layer].astype(BF16), xa, g1, tm=tm_big, tpb=tpb_big, seq=seq, ctx_row=ctx_row, tn=tn_in)

        j = layer // 2
        if layer % 2 == 0:
            h2 = norm_mod(xa, norm2_g[layer], sc2, sh2)
            xa = ffn_residual(h2, ffn_w_gate[j].astype(BF16), ffn_w_up[j].astype(BF16), ffn_w_down[j].astype(BF16),
                              xa, g2, tm=tm_ffn, tpb=tpb_ffn, seq=seq, ctx_row=ctx_row)
        else:
            assert last, "routed-expert layers before the last layer are not supported"
            wr = jnp.zeros((d, LANES), F32).at[:, :N_EXPERTS].set(moe_router[j])
            sb = pl.BlockSpec((None, 1, d), lambda i: (i // tpb_norm, 0, 0))
            scx = pl.BlockSpec((None, 1, d), lambda i: (ctx_row, 0, 0))
            row = pl.BlockSpec((tm_norm, d), lambda i: (i, 0))
            h2, rec = pl.pallas_call(
                functools.partial(_norm_mod_router_kernel, tm=tm_norm, tpb=tpb_norm, seq=seq),
                out_shape=(jax.ShapeDtypeStruct((m, d), F32), jax.ShapeDtypeStruct((m, LANES), F32)),
                grid=(m // tm_norm,),
                in_specs=[row, pl.BlockSpec((1, d), lambda i: (0, 0)), sb, scx, sb, scx, pl.BlockSpec((d, LANES), lambda i: (0, 0))],
                out_specs=(row, pl.BlockSpec((tm_norm, LANES), lambda i: (i, 0))),
                compiler_params=_cparams("parallel"),
                name="norm_mod_router",
            )(xa, norm2_g[layer].reshape(1, d), sc2, sc2, sh2, sh2, wr)
            e12 = rec.reshape(b, nt, LANES)[:, :seq, :2].reshape(b * seq, 2).astype(jnp.int32)
            t_tok = b * seq
            eff = moe_w_gate.shape[-1]
            tf_moe = 256 if eff % 256 == 0 else eff
            row_step = int(np.lcm(eff // tf_moe, 16))
            tm_moe = max(896 // row_step, 1) * row_step
            plan = moe_plan(e12[:, 0], e12[:, 1], tm_moe)
            pairs = moe_experts(h2, moe_w_gate[j], moe_w_up[j], moe_w_down[j],
                                plan, t_tok, lambda tok: (tok // seq) * nt + tok % seq, tm=tm_moe, tf=tf_moe)
            tmc = Q_TILE
            spb = seq // tmc
            return pl.pallas_call(
                _combine_final_kernel,
                out_shape=jax.ShapeDtypeStruct((b, seq, d), F32),
                grid=(b, spb),
                in_specs=[
                    pl.BlockSpec((tmc, d), lambda bi, i: (bi * spb + i, 0)),
                    pl.BlockSpec((tmc, d), lambda bi, i: (t_tok // tmc + bi * spb + i, 0)),
                    pl.BlockSpec((tmc, LANES), lambda bi, i: (bi * (nt // tmc) + i, 0)),
                    pl.BlockSpec((None, tmc, d), lambda bi, i: (bi, i, 0)),
                    pl.BlockSpec((None, 1, d), lambda bi, i: (bi, 0, 0)),
                    pl.BlockSpec((1, d), lambda bi, i: (0, 0)),
                ],
                out_specs=pl.BlockSpec((None, tmc, d), lambda bi, i: (bi, i, 0)),
                compiler_params=_cparams("parallel", "parallel"),
                name="combine_final",
            )(pairs, pairs, rec, xa.reshape(b, nt, d), g2, final_g.reshape(1, d))

    tm_f = Q_TILE
    return pl.pallas_call(
        _final_norm_kernel,
        out_shape=jax.ShapeDtypeStruct((b, seq, d), F32),
        grid=(b, seq // tm_f),
        in_specs=[pl.BlockSpec((None, tm_f, d), lambda bi, i: (bi, i, 0)), pl.BlockSpec((1, d), lambda bi, i: (0, 0))],
        out_specs=pl.BlockSpec((None, tm_f, d), lambda bi, i: (bi, i, 0)),
        compiler_params=_cparams("parallel", "parallel"),
        name="final_norm",
    )(xa.reshape(b, nt, d), final_g.reshape(1, d))
```

```python
import functools

import jax
import jax.numpy as jnp
import numpy as np
from jax import lax
from jax.experimental import pallas as pl
from jax.experimental.pallas import tpu as pltpu

F32 = jnp.float32
BF16 = jnp.bfloat16

GRID_W = 64
N_HEADS = 16
N_KV_HEADS = 4
HEAD_DIM = 128
KV_GROUP = N_HEADS // N_KV_HEADS
ATT_WIDTH = N_HEADS * HEAD_DIM
KV_WIDTH = N_KV_HEADS * HEAD_DIM
ROPE_THETA = 10000.0
RET_HEADS = 8
RET_DK = 128
RET_DV = 256
RET_QK_WIDTH = RET_HEADS * RET_DK
RET_V_WIDTH = RET_HEADS * RET_DV
RET_CHUNK = 128
N_EXPERTS = 8
N_MOD = 6
EPS = 1e-6
GN_EPS = 1e-5

Q_TILE = 256
ATTN_ROW_CHUNKS = 2
FFN_EPILOGUE_CHUNKS = 2
LANES = 128
VMEM_LIMIT = 56 * 2**20


def _cparams(*sem):
    return pltpu.CompilerParams(dimension_semantics=sem, vmem_limit_bytes=VMEM_LIMIT)


def _silu(v):
    return v * jax.nn.sigmoid(v)


def _adaln_kernel(c_ref, w_ref, b_ref, o_ref):
    s = _silu(c_ref[...]).astype(BF16)
    o_ref[...] = jnp.dot(s, w_ref[...].astype(BF16), preferred_element_type=F32) + b_ref[...]


def adaln_all(cc, w_ada, b_ada):
    depth, d, n = w_ada.shape
    r = cc.shape[0]
    tn = 1024 if n % 1024 == 0 else n
    return pl.pallas_call(
        _adaln_kernel,
        out_shape=jax.ShapeDtypeStruct((depth, r, n), F32),
        grid=(depth, n // tn),
        in_specs=[
            pl.BlockSpec((r, d), lambda l, j: (0, 0)),
            pl.BlockSpec((None, d, tn), lambda l, j: (l, 0, j)),
            pl.BlockSpec((None, 1, tn), lambda l, j: (l, 0, j)),
        ],
        out_specs=pl.BlockSpec((None, r, tn), lambda l, j: (l, 0, j)),
        compiler_params=_cparams("parallel", "parallel"),
        name="adaln",
    )(cc, w_ada, b_ada.reshape(depth, 1, n))


def _ctx_rows(i, tm, tiles_per_batch, seq):
    rows = lax.broadcasted_iota(jnp.int32, (tm, 1), 0) + (i % tiles_per_batch) * tm
    return rows >= seq


def _rms(x, g):
    return x * lax.rsqrt(jnp.mean(x * x, axis=-1, keepdims=True) + EPS) * g


def _norm_mod_kernel(x_ref, g_ref, scb_ref, scc_ref, shb_ref, shc_ref, o_ref, *, tm, tpb, seq):
    i = pl.program_id(0)
    is_ctx = _ctx_rows(i, tm, tpb, seq)
    y = _rms(x_ref[...], g_ref[...])
    scale = jnp.where(is_ctx, scc_ref[...], scb_ref[...])
    shift = jnp.where(is_ctx, shc_ref[...], shb_ref[...])
    o_ref[...] = (y * (1 + scale) + shift).astype(o_ref.dtype)


def _norm_mod_router_kernel(x_ref, g_ref, scb_ref, scc_ref, shb_ref, shc_ref, wr_ref, o_ref, gate_ref, *, tm, tpb, seq):
    is_ctx = _ctx_rows(pl.program_id(0), tm, tpb, seq)
    y = _rms(x_ref[...], g_ref[...])
    scale = jnp.where(is_ctx, scc_ref[...], scb_ref[...])
    shift = jnp.where(is_ctx, shc_ref[...], shb_ref[...])
    t = y * (1 + scale) + shift
    o_ref[...] = t.astype(o_ref.dtype)
    logits = jnp.dot(t, wr_ref[...], preferred_element_type=F32, precision=lax.Precision.HIGHEST)
    lane = lax.broadcasted_iota(jnp.int32, logits.shape, 1)
    neg = jnp.float32(-jnp.inf)
    logits = jnp.where(lane < N_EXPERTS, logits, neg)
    m1 = jnp.max(logits, axis=-1, keepdims=True)
    i1 = jnp.min(jnp.where(logits == m1, lane, LANES), axis=-1, keepdims=True)
    rest = jnp.where(lane == i1, neg, logits)
    m2 = jnp.max(rest, axis=-1, keepdims=True)
    i2 = jnp.min(jnp.where(rest == m2, lane, LANES), axis=-1, keepdims=True)
    e2 = jnp.exp(m2 - m1)
    w1 = 1.0 / (1.0 + e2)
    w2 = e2 / (1.0 + e2)
    rec = jnp.where(lane == 0, i1.astype(F32), jnp.where(lane == 1, i2.astype(F32), jnp.where(lane == 2, w1, w2)))
    gate_ref[...] = jnp.where(lane < 4, rec, 0.0)


def _final_norm_kernel(x_ref, g_ref, o_ref):
    o_ref[...] = _rms(x_ref[...], g_ref[...])


def _mm_kernel(x_ref, w_ref, *rest, epilogue, row_chunks):
    o_ref = rest[-1]
    tm = x_ref.shape[0]
    rows = tm // row_chunks
    w = w_ref[...].astype(BF16)
    for c in range(row_chunks):
        rs = slice(c * rows, (c + 1) * rows)
        acc = jnp.dot(x_ref[rs, :], w, preferred_element_type=F32)
        extras = [r[rs, :] if r.shape[0] == tm else r[...] for r in rest[:-1]]
        o_ref[rs, :] = epilogue(acc, *extras).astype(o_ref.dtype)


def _rope(y, cos, sin):
    return y * cos + pltpu.roll(y, HEAD_DIM // 2, 1) * sin


def _ep_plain(acc):
    return acc


def _ep_sigmoid(acc):
    return jax.nn.sigmoid(acc)


def _ep_head_norm_rope(acc, g, cos, sin):
    outs = []
    for h in range(acc.shape[1] // HEAD_DIM):
        y = _rms(acc[:, h * HEAD_DIM:(h + 1) * HEAD_DIM], g)
        outs.append(_rope(y, cos, sin))
    return jnp.concatenate(outs, axis=1)


def _ep_rope(acc, cos, sin, *, scale):
    outs = []
    for h in range(acc.shape[1] // RET_DK):
        y = _rope(acc[:, h * RET_DK:(h + 1) * RET_DK], cos, sin)
        outs.append(y * scale if scale != 1.0 else y)
    return jnp.concatenate(outs, axis=1)


def proj(h, w, layer, col0, ncols, out_dtype, epilogue, extras=(), *, tm, tn=512, row_chunks=1, name):
    m, k = h.shape
    assert col0 % tn == 0 and ncols % tn == 0 and m % tm == 0 and tm % (16 * row_chunks) == 0
    jb = col0 // tn
    in_specs = [
        pl.BlockSpec((tm, k), lambda i, j: (i, 0)),
        pl.BlockSpec((None, k, tn), lambda i, j: (layer, 0, jb + j)),
    ]
    args = [h, w]
    for arr, bs, im in extras:
        in_specs.append(pl.BlockSpec(bs, im))
        args.append(arr)
    return pl.pallas_call(
        functools.partial(_mm_kernel, epilogue=epilogue, row_chunks=row_chunks),
        out_shape=jax.ShapeDtypeStruct((m, ncols), out_dtype),
        grid=(m // tm, ncols // tn),
        in_specs=in_specs,
        out_specs=pl.BlockSpec((tm, tn), lambda i, j: (i, j)),
        compiler_params=_cparams("parallel", "parallel"),
        name=name,
    )(*args)


def _attn_kernel(q_ref, k_ref, v_ref, o_ref, v1_ref, *, seq, n_lat_tiles, with_ctx):
    qi = pl.program_id(2)
    c = HEAD_DIM ** -0.5 * np.log2(np.e)

    @pl.when(qi == 0)
    def _():
        v1_ref[:, :HEAD_DIM] = v_ref[...]
        v1_ref[:, HEAD_DIM:] = jnp.ones(v_ref.shape, v1_ref.dtype)

    def run(k, v1):
        rows = q_ref.shape[0] // ATTN_ROW_CHUNKS
        for g in range(KV_GROUP):
            for r in range(ATTN_ROW_CHUNKS):
                rs = slice(r * rows, (r + 1) * rows)
                q = q_ref[rs, g * HEAD_DIM:(g + 1) * HEAD_DIM]
                s = lax.dot_general(q, k, (((1,), (1,)), ((), ())), preferred_element_type=F32)
                m = jnp.max(s, axis=-1, keepdims=True)
                p = jnp.exp2((s - m) * c).astype(BF16)
                o = jnp.dot(p, v1, preferred_element_type=F32)
                o_ref[rs, g * HEAD_DIM:(g + 1) * HEAD_DIM] = (o[:, :HEAD_DIM] / o[:, HEAD_DIM:]).astype(o_ref.dtype)

    @pl.when(qi < n_lat_tiles)
    def _():
        run(k_ref[...], v1_ref[...])

    @pl.when(qi >= n_lat_tiles)
    def _():
        if with_ctx:
            run(k_ref[seq:, :], v1_ref[seq:, :])
        else:
            o_ref[...] = jnp.zeros_like(o_ref)


def attention(q, k, v, *, seq, with_ctx):
    b, nt, _ = q.shape
    n_lat = seq // Q_TILE
    n_tiles = nt // Q_TILE
    gw = KV_GROUP * HEAD_DIM
    return pl.pallas_call(
        functools.partial(_attn_kernel, seq=seq, n_lat_tiles=n_lat, with_ctx=with_ctx),
        out_shape=jax.ShapeDtypeStruct((b, nt, ATT_WIDTH), BF16),
        grid=(b, N_KV_HEADS, n_tiles),
        in_specs=[
            pl.BlockSpec((None, Q_TILE, gw), lambda bi, h, i: (bi, i, h)),
            pl.BlockSpec((None, nt, HEAD_DIM), lambda bi, h, i: (bi, 0, h)),
            pl.BlockSpec((None, nt, HEAD_DIM), lambda bi, h, i: (bi, 0, h)),
        ],
        out_specs=pl.BlockSpec((None, Q_TILE, gw), lambda bi, h, i: (bi, i, h)),
        scratch_shapes=[pltpu.VMEM((nt, 2 * HEAD_DIM), BF16)],
        compiler_params=_cparams("parallel", "parallel", "arbitrary"),
        name="attention",
    )(q, k, v)


def _retention_kernel(*refs, has_prev):
    if has_prev:
        q_ref, k_ref, v_ref, g_ref, dec_ref, xi_ref, zeta_ref, cd_ref, prev_ref, o_ref, r_ref = refs
    else:
        q_ref, k_ref, v_ref, g_ref, dec_ref, xi_ref, zeta_ref, cd_ref, o_ref, r_ref = refs

    @pl.when(pl.program_id(1) == 0)
    def _():
        r_ref[...] = jnp.zeros_like(r_ref)

    for bi in range(q_ref.shape[0]):
        for h in range(RET_HEADS):
            ks = slice(h * RET_DK, (h + 1) * RET_DK)
            vs = slice(h * RET_DV, (h + 1) * RET_DV)
            q = q_ref[bi, :, ks]
            k = k_ref[bi, :, ks]
            v = v_ref[bi, :, vs]
            state = r_ref[bi, h]
            inner = lax.dot_general(q, k, (((1,), (1,)), ((), ())), preferred_element_type=F32) * dec_ref[h]
            o = jnp.dot(inner.astype(BF16), v, preferred_element_type=F32)
            o = o + jnp.dot(q, state.astype(BF16), preferred_element_type=F32) * xi_ref[h]
            kz = (k.astype(F32) * zeta_ref[h]).astype(BF16)
            r_ref[bi, h] = state * cd_ref[h] + lax.dot_general(kz, v, (((0,), (0,)), ((), ())), preferred_element_type=F32)
            mu = jnp.mean(o, axis=-1, keepdims=True)
            d = o - mu
            var = jnp.mean(d * d, axis=-1, keepdims=True)
            y = _silu(g_ref[bi, :, vs]) * (d * lax.rsqrt(var + GN_EPS))
            if has_prev:
                y = prev_ref[bi, :, vs] + y
            o_ref[bi, :, vs] = y.astype(o_ref.dtype)


def retention_dir(rq, rk, rv, rg, gate_col, tables, chunk_of, prev, out_dtype, *, name):
    b, nt, _ = rq.shape
    nc = nt // RET_CHUNK
    L = RET_CHUNK
    bb = next(n for n in (4, 2, 1) if b % n == 0)
    tok = lambda w: pl.BlockSpec((bb, L, w), lambda bi, t: (bi, chunk_of(t), 0))
    full = lambda a: pl.BlockSpec(a.shape, lambda bi, t: (0,) * a.ndim)
    in_specs = [tok(RET_QK_WIDTH), tok(RET_QK_WIDTH), tok(RET_V_WIDTH),
                pl.BlockSpec((bb, L, RET_V_WIDTH), lambda bi, t: (bi, chunk_of(t), gate_col))]
    in_specs += [full(a) for a in tables]
    args = [rq, rk, rv, rg, *tables]
    if prev is not None:
        in_specs.append(tok(RET_V_WIDTH))
        args.append(prev)
    return pl.pallas_call(
        functools.partial(_retention_kernel, has_prev=prev is not None),
        out_shape=jax.ShapeDtypeStruct((b, nt, RET_V_WIDTH), out_dtype),
        grid=(b // bb, nc),
        in_specs=in_specs,
        out_specs=tok(RET_V_WIDTH),
        scratch_shapes=[pltpu.VMEM((bb, RET_HEADS, RET_DK, RET_DV), F32)],
        compiler_params=_cparams("parallel", "arbitrary"),
        name=name,
    )(*args)


def retention_tables(decay_logit, backward):
    L = RET_CHUNK
    lg = jax.nn.log_sigmoid(decay_logit.astype(F32))
    pos = jnp.arange(L, dtype=F32)
    diff = pos[:, None] - pos[None, :]
    if backward:
        diff = -diff
        xi_e = L - pos
        zeta_e = pos
    else:
        xi_e = pos + 1.0
        zeta_e = L - 1.0 - pos
    decay = jnp.where(diff >= 0, jnp.exp(jnp.maximum(diff, 0.0) * lg[:, None, None]), 0.0)
    xi = jnp.broadcast_to(jnp.exp(xi_e * lg[:, None])[:, :, None], (RET_HEADS, L, RET_DV))
    zeta = jnp.broadcast_to(jnp.exp(zeta_e * lg[:, None])[:, :, None], (RET_HEADS, L, RET_DK))
    cd = jnp.broadcast_to(jnp.exp(L * lg)[:, None, None], (RET_HEADS, 1, RET_DV))
    return decay, xi, zeta, cd


def _merge_kernel(a_ref, r_ref, wa_ref, wr_ref, ga_ref, gr_ref, o_ref):
    za = jnp.dot(a_ref[...], wa_ref[...], preferred_element_type=F32)
    zr = jnp.dot(r_ref[...], wr_ref[...], preferred_element_type=F32)
    o_ref[...] = (ga_ref[...] * za + gr_ref[...] * zr).astype(o_ref.dtype)


def branch_merge(att, ret, w_att_o, w_ret_o, gsig, *, tm, tn=512):
    m, ka = att.shape
    kr = ret.shape[1]
    d = w_att_o.shape[1]
    nj = d // tn
    return pl.pallas_call(
        _merge_kernel,
        out_shape=jax.ShapeDtypeStruct((m, d), BF16),
        grid=(m // tm, nj),
        in_specs=[
            pl.BlockSpec((tm, ka), lambda i, j: (i, 0)),
            pl.BlockSpec((tm, kr), lambda i, j: (i, 0)),
            pl.BlockSpec((ka, tn), lambda i, j: (0, j)),
            pl.BlockSpec((kr, tn), lambda i, j: (0, j)),
            pl.BlockSpec((tm, tn), lambda i, j: (i, j)),
            pl.BlockSpec((tm, tn), lambda i, j: (i, nj + j)),
        ],
        out_specs=pl.BlockSpec((tm, tn), lambda i, j: (i, j)),
        compiler_params=_cparams("parallel", "parallel"),
        name="branch_merge",
    )(att, ret, w_att_o, w_ret_o, gsig, gsig)


def _resid_mm_kernel(z_ref, w_ref, x_ref, gb_ref, gc_ref, o_ref, *, tm, tpb, seq):
    i = pl.program_id(0)
    gate = jnp.where(_ctx_rows(i, tm, tpb, seq), gc_ref[...], gb_ref[...])
    o_ref[...] = x_ref[...] + gate * jnp.dot(z_ref[...], w_ref[...], preferred_element_type=F32)


def out_proj_residual(z, w_out, x, gate, *, tm, tpb, seq, ctx_row, tn=512):
    m, k = z.shape
    d = w_out.shape[1]
    return pl.pallas_call(
        functools.partial(_resid_mm_kernel, tm=tm, tpb=tpb, seq=seq),
        out_shape=jax.ShapeDtypeStruct((m, d), F32),
        grid=(m // tm, d // tn),
        in_specs=[
            pl.BlockSpec((tm, k), lambda i, j: (i, 0)),
            pl.BlockSpec((k, tn), lambda i, j: (0, j)),
            pl.BlockSpec((tm, tn), lambda i, j: (i, j)),
            pl.BlockSpec((None, 1, tn), lambda i, j: (i // tpb, 0, j)),
            pl.BlockSpec((None, 1, tn), lambda i, j: (ctx_row, 0, j)),
        ],
        out_specs=pl.BlockSpec((tm, tn), lambda i, j: (i, j)),
        compiler_params=_cparams("parallel", "parallel"),
        name="out_proj",
    )(z, w_out, x, gate, gate)


def _ffn_kernel(*refs, tm, tpb, seq, with_next):
    if with_next:
        (h_ref, wg_ref, wu_ref, wd_ref, x_ref, gb_ref, gc_ref,
         ng_ref, nscb_ref, nscc_ref, nshb_ref, nshc_ref, o_ref, hn_ref, acc_ref) = refs
    else:
        h_ref, wg_ref, wu_ref, wd_ref, x_ref, gb_ref, gc_ref, o_ref, acc_ref = refs
    i, f = pl.program_id(0), pl.program_id(1)

    @pl.when(f == 0)
    def _():
        acc_ref[...] = jnp.zeros_like(acc_ref)

    h = h_ref[...]
    g = jnp.dot(h, wg_ref[...], preferred_element_type=F32)
    u = jnp.dot(h, wu_ref[...], preferred_element_type=F32)
    a = (_silu(g) * u).astype(BF16)
    acc_ref[...] += jnp.dot(a, wd_ref[...], preferred_element_type=F32)

    @pl.when(f == pl.num_programs(1) - 1)
    def _():
        rows = tm // FFN_EPILOGUE_CHUNKS
        for c in range(FFN_EPILOGUE_CHUNKS):
            rs = slice(c * rows, (c + 1) * rows)
            is_ctx = (lax.broadcasted_iota(jnp.int32, (rows, 1), 0) + ((i % tpb) * tm + c * rows)) >= seq
            xn = x_ref[rs, :] + jnp.where(is_ctx, gc_ref[...], gb_ref[...]) * acc_ref[rs, :]
            o_ref[rs, :] = xn
            if with_next:
                scale = jnp.where(is_ctx, nscc_ref[...], nscb_ref[...])
                shift = jnp.where(is_ctx, nshc_ref[...], nshb_ref[...])
                hn_ref[rs, :] = (_rms(xn, ng_ref[...]) * (1 + scale) + shift).astype(hn_ref.dtype)


def ffn_residual(h, wg, wu, wd, x, gate, next_norm, *, tm, tpb, seq, ctx_row, tf=512):
    m, d = h.shape
    ff = wg.shape[1]
    vec_b = pl.BlockSpec((None, 1, d), lambda i, f: (i // tpb, 0, 0))
    vec_c = pl.BlockSpec((None, 1, d), lambda i, f: (ctx_row, 0, 0))
    rows = pl.BlockSpec((tm, d), lambda i, f: (i, 0))
    in_specs = [
        rows,
        pl.BlockSpec((d, tf), lambda i, f: (0, f)),
        pl.BlockSpec((d, tf), lambda i, f: (0, f)),
        pl.BlockSpec((tf, d), lambda i, f: (f, 0)),
        rows, vec_b, vec_c,
    ]
    args = [h, wg, wu, wd, x, gate, gate]
    out_shape = jax.ShapeDtypeStruct((m, d), F32)
    out_specs = rows
    if next_norm is not None:
        ng, nsc, nsh = next_norm
        in_specs += [pl.BlockSpec((1, d), lambda i, f: (0, 0)), vec_b, vec_c, vec_b, vec_c]
        args += [ng.reshape(1, d), nsc, nsc, nsh, nsh]
        out_shape = (out_shape, jax.ShapeDtypeStruct((m, d), BF16))
        out_specs = (rows, rows)
    return pl.pallas_call(
        functools.partial(_ffn_kernel, tm=tm, tpb=tpb, seq=seq, with_next=next_norm is not None),
        out_shape=out_shape,
        grid=(m // tm, ff // tf),
        in_specs=in_specs,
        out_specs=out_specs,
        scratch_shapes=[pltpu.VMEM((tm, d), F32)],
        compiler_params=_cparams("parallel", "arbitrary"),
        name="ffn",
    )(*args)


def moe_plan(e1, e2, tm):
    t = e1.shape[0]
    p = 2 * t
    e = jnp.stack([e1, e2], axis=1).reshape(p)
    onehot = (e[:, None] == jnp.arange(N_EXPERTS, dtype=jnp.int32)[None, :]).astype(jnp.int32)
    csum = jnp.cumsum(onehot, axis=0)
    rank = jnp.take_along_axis(csum, e[:, None], axis=1)[:, 0] - 1
    cnt = csum[-1]
    gsz = (cnt + tm - 1) // tm * tm
    gend = jnp.cumsum(gsz)
    gstart = gend - gsz
    pos = gstart[e] + rank
    n_tiles = -(-p // tm) + N_EXPERTS
    pair_of = jnp.full((n_tiles * tm,), -1, jnp.int32).at[pos].set(jnp.arange(p, dtype=jnp.int32))
    tile_start = jnp.arange(n_tiles, dtype=jnp.int32) * tm
    tile_e = jnp.minimum(jnp.searchsorted(gend, tile_start, side="right").astype(jnp.int32), N_EXPERTS - 1)
    tile_valid = jnp.clip(gstart[tile_e] + cnt[tile_e] - tile_start, 0, tm).astype(jnp.int32)
    used = gend[-1] // tm
    tile_e = jnp.where(tile_start < gend[-1], tile_e, tile_e[jnp.maximum(used - 1, 0)])
    return pair_of, tile_e, tile_valid


def _moe_kernel(te_ref, tv_ref, src_ref, nsrc_ref, pdst_ref, h_hbm, wg_ref, wu_ref, wd_ref, pairs_hbm,
                xg_ref, xb_ref, acc_ref, ys_ref, gsem, ssem, *, tm, n_f, dump_row):
    i, f = pl.program_id(0), pl.program_id(1)
    n_i = pl.num_programs(0)
    has_rows = tv_ref[i] > 0
    slot = i % 2
    rows_per_step = tm // n_f

    def gather_copy(s, r, row):
        return pltpu.make_async_copy(h_hbm.at[pl.ds(row, 1)], xg_ref.at[s, pl.ds(r, 1)], gsem.at[s])

    def scatter_copy(r, row):
        return pltpu.make_async_copy(ys_ref.at[pl.ds(r, 1)], pairs_hbm.at[pl.ds(row, 1)], ssem)

    def wait_gather(s):
        def body(r, carry):
            gather_copy(s, r, 0).wait()
            return carry
        lax.fori_loop(0, tm, body, 0, unroll=8)

    def wait_scatter():
        def body(r, carry):
            scatter_copy(r, 0).wait()
            return carry
        lax.fori_loop(0, tm, body, 0, unroll=8)

    def start_step_dmas():
        base = f * rows_per_step
        for j in range(rows_per_step):
            r = base + j
            gather_copy(1 - slot, r, nsrc_ref[0, r]).start()
            scatter_copy(r, jnp.where(i == 0, dump_row + r, pdst_ref[0, r])).start()

    @pl.when((i == 0) & (f == 0))
    def _():
        ys_ref[...] = jnp.zeros_like(ys_ref)

        def body(r, carry):
            gather_copy(0, r, src_ref[0, r]).start()
            return carry
        lax.fori_loop(0, tm, body, 0, unroll=8)

    @pl.when(f == 0)
    def _():
        wait_gather(slot)

    @pl.when(has_rows)
    def _():
        @pl.when(f == 0)
        def _():
            xb_ref[...] = xg_ref[slot].astype(BF16)
            acc_ref[...] = jnp.zeros_like(acc_ref)

        start_step_dmas()
        x = xb_ref[...]
        g = jnp.dot(x, wg_ref[...].astype(BF16), preferred_element_type=F32)
        u = jnp.dot(x, wu_ref[...].astype(BF16), preferred_element_type=F32)
        a = (_silu(g) * u).astype(BF16)
        acc_ref[...] += jnp.dot(a, wd_ref[...].astype(BF16), preferred_element_type=F32)

    @pl.when(jnp.logical_not(has_rows))
    def _():
        start_step_dmas()

    @pl.when(f == n_f - 1)
    def _():
        wait_scatter()

        @pl.when(has_rows)
        def _():
            ys_ref[...] = acc_ref[...]

        @pl.when(i == n_i - 1)
        def _():
            wait_gather(1 - slot)


def moe_experts(h, wg, wu, wd, plan, t, src_of_token, *, tm, tf):
    pair_of, tile_e, tile_valid = plan
    n_tiles = tile_e.shape[0]
    _, d, ff = wg.shape
    nf = ff // tf
    assert tm % nf == 0
    real = pair_of >= 0
    pr = jnp.maximum(pair_of, 0)
    dump = 2 * t + jnp.arange(n_tiles * tm, dtype=jnp.int32) % tm
    row_src = jnp.where(real, src_of_token(pr // 2), 0).reshape(n_tiles, 1, tm)
    row_dst = jnp.where(real, (pr % 2) * t + pr // 2, dump).reshape(n_tiles, 1, tm)
    fe = lambda i, f, te, tv: jnp.where(tv[i] > 0, f, nf - 1)
    rows_at = lambda off: pl.BlockSpec((None, 1, tm), lambda i, f, te, tv: (jnp.clip(i + off, 0, n_tiles - 1), 0, 0),
                                       memory_space=pltpu.SMEM)
    grid_spec = pltpu.PrefetchScalarGridSpec(
        num_scalar_prefetch=2,
        grid=(n_tiles, nf),
        in_specs=[
            rows_at(0), rows_at(1), rows_at(-1),
            pl.BlockSpec(memory_space=pl.ANY),
            pl.BlockSpec((None, d, tf), lambda i, f, te, tv: (te[i], 0, fe(i, f, te, tv))),
            pl.BlockSpec((None, d, tf), lambda i, f, te, tv: (te[i], 0, fe(i, f, te, tv))),
            pl.BlockSpec((None, tf, d), lambda i, f, te, tv: (te[i], fe(i, f, te, tv), 0)),
        ],
        out_specs=pl.BlockSpec(memory_space=pl.ANY),
        scratch_shapes=[pltpu.VMEM((2, tm, d), F32), pltpu.VMEM((tm, d), BF16), pltpu.VMEM((tm, d), F32),
                        pltpu.VMEM((tm, d), F32), pltpu.SemaphoreType.DMA((2,)), pltpu.SemaphoreType.DMA(())],
    )
    return pl.pallas_call(
        functools.partial(_moe_kernel, tm=tm, n_f=nf, dump_row=2 * t),
        out_shape=jax.ShapeDtypeStruct((2 * t + tm, d), F32),
        grid_spec=grid_spec,
        compiler_params=pltpu.CompilerParams(dimension_semantics=("arbitrary", "arbitrary"),
                                             vmem_limit_bytes=VMEM_LIMIT, disable_bounds_checks=True),
        name="moe_experts",
    )(tile_e, tile_valid, row_src, row_src, row_dst, h, wg, wu, wd)


def _combine_final_kernel(p0_ref, p1_ref, rec_ref, x_ref, g_ref, fg_ref, o_ref):
    rec = rec_ref[...]
    y = rec[:, 2:3] * p0_ref[...] + rec[:, 3:4] * p1_ref[...]
    o_ref[...] = _rms(x_ref[...] + g_ref[...] * y, fg_ref[...])


def _rope_tables(ang):
    c, s = jnp.cos(ang), jnp.sin(ang)
    return jnp.concatenate([c, c], axis=-1), jnp.concatenate([-s, s], axis=-1)


def _axial_angles(seq):
    rows = seq // GRID_W
    row = jnp.repeat(jnp.arange(rows), GRID_W).astype(F32)
    col = jnp.tile(jnp.arange(GRID_W), rows).astype(F32)
    n_freq = HEAD_DIM // 4
    inv = ROPE_THETA ** (-jnp.arange(n_freq, dtype=F32) / n_freq)
    return jnp.concatenate([row[:, None] * inv, col[:, None] * inv], axis=-1)


def _linear_angles(pos, dim):
    n_freq = dim // 2
    inv = ROPE_THETA ** (-jnp.arange(n_freq, dtype=F32) / n_freq)
    return pos.astype(F32)[:, None] * inv


def kernel(x, c, ctx, c_ctx, w_ada, b_ada, norm1_g, norm2_g, w_in, q_norm_g, k_norm_g, ret_decay_f, ret_decay_b, w_att_o, w_ret_o, w_out, ffn_w_gate, ffn_w_up, ffn_w_down, moe_router, moe_w_gate, moe_w_up, moe_w_down, final_g):
    b, seq, d = x.shape
    ctx_len = ctx.shape[1]
    depth = w_ada.shape[0]
    assert ctx_len == Q_TILE and seq % Q_TILE == 0 and seq % GRID_W == 0
    nt = seq + ctx_len
    m = b * nt
    tpb_big, tpb_ffn = 4, 8
    tm_big, tm_ffn = nt // tpb_big, nt // tpb_ffn
    tm_norm, tpb_norm = tm_big, tpb_big
    ctx_row = b
    nchunks = nt // RET_CHUNK
    nc_lat = seq // RET_CHUNK

    cos_a, sin_a = _rope_tables(_axial_angles(seq))
    cos_a = jnp.concatenate([cos_a, jnp.ones((ctx_len, HEAD_DIM), F32)], axis=0)
    sin_a = jnp.concatenate([sin_a, jnp.zeros((ctx_len, HEAD_DIM), F32)], axis=0)
    pos = jnp.concatenate([ctx_len + jnp.arange(seq), jnp.arange(ctx_len)])
    cos_r, sin_r = _rope_tables(_linear_angles(pos, RET_DK))

    n_rows = -(-(b + 1) // 16) * 16
    cc = jnp.zeros((n_rows, d), F32).at[:b].set(c).at[b].set(c_ctx)
    mods = adaln_all(cc, w_ada, b_ada)
    mods = mods.reshape(depth, n_rows, N_MOD, d).transpose(0, 2, 1, 3).reshape(depth, N_MOD, n_rows, 1, d)

    xa = jnp.concatenate([x, ctx], axis=1).reshape(m, d)

    def norm_mod(xcur, g, sc, sh):
        sb = pl.BlockSpec((None, 1, d), lambda i: (i // tpb_norm, 0, 0))
        scx = pl.BlockSpec((None, 1, d), lambda i: (ctx_row, 0, 0))
        row = pl.BlockSpec((tm_norm, d), lambda i: (i, 0))
        return pl.pallas_call(
            functools.partial(_norm_mod_kernel, tm=tm_norm, tpb=tpb_norm, seq=seq),
            out_shape=jax.ShapeDtypeStruct((m, d), BF16),
            grid=(m // tm_norm,),
            in_specs=[row, pl.BlockSpec((1, d), lambda i: (0, 0)), sb, scx, sb, scx],
            out_specs=row,
            compiler_params=_cparams("parallel"),
            name="norm_mod",
        )(xcur, g.reshape(1, d), sc, sc, sh, sh)

    tok_tab = lambda t: (t, (tm_big, HEAD_DIM), lambda i, j: (i % tpb_big, 0))
    row_vec = lambda v: (v.reshape(1, -1), (1, v.shape[-1]), lambda i, j: (0, 0))
    rope_scaled = functools.partial(_ep_rope, scale=RET_DK ** -0.5)
    rope_plain = functools.partial(_ep_rope, scale=1.0)
    fwd_chunk = lambda t: (t + nc_lat) % nchunks
    bwd_chunk = lambda t: nchunks - 1 - t

    c0 = 0
    c_q, c0 = c0, c0 + ATT_WIDTH
    c_k, c0 = c0, c0 + KV_WIDTH
    c_v, c0 = c0, c0 + KV_WIDTH
    c_rq, c0 = c0, c0 + RET_QK_WIDTH
    c_rk, c0 = c0, c0 + RET_QK_WIDTH
    c_rv, c0 = c0, c0 + RET_V_WIDTH
    c_rg, c0 = c0, c0 + 2 * RET_V_WIDTH
    c_bg, c0 = c0, c0 + 2 * d

    h_next = None
    for layer in range(depth):
        last = layer == depth - 1
        sh1, sc1, g1, sh2, sc2, g2 = [mods[layer, n] for n in range(N_MOD)]
        tn_in = 512 if d % 512 == 0 else 256

        if h_next is None:
            h = norm_mod(xa, norm1_g[layer], sc1, sh1)
        else:
            h = h_next
        pj = functools.partial(proj, h, w_in, layer, tm=tm_big, tn=tn_in)
        rc = 4 if tm_big % 64 == 0 else 1
        q = pj(c_q, ATT_WIDTH, BF16, _ep_head_norm_rope, [row_vec(q_norm_g[layer]), tok_tab(cos_a), tok_tab(sin_a)], row_chunks=rc, name="proj_q")
        k = pj(c_k, KV_WIDTH, BF16, _ep_head_norm_rope, [row_vec(k_norm_g[layer]), tok_tab(cos_a), tok_tab(sin_a)], row_chunks=rc, name="proj_k")
        v = pj(c_v, KV_WIDTH, BF16, _ep_plain, name="proj_v")
        rq = pj(c_rq, RET_QK_WIDTH, BF16, rope_plain, [tok_tab(cos_r), tok_tab(sin_r)], row_chunks=rc, name="proj_rq")
        rk = pj(c_rk, RET_QK_WIDTH, BF16, rope_scaled, [tok_tab(cos_r), tok_tab(sin_r)], row_chunks=rc, name="proj_rk")
        rv = pj(c_rv, RET_V_WIDTH, BF16, _ep_plain, name="proj_rv")
        rg = pj(c_rg, 2 * RET_V_WIDTH, F32, _ep_plain, name="proj_rg")
        gsig = pj(c_bg, 2 * d, F32, _ep_sigmoid, row_chunks=rc, name="proj_gates")

        r3 = lambda a: a.reshape(b, nt, a.shape[-1])
        att = attention(r3(q), r3(k), r3(v), seq=seq, with_ctx=not last).reshape(m, ATT_WIDTH)
        tf_ = retention_tables(ret_decay_f[layer], backward=False)
        tb_ = retention_tables(ret_decay_b[layer], backward=True)
        yf = retention_dir(r3(rq), r3(rk), r3(rv), r3(rg), 0, tf_, fwd_chunk, None, F32, name="retention_fwd")
        ret = retention_dir(r3(rq), r3(rk), r3(rv), r3(rg), 1, tb_, bwd_chunk, yf, BF16, name="retention_bwd").reshape(m, RET_V_WIDTH)

        z = branch_merge(att, ret, w_att_o[layer].astype(BF16), w_ret_o[layer].astype(BF16), gsig, tm=tm_big, tn=tn_in)
        xa = out_proj_residual(z, w_out[layer].astype(BF16), xa, g1, tm=tm_big, tpb=tpb_big, seq=seq, ctx_row=ctx_row, tn=tn_in)

        j = layer // 2
        if layer % 2 == 0:
            h2 = norm_mod(xa, norm2_g[layer], sc2, sh2)
            next_norm = None if last else (norm1_g[layer + 1], mods[layer + 1, 1], mods[layer + 1, 0])
            res = ffn_residual(h2, ffn_w_gate[j].astype(BF16), ffn_w_up[j].astype(BF16), ffn_w_down[j].astype(BF16),
                               xa, g2, next_norm, tm=tm_ffn, tpb=tpb_ffn, seq=seq, ctx_row=ctx_row)
            xa, h_next = (res, None) if last else res
        else:
            assert last, "routed-expert layers before the last layer are not supported"
            wr = jnp.zeros((d, LANES), F32).at[:, :N_EXPERTS].set(moe_router[j])
            sb = pl.BlockSpec((None, 1, d), lambda i: (i // tpb_norm, 0, 0))
            scx = pl.BlockSpec((None, 1, d), lambda i: (ctx_row, 0, 0))
            row = pl.BlockSpec((tm_norm, d), lambda i: (i, 0))
            h2, rec = pl.pallas_call(
                functools.partial(_norm_mod_router_kernel, tm=tm_norm, tpb=tpb_norm, seq=seq),
                out_shape=(jax.ShapeDtypeStruct((m, d), F32), jax.ShapeDtypeStruct((m, LANES), F32)),
                grid=(m // tm_norm,),
                in_specs=[row, pl.BlockSpec((1, d), lambda i: (0, 0)), sb, scx, sb, scx, pl.BlockSpec((d, LANES), lambda i: (0, 0))],
                out_specs=(row, pl.BlockSpec((tm_norm, LANES), lambda i: (i, 0))),
                compiler_params=_cparams("parallel"),
                name="norm_mod_router",
            )(xa, norm2_g[layer].reshape(1, d), sc2, sc2, sh2, sh2, wr)
            e12 = rec.reshape(b, nt, LANES)[:, :seq, :2].reshape(b * seq, 2).astype(jnp.int32)
            t_tok = b * seq
            eff = moe_w_gate.shape[-1]
            tf_moe = 512 if eff % 512 == 0 else eff
            row_step = int(np.lcm(eff // tf_moe, 16))
            tm_moe = max(784 // row_step, 1) * row_step
            plan = moe_plan(e12[:, 0], e12[:, 1], tm_moe)
            pairs = moe_experts(h2, moe_w_gate[j], moe_w_up[j], moe_w_down[j],
                                plan, t_tok, lambda tok: (tok // seq) * nt + tok % seq, tm=tm_moe, tf=tf_moe)
            tmc = Q_TILE
            spb = seq // tmc
            return pl.pallas_call(
                _combine_final_kernel,
                out_shape=jax.ShapeDtypeStruct((b, seq, d), F32),
                grid=(b, spb),
                in_specs=[
                    pl.BlockSpec((tmc, d), lambda bi, i: (bi * spb + i, 0)),
                    pl.BlockSpec((tmc, d), lambda bi, i: (t_tok // tmc + bi * spb + i, 0)),
                    pl.BlockSpec((tmc, LANES), lambda bi, i: (bi * (nt // tmc) + i, 0)),
                    pl.BlockSpec((None, tmc, d), lambda bi, i: (bi, i, 0)),
                    pl.BlockSpec((None, 1, d), lambda bi, i: (bi, 0, 0)),
                    pl.BlockSpec((1, d), lambda bi, i: (0, 0)),
                ],
                out_specs=pl.BlockSpec((None, tmc, d), lambda bi, i: (bi, i, 0)),
                compiler_params=_cparams("parallel", "parallel"),
                name="combine_final",
            )(pairs, pairs, rec, xa.reshape(b, nt, d), g2, final_g.reshape(1, d))

    tm_f = Q_TILE
    return pl.pallas_call(
        _final_norm_kernel,
        out_shape=jax.ShapeDtypeStruct((b, seq, d), F32),
        grid=(b, seq // tm_f),
        in_specs=[pl.BlockSpec((None, tm_f, d), lambda bi, i: (bi, i, 0)), pl.BlockSpec((1, d), lambda bi, i: (0, 0))],
        out_specs=pl.BlockSpec((None, tm_f, d), lambda bi, i: (bi, i, 0)),
        compiler_params=_cparams("parallel", "parallel"),
        name="final_norm",
    )(xa.reshape(b, nt, d), final_g.reshape(1, d))
```

```python
import functools

import jax
import jax.numpy as jnp
import numpy as np
from jax import lax
from jax.experimental import pallas as pl
from jax.experimental.pallas import tpu as pltpu

F32 = jnp.float32
BF16 = jnp.bfloat16

GRID_W = 64
N_HEADS = 16
N_KV_HEADS = 4
HEAD_DIM = 128
KV_GROUP = N_HEADS // N_KV_HEADS
ATT_WIDTH = N_HEADS * HEAD_DIM
KV_WIDTH = N_KV_HEADS * HEAD_DIM
ROPE_THETA = 10000.0
RET_HEADS = 8
RET_DK = 128
RET_DV = 256
RET_QK_WIDTH = RET_HEADS * RET_DK
RET_V_WIDTH = RET_HEADS * RET_DV
RET_CHUNK = 128
N_EXPERTS = 8
N_MOD = 6
EPS = 1e-6
GN_EPS = 1e-5

Q_TILE = 256
ATTN_ROW_CHUNKS = 2
LANES = 128
VMEM_LIMIT = 56 * 2**20


def _cparams(*sem):
    return pltpu.CompilerParams(dimension_semantics=sem, vmem_limit_bytes=VMEM_LIMIT)


def _silu(v):
    return v * jax.nn.sigmoid(v)


def _adaln_kernel(c_ref, w_ref, b_ref, o_ref):
    s = _silu(c_ref[...]).astype(BF16)
    o_ref[...] = jnp.dot(s, w_ref[...].astype(BF16), preferred_element_type=F32) + b_ref[...]


def adaln_all(cc, w_ada, b_ada):
    depth, d, n = w_ada.shape
    r = cc.shape[0]
    tn = 1024 if n % 1024 == 0 else n
    return pl.pallas_call(
        _adaln_kernel,
        out_shape=jax.ShapeDtypeStruct((depth, r, n), F32),
        grid=(depth, n // tn),
        in_specs=[
            pl.BlockSpec((r, d), lambda l, j: (0, 0)),
            pl.BlockSpec((None, d, tn), lambda l, j: (l, 0, j)),
            pl.BlockSpec((None, 1, tn), lambda l, j: (l, 0, j)),
        ],
        out_specs=pl.BlockSpec((None, r, tn), lambda l, j: (l, 0, j)),
        compiler_params=_cparams("parallel", "parallel"),
        name="adaln",
    )(cc, w_ada, b_ada.reshape(depth, 1, n))


def _ctx_rows(i, tm, tiles_per_batch, seq):
    rows = lax.broadcasted_iota(jnp.int32, (tm, 1), 0) + (i % tiles_per_batch) * tm
    return rows >= seq


def _rms(x, g):
    return x * lax.rsqrt(jnp.mean(x * x, axis=-1, keepdims=True) + EPS) * g


def _norm_mod_kernel(x_ref, g_ref, scb_ref, scc_ref, shb_ref, shc_ref, o_ref, *, tm, tpb, seq):
    i = pl.program_id(0)
    is_ctx = _ctx_rows(i, tm, tpb, seq)
    y = _rms(x_ref[...], g_ref[...])
    scale = jnp.where(is_ctx, scc_ref[...], scb_ref[...])
    shift = jnp.where(is_ctx, shc_ref[...], shb_ref[...])
    o_ref[...] = (y * (1 + scale) + shift).astype(o_ref.dtype)


def _norm_mod_router_kernel(x_ref, g_ref, scb_ref, scc_ref, shb_ref, shc_ref, wr_ref, o_ref, gate_ref, *, tm, tpb, seq):
    is_ctx = _ctx_rows(pl.program_id(0), tm, tpb, seq)
    y = _rms(x_ref[...], g_ref[...])
    scale = jnp.where(is_ctx, scc_ref[...], scb_ref[...])
    shift = jnp.where(is_ctx, shc_ref[...], shb_ref[...])
    t = y * (1 + scale) + shift
    o_ref[...] = t.astype(o_ref.dtype)
    logits = jnp.dot(t, wr_ref[...], preferred_element_type=F32, precision=lax.Precision.HIGHEST)
    lane = lax.broadcasted_iota(jnp.int32, logits.shape, 1)
    neg = jnp.float32(-jnp.inf)
    logits = jnp.where(lane < N_EXPERTS, logits, neg)
    m1 = jnp.max(logits, axis=-1, keepdims=True)
    i1 = jnp.min(jnp.where(logits == m1, lane, LANES), axis=-1, keepdims=True)
    rest = jnp.where(lane == i1, neg, logits)
    m2 = jnp.max(rest, axis=-1, keepdims=True)
    i2 = jnp.min(jnp.where(rest == m2, lane, LANES), axis=-1, keepdims=True)
    e2 = jnp.exp(m2 - m1)
    w1 = 1.0 / (1.0 + e2)
    w2 = e2 / (1.0 + e2)
    rec = jnp.where(lane == 0, i1.astype(F32), jnp.where(lane == 1, i2.astype(F32), jnp.where(lane == 2, w1, w2)))
    gate_ref[...] = jnp.where(lane < 4, rec, 0.0)


def _final_norm_kernel(x_ref, g_ref, o_ref):
    o_ref[...] = _rms(x_ref[...], g_ref[...])


def _mm_kernel(x_ref, w_ref, *rest, segments, row_chunks):
    o_ref = rest[-1]
    j = pl.program_id(1)
    tm = x_ref.shape[0]
    rows = tm // row_chunks
    for j_lo, j_hi, epilogue, extra_idx in segments:
        @pl.when((j >= j_lo) & (j < j_hi))
        def _():
            for c in range(row_chunks):
                rs = slice(c * rows, (c + 1) * rows)
                acc = jnp.dot(x_ref[rs, :], w_ref[...], preferred_element_type=F32)
                extras = [rest[e][rs, :] if rest[e].shape[0] == tm else rest[e][...] for e in extra_idx]
                o_ref[rs, :] = epilogue(acc, *extras).astype(o_ref.dtype)


def _rope(y, cos, sin):
    return y * cos + pltpu.roll(y, HEAD_DIM // 2, 1) * sin


def _ep_plain(acc):
    return acc


def _ep_sigmoid(acc):
    return jax.nn.sigmoid(acc)


def _ep_head_norm_rope(acc, g, cos, sin):
    outs = []
    for h in range(acc.shape[1] // HEAD_DIM):
        y = _rms(acc[:, h * HEAD_DIM:(h + 1) * HEAD_DIM], g)
        outs.append(_rope(y, cos, sin))
    return jnp.concatenate(outs, axis=1)


def _ep_rope(acc, cos, sin, *, scale):
    outs = []
    for h in range(acc.shape[1] // RET_DK):
        y = _rope(acc[:, h * RET_DK:(h + 1) * RET_DK], cos, sin)
        outs.append(y * scale if scale != 1.0 else y)
    return jnp.concatenate(outs, axis=1)


def proj(h, w, layer, col0, segments, out_dtype, extras, out_block, *, tm, tn, row_chunks, name):
    m, k = h.shape
    assert col0 % tn == 0 and m % tm == 0 and tm % (16 * row_chunks) == 0
    jb = col0 // tn
    segs, j0 = [], 0
    for width, epilogue, extra_idx in segments:
        assert width % tn == 0
        segs.append((j0, j0 + width // tn, epilogue, tuple(extra_idx)))
        j0 += width // tn
    in_specs = [
        pl.BlockSpec((tm, k), lambda i, j: (i, 0)),
        pl.BlockSpec((None, k, tn), lambda i, j: (layer, 0, jb + j)),
    ]
    args = [h, w]
    for arr, bs, im in extras:
        in_specs.append(pl.BlockSpec(bs, im))
        args.append(arr)
    return pl.pallas_call(
        functools.partial(_mm_kernel, segments=tuple(segs), row_chunks=row_chunks),
        out_shape=jax.ShapeDtypeStruct((m, j0 * tn), out_dtype),
        grid=(m // tm, j0),
        in_specs=in_specs,
        out_specs=pl.BlockSpec((tm, tn), lambda i, j: (i, out_block(j))),
        compiler_params=_cparams("parallel", "parallel"),
        name=name,
    )(*args)


def _attn_kernel(q_ref, k_ref, v_ref, o_ref, v1_ref, *, seq, n_lat_tiles, with_ctx):
    qi = pl.program_id(2)
    c = HEAD_DIM ** -0.5 * np.log2(np.e)

    @pl.when(qi == 0)
    def _():
        v1_ref[:, :HEAD_DIM] = v_ref[...]
        v1_ref[:, HEAD_DIM:] = jnp.ones(v_ref.shape, v1_ref.dtype)

    def run(k, v1):
        rows = q_ref.shape[0] // ATTN_ROW_CHUNKS
        for g in range(KV_GROUP):
            for r in range(ATTN_ROW_CHUNKS):
                rs = slice(r * rows, (r + 1) * rows)
                q = q_ref[rs, g * HEAD_DIM:(g + 1) * HEAD_DIM]
                s = lax.dot_general(q, k, (((1,), (1,)), ((), ())), preferred_element_type=F32)
                m = jnp.max(s, axis=-1, keepdims=True)
                p = jnp.exp2((s - m) * c).astype(BF16)
                o = jnp.dot(p, v1, preferred_element_type=F32)
                o_ref[rs, g * HEAD_DIM:(g + 1) * HEAD_DIM] = (o[:, :HEAD_DIM] / o[:, HEAD_DIM:]).astype(o_ref.dtype)

    @pl.when(qi < n_lat_tiles)
    def _():
        run(k_ref[...], v1_ref[...])

    @pl.when(qi >= n_lat_tiles)
    def _():
        if with_ctx:
            run(k_ref[seq:, :], v1_ref[seq:, :])
        else:
            o_ref[...] = jnp.zeros_like(o_ref)


def attention(qkv, q_col, k_col, v_col, *, seq, with_ctx):
    b, nt, _ = qkv.shape
    n_lat = seq // Q_TILE
    n_tiles = nt // Q_TILE
    gw = KV_GROUP * HEAD_DIM
    return pl.pallas_call(
        functools.partial(_attn_kernel, seq=seq, n_lat_tiles=n_lat, with_ctx=with_ctx),
        out_shape=jax.ShapeDtypeStruct((b, nt, ATT_WIDTH), BF16),
        grid=(b, N_KV_HEADS, n_tiles),
        in_specs=[
            pl.BlockSpec((None, Q_TILE, gw), lambda bi, h, i: (bi, i, q_col // gw + h)),
            pl.BlockSpec((None, nt, HEAD_DIM), lambda bi, h, i: (bi, 0, k_col // HEAD_DIM + h)),
            pl.BlockSpec((None, nt, HEAD_DIM), lambda bi, h, i: (bi, 0, v_col // HEAD_DIM + h)),
        ],
        out_specs=pl.BlockSpec((None, Q_TILE, gw), lambda bi, h, i: (bi, i, h)),
        scratch_shapes=[pltpu.VMEM((nt, 2 * HEAD_DIM), BF16)],
        compiler_params=_cparams("parallel", "parallel", "arbitrary"),
        name="attention",
    )(qkv, qkv, qkv)


def _retention_kernel(*refs, has_prev):
    if has_prev:
        q_ref, k_ref, v_ref, g_ref, dec_ref, xi_ref, zeta_ref, cd_ref, prev_ref, o_ref, r_ref = refs
    else:
        q_ref, k_ref, v_ref, g_ref, dec_ref, xi_ref, zeta_ref, cd_ref, o_ref, r_ref = refs

    @pl.when(pl.program_id(1) == 0)
    def _():
        r_ref[...] = jnp.zeros_like(r_ref)

    for bi in range(q_ref.shape[0]):
        for h in range(RET_HEADS):
            ks = slice(h * RET_DK, (h + 1) * RET_DK)
            vs = slice(h * RET_DV, (h + 1) * RET_DV)
            q = q_ref[bi, :, ks]
            k = k_ref[bi, :, ks]
            v = v_ref[bi, :, vs]
            state = r_ref[bi, h]
            inner = lax.dot_general(q, k, (((1,), (1,)), ((), ())), preferred_element_type=F32) * dec_ref[h]
            o = jnp.dot(inner.astype(BF16), v, preferred_element_type=F32)
            o = o + jnp.dot(q, state.astype(BF16), preferred_element_type=F32) * xi_ref[h]
            kz = (k.astype(F32) * zeta_ref[h]).astype(BF16)
            r_ref[bi, h] = state * cd_ref[h] + lax.dot_general(kz, v, (((0,), (0,)), ((), ())), preferred_element_type=F32)
            mu = jnp.mean(o, axis=-1, keepdims=True)
            d = o - mu
            var = jnp.mean(d * d, axis=-1, keepdims=True)
            y = _silu(g_ref[bi, :, vs]) * (d * lax.rsqrt(var + GN_EPS))
            if has_prev:
                y = prev_ref[bi, :, vs] + y
            o_ref[bi, :, vs] = y.astype(o_ref.dtype)


def retention_dir(qkv, q_col, k_col, v_col, rg, gate_col, tables, chunk_of, prev, out_dtype, *, name):
    b, nt, _ = qkv.shape
    nc = nt // RET_CHUNK
    L = RET_CHUNK
    bb = next(n for n in (4, 2, 1) if b % n == 0)
    tok = lambda w, col=0: pl.BlockSpec((bb, L, w), lambda bi, t: (bi, chunk_of(t), col // w))
    full = lambda a: pl.BlockSpec(a.shape, lambda bi, t: (0,) * a.ndim)
    assert q_col % RET_QK_WIDTH == 0 and k_col % RET_QK_WIDTH == 0 and v_col % RET_V_WIDTH == 0
    in_specs = [tok(RET_QK_WIDTH, q_col), tok(RET_QK_WIDTH, k_col), tok(RET_V_WIDTH, v_col),
                tok(RET_V_WIDTH, gate_col * RET_V_WIDTH)]
    in_specs += [full(a) for a in tables]
    args = [qkv, qkv, qkv, rg, *tables]
    if prev is not None:
        in_specs.append(tok(RET_V_WIDTH))
        args.append(prev)
    return pl.pallas_call(
        functools.partial(_retention_kernel, has_prev=prev is not None),
        out_shape=jax.ShapeDtypeStruct((b, nt, RET_V_WIDTH), out_dtype),
        grid=(b // bb, nc),
        in_specs=in_specs,
        out_specs=tok(RET_V_WIDTH),
        scratch_shapes=[pltpu.VMEM((bb, RET_HEADS, RET_DK, RET_DV), F32)],
        compiler_params=_cparams("parallel", "arbitrary"),
        name=name,
    )(*args)


def retention_tables(decay_logit, backward):
    L = RET_CHUNK
    lg = jax.nn.log_sigmoid(decay_logit.astype(F32))
    pos = jnp.arange(L, dtype=F32)
    diff = pos[:, None] - pos[None, :]
    if backward:
        diff = -diff
        xi_e = L - pos
        zeta_e = pos
    else:
        xi_e = pos + 1.0
        zeta_e = L - 1.0 - pos
    decay = jnp.where(diff >= 0, jnp.exp(jnp.maximum(diff, 0.0) * lg[:, None, None]), 0.0)
    xi = jnp.broadcast_to(jnp.exp(xi_e * lg[:, None])[:, :, None], (RET_HEADS, L, RET_DV))
    zeta = jnp.broadcast_to(jnp.exp(zeta_e * lg[:, None])[:, :, None], (RET_HEADS, L, RET_DK))
    cd = jnp.broadcast_to(jnp.exp(L * lg)[:, None, None], (RET_HEADS, 1, RET_DV))
    return decay, xi, zeta, cd


def _merge_kernel(a_ref, r_ref, wa_ref, wr_ref, ga_ref, gr_ref, o_ref):
    za = jnp.dot(a_ref[...], wa_ref[...], preferred_element_type=F32)
    zr = jnp.dot(r_ref[...], wr_ref[...], preferred_element_type=F32)
    o_ref[...] = (ga_ref[...] * za + gr_ref[...] * zr).astype(o_ref.dtype)


def branch_merge(att, ret, w_att_o, w_ret_o, gates, gate_col, *, tm, tn=512):
    m, ka = att.shape
    kr = ret.shape[1]
    d = w_att_o.shape[1]
    nj = d // tn
    return pl.pallas_call(
        _merge_kernel,
        out_shape=jax.ShapeDtypeStruct((m, d), BF16),
        grid=(m // tm, nj),
        in_specs=[
            pl.BlockSpec((tm, ka), lambda i, j: (i, 0)),
            pl.BlockSpec((tm, kr), lambda i, j: (i, 0)),
            pl.BlockSpec((ka, tn), lambda i, j: (0, j)),
            pl.BlockSpec((kr, tn), lambda i, j: (0, j)),
            pl.BlockSpec((tm, tn), lambda i, j: (i, gate_col // tn + j)),
            pl.BlockSpec((tm, tn), lambda i, j: (i, gate_col // tn + nj + j)),
        ],
        out_specs=pl.BlockSpec((tm, tn), lambda i, j: (i, j)),
        compiler_params=_cparams("parallel", "parallel"),
        name="branch_merge",
    )(att, ret, w_att_o, w_ret_o, gates, gates)


def _resid_mm_kernel(z_ref, w_ref, x_ref, gb_ref, gc_ref, o_ref, *, tm, tpb, seq):
    i = pl.program_id(0)
    gate = jnp.where(_ctx_rows(i, tm, tpb, seq), gc_ref[...], gb_ref[...])
    o_ref[...] = x_ref[...] + gate * jnp.dot(z_ref[...], w_ref[...], preferred_element_type=F32)


def out_proj_residual(z, w_out, x, gate, *, tm, tpb, seq, ctx_row, tn=512):
    m, k = z.shape
    d = w_out.shape[1]
    return pl.pallas_call(
        functools.partial(_resid_mm_kernel, tm=tm, tpb=tpb, seq=seq),
        out_shape=jax.ShapeDtypeStruct((m, d), F32),
        grid=(m // tm, d // tn),
        in_specs=[
            pl.BlockSpec((tm, k), lambda i, j: (i, 0)),
            pl.BlockSpec((k, tn), lambda i, j: (0, j)),
            pl.BlockSpec((tm, tn), lambda i, j: (i, j)),
            pl.BlockSpec((None, 1, tn), lambda i, j: (i // tpb, 0, j)),
            pl.BlockSpec((None, 1, tn), lambda i, j: (ctx_row, 0, j)),
        ],
        out_specs=pl.BlockSpec((tm, tn), lambda i, j: (i, j)),
        compiler_params=_cparams("parallel", "parallel"),
        name="out_proj",
    )(z, w_out, x, gate, gate)


def _ffn_kernel(h_ref, wg_ref, wu_ref, wd_ref, x_ref, gb_ref, gc_ref, o_ref, acc_ref, *, tm, tpb, seq):
    i, f = pl.program_id(0), pl.program_id(1)

    @pl.when(f == 0)
    def _():
        acc_ref[...] = jnp.zeros_like(acc_ref)

    h = h_ref[...]
    g = jnp.dot(h, wg_ref[...], preferred_element_type=F32)
    u = jnp.dot(h, wu_ref[...], preferred_element_type=F32)
    a = (_silu(g) * u).astype(BF16)
    acc_ref[...] += jnp.dot(a, wd_ref[...], preferred_element_type=F32)

    @pl.when(f == pl.num_programs(1) - 1)
    def _():
        gate = jnp.where(_ctx_rows(i, tm, tpb, seq), gc_ref[...], gb_ref[...])
        o_ref[...] = x_ref[...] + gate * acc_ref[...]


def ffn_residual(h, wg, wu, wd, x, gate, *, tm, tpb, seq, ctx_row, tf=512):
    m, d = h.shape
    ff = wg.shape[1]
    return pl.pallas_call(
        functools.partial(_ffn_kernel, tm=tm, tpb=tpb, seq=seq),
        out_shape=jax.ShapeDtypeStruct((m, d), F32),
        grid=(m // tm, ff // tf),
        in_specs=[
            pl.BlockSpec((tm, d), lambda i, f: (i, 0)),
            pl.BlockSpec((d, tf), lambda i, f: (0, f)),
            pl.BlockSpec((d, tf), lambda i, f: (0, f)),
            pl.BlockSpec((tf, d), lambda i, f: (f, 0)),
            pl.BlockSpec((tm, d), lambda i, f: (i, 0)),
            pl.BlockSpec((None, 1, d), lambda i, f: (i // tpb, 0, 0)),
            pl.BlockSpec((None, 1, d), lambda i, f: (ctx_row, 0, 0)),
        ],
        out_specs=pl.BlockSpec((tm, d), lambda i, f: (i, 0)),
        scratch_shapes=[pltpu.VMEM((tm, d), F32)],
        compiler_params=_cparams("parallel", "arbitrary"),
        name="ffn",
    )(h, wg, wu, wd, x, gate, gate)


def moe_plan(e1, e2, tm):
    t = e1.shape[0]
    p = 2 * t
    e = jnp.stack([e1, e2], axis=1).reshape(p)
    onehot = (e[:, None] == jnp.arange(N_EXPERTS, dtype=jnp.int32)[None, :]).astype(jnp.int32)
    csum = jnp.cumsum(onehot, axis=0)
    rank = jnp.take_along_axis(csum, e[:, None], axis=1)[:, 0] - 1
    cnt = csum[-1]
    gsz = (cnt + tm - 1) // tm * tm
    gend = jnp.cumsum(gsz)
    gstart = gend - gsz
    pos = gstart[e] + rank
    n_tiles = -(-p // tm) + N_EXPERTS
    pair_of = jnp.full((n_tiles * tm,), -1, jnp.int32).at[pos].set(jnp.arange(p, dtype=jnp.int32))
    tile_start = jnp.arange(n_tiles, dtype=jnp.int32) * tm
    tile_e = jnp.minimum(jnp.searchsorted(gend, tile_start, side="right").astype(jnp.int32), N_EXPERTS - 1)
    tile_valid = jnp.clip(gstart[tile_e] + cnt[tile_e] - tile_start, 0, tm).astype(jnp.int32)
    used = gend[-1] // tm
    tile_e = jnp.where(tile_start < gend[-1], tile_e, tile_e[jnp.maximum(used - 1, 0)])
    return pair_of, tile_e, tile_valid


def _moe_kernel(te_ref, tv_ref, src_ref, nsrc_ref, pdst_ref, h_hbm, wg_ref, wu_ref, wd_ref, pairs_hbm,
                xg_ref, xb_ref, acc_ref, ys_ref, gsem, ssem, *, tm, n_f, dump_row):
    i, f = pl.program_id(0), pl.program_id(1)
    n_i = pl.num_programs(0)
    has_rows = tv_ref[i] > 0
    slot = i % 2
    rows_per_step = tm // n_f

    def gather_copy(s, r, row):
        return pltpu.make_async_copy(h_hbm.at[pl.ds(row, 1)], xg_ref.at[s, pl.ds(r, 1)], gsem.at[s])

    def scatter_copy(r, row):
        return pltpu.make_async_copy(ys_ref.at[pl.ds(r, 1)], pairs_hbm.at[pl.ds(row, 1)], ssem)

    def wait_gather(s):
        def body(r, carry):
            gather_copy(s, r, 0).wait()
            return carry
        lax.fori_loop(0, tm, body, 0, unroll=8)

    def wait_scatter():
        def body(r, carry):
            scatter_copy(r, 0).wait()
            return carry
        lax.fori_loop(0, tm, body, 0, unroll=8)

    def start_step_dmas():
        base = f * rows_per_step
        for j in range(rows_per_step):
            r = base + j
            gather_copy(1 - slot, r, nsrc_ref[0, r]).start()
            scatter_copy(r, jnp.where(i == 0, dump_row + r, pdst_ref[0, r])).start()

    @pl.when((i == 0) & (f == 0))
    def _():
        ys_ref[...] = jnp.zeros_like(ys_ref)

        def body(r, carry):
            gather_copy(0, r, src_ref[0, r]).start()
            return carry
        lax.fori_loop(0, tm, body, 0, unroll=8)

    @pl.when(f == 0)
    def _():
        wait_gather(slot)

    @pl.when(has_rows)
    def _():
        @pl.when(f == 0)
        def _():
            xb_ref[...] = xg_ref[slot].astype(BF16)
            acc_ref[...] = jnp.zeros_like(acc_ref)

        start_step_dmas()
        x = xb_ref[...]
        g = jnp.dot(x, wg_ref[...].astype(BF16), preferred_element_type=F32)
        u = jnp.dot(x, wu_ref[...].astype(BF16), preferred_element_type=F32)
        a = (_silu(g) * u).astype(BF16)
        acc_ref[...] += jnp.dot(a, wd_ref[...].astype(BF16), preferred_element_type=F32)

    @pl.when(jnp.logical_not(has_rows))
    def _():
        start_step_dmas()

    @pl.when(f == n_f - 1)
    def _():
        wait_scatter()

        @pl.when(has_rows)
        def _():
            ys_ref[...] = acc_ref[...]

        @pl.when(i == n_i - 1)
        def _():
            wait_gather(1 - slot)


def moe_experts(h, wg, wu, wd, plan, t, src_of_token, *, tm, tf):
    pair_of, tile_e, tile_valid = plan
    n_tiles = tile_e.shape[0]
    _, d, ff = wg.shape
    nf = ff // tf
    assert tm % nf == 0
    real = pair_of >= 0
    pr = jnp.maximum(pair_of, 0)
    dump = 2 * t + jnp.arange(n_tiles * tm, dtype=jnp.int32) % tm
    row_src = jnp.where(real, src_of_token(pr // 2), 0).reshape(n_tiles, 1, tm)
    row_dst = jnp.where(real, (pr % 2) * t + pr // 2, dump).reshape(n_tiles, 1, tm)
    fe = lambda i, f, te, tv: jnp.where(tv[i] > 0, f, nf - 1)
    rows_at = lambda off: pl.BlockSpec((None, 1, tm), lambda i, f, te, tv: (jnp.clip(i + off, 0, n_tiles - 1), 0, 0),
                                       memory_space=pltpu.SMEM)
    grid_spec = pltpu.PrefetchScalarGridSpec(
        num_scalar_prefetch=2,
        grid=(n_tiles, nf),
        in_specs=[
            rows_at(0), rows_at(1), rows_at(-1),
            pl.BlockSpec(memory_space=pl.ANY),
            pl.BlockSpec((None, d, tf), lambda i, f, te, tv: (te[i], 0, fe(i, f, te, tv))),
            pl.BlockSpec((None, d, tf), lambda i, f, te, tv: (te[i], 0, fe(i, f, te, tv))),
            pl.BlockSpec((None, tf, d), lambda i, f, te, tv: (te[i], fe(i, f, te, tv), 0)),
        ],
        out_specs=pl.BlockSpec(memory_space=pl.ANY),
        scratch_shapes=[pltpu.VMEM((2, tm, d), F32), pltpu.VMEM((tm, d), BF16), pltpu.VMEM((tm, d), F32),
                        pltpu.VMEM((tm, d), F32), pltpu.SemaphoreType.DMA((2,)), pltpu.SemaphoreType.DMA(())],
    )
    return pl.pallas_call(
        functools.partial(_moe_kernel, tm=tm, n_f=nf, dump_row=2 * t),
        out_shape=jax.ShapeDtypeStruct((2 * t + tm, d), F32),
        grid_spec=grid_spec,
        compiler_params=pltpu.CompilerParams(dimension_semantics=("arbitrary", "arbitrary"),
                                             vmem_limit_bytes=VMEM_LIMIT, disable_bounds_checks=True),
        name="moe_experts",
    )(tile_e, tile_valid, row_src, row_src, row_dst, h, wg, wu, wd)


def _combine_final_kernel(p0_ref, p1_ref, rec_ref, x_ref, g_ref, fg_ref, o_ref):
    rec = rec_ref[...]
    y = rec[:, 2:3] * p0_ref[...] + rec[:, 3:4] * p1_ref[...]
    o_ref[...] = _rms(x_ref[...] + g_ref[...] * y, fg_ref[...])


def _rope_tables(ang):
    c, s = jnp.cos(ang), jnp.sin(ang)
    return jnp.concatenate([c, c], axis=-1), jnp.concatenate([-s, s], axis=-1)


def _axial_angles(seq):
    rows = seq // GRID_W
    row = jnp.repeat(jnp.arange(rows), GRID_W).astype(F32)
    col = jnp.tile(jnp.arange(GRID_W), rows).astype(F32)
    n_freq = HEAD_DIM // 4
    inv = ROPE_THETA ** (-jnp.arange(n_freq, dtype=F32) / n_freq)
    return jnp.concatenate([row[:, None] * inv, col[:, None] * inv], axis=-1)


def _linear_angles(pos, dim):
    n_freq = dim // 2
    inv = ROPE_THETA ** (-jnp.arange(n_freq, dtype=F32) / n_freq)
    return pos.astype(F32)[:, None] * inv


def kernel(x, c, ctx, c_ctx, w_ada, b_ada, norm1_g, norm2_g, w_in, q_norm_g, k_norm_g, ret_decay_f, ret_decay_b, w_att_o, w_ret_o, w_out, ffn_w_gate, ffn_w_up, ffn_w_down, moe_router, moe_w_gate, moe_w_up, moe_w_down, final_g):
    b, seq, d = x.shape
    ctx_len = ctx.shape[1]
    depth = w_ada.shape[0]
    assert ctx_len == Q_TILE and seq % Q_TILE == 0 and seq % GRID_W == 0
    nt = seq + ctx_len
    m = b * nt
    tpb_big, tpb_ffn = 4, 8
    tm_big, tm_ffn = nt // tpb_big, nt // tpb_ffn
    tm_norm, tpb_norm = tm_big, tpb_big
    ctx_row = b
    nchunks = nt // RET_CHUNK
    nc_lat = seq // RET_CHUNK

    cos_a, sin_a = _rope_tables(_axial_angles(seq))
    cos_a = jnp.concatenate([cos_a, jnp.ones((ctx_len, HEAD_DIM), F32)], axis=0)
    sin_a = jnp.concatenate([sin_a, jnp.zeros((ctx_len, HEAD_DIM), F32)], axis=0)
    pos = jnp.concatenate([ctx_len + jnp.arange(seq), jnp.arange(ctx_len)])
    cos_r, sin_r = _rope_tables(_linear_angles(pos, RET_DK))

    n_rows = -(-(b + 1) // 16) * 16
    cc = jnp.zeros((n_rows, d), F32).at[:b].set(c).at[b].set(c_ctx)
    mods = adaln_all(cc, w_ada, b_ada)
    mods = mods.reshape(depth, n_rows, N_MOD, d).transpose(0, 2, 1, 3).reshape(depth, N_MOD, n_rows, 1, d)

    xa = jnp.concatenate([x, ctx], axis=1).reshape(m, d)

    def norm_mod(xcur, g, sc, sh):
        sb = pl.BlockSpec((None, 1, d), lambda i: (i // tpb_norm, 0, 0))
        scx = pl.BlockSpec((None, 1, d), lambda i: (ctx_row, 0, 0))
        row = pl.BlockSpec((tm_norm, d), lambda i: (i, 0))
        return pl.pallas_call(
            functools.partial(_norm_mod_kernel, tm=tm_norm, tpb=tpb_norm, seq=seq),
            out_shape=jax.ShapeDtypeStruct((m, d), BF16),
            grid=(m // tm_norm,),
            in_specs=[row, pl.BlockSpec((1, d), lambda i: (0, 0)), sb, scx, sb, scx],
            out_specs=row,
            compiler_params=_cparams("parallel"),
            name="norm_mod",
        )(xcur, g.reshape(1, d), sc, sc, sh, sh)

    tok_tab = lambda t: (t, (tm_big, HEAD_DIM), lambda i, j: (i % tpb_big, 0))
    row_vec = lambda v: (v.reshape(1, -1), (1, v.shape[-1]), lambda i, j: (0, 0))
    rope_scaled = functools.partial(_ep_rope, scale=RET_DK ** -0.5)
    rope_plain = functools.partial(_ep_rope, scale=1.0)
    fwd_chunk = lambda t: (t + nc_lat) % nchunks
    bwd_chunk = lambda t: nchunks - 1 - t

    w_in_b = w_in.astype(BF16)
    for layer in range(depth):
        last = layer == depth - 1
        sh1, sc1, g1, sh2, sc2, g2 = [mods[layer, n] for n in range(N_MOD)]
        tn_in = 512 if d % 512 == 0 else 256

        h = norm_mod(xa, norm1_g[layer], sc1, sh1)
        rc = 4 if tm_big % 64 == 0 else 1
        pre = ATT_WIDTH + 2 * KV_WIDTH + 2 * RET_QK_WIDTH
        tabs = [row_vec(q_norm_g[layer]), row_vec(k_norm_g[layer]), tok_tab(cos_a), tok_tab(sin_a), tok_tab(cos_r), tok_tab(sin_r)]
        pb = proj(h, w_in_b, layer, 0,
                  [(ATT_WIDTH, _ep_head_norm_rope, (0, 2, 3)), (KV_WIDTH, _ep_head_norm_rope, (1, 2, 3)), (KV_WIDTH, _ep_plain, ()),
                   (RET_QK_WIDTH, rope_plain, (4, 5)), (RET_QK_WIDTH, rope_scaled, (4, 5)), (RET_V_WIDTH, _ep_plain, ())],
                  BF16, tabs, lambda j: jnp.where(j < pre // tn_in, j + RET_V_WIDTH // tn_in, j - pre // tn_in),
                  tm=tm_big, tn=tn_in, row_chunks=rc, name="proj_bf16")
        o_rv, o_q = 0, RET_V_WIDTH
        o_k = o_q + ATT_WIDTH
        o_v = o_k + KV_WIDTH
        o_rq = o_v + KV_WIDTH
        o_rk = o_rq + RET_QK_WIDTH
        pf = proj(h, w_in_b, layer, pre + RET_V_WIDTH, [(2 * RET_V_WIDTH, _ep_plain, ()), (2 * d, _ep_sigmoid, ())],
                  F32, [], lambda j: j, tm=tm_big, tn=tn_in, row_chunks=rc, name="proj_f32")

        pb3 = pb.reshape(b, nt, pb.shape[-1])
        pf3 = pf.reshape(b, nt, pf.shape[-1])
        att = attention(pb3, o_q, o_k, o_v, seq=seq, with_ctx=not last).reshape(m, ATT_WIDTH)
        tf_ = retention_tables(ret_decay_f[layer], backward=False)
        tb_ = retention_tables(ret_decay_b[layer], backward=True)
        yf = retention_dir(pb3, o_rq, o_rk, o_rv, pf3, 0, tf_, fwd_chunk, None, F32, name="retention_fwd")
        ret = retention_dir(pb3, o_rq, o_rk, o_rv, pf3, 1, tb_, bwd_chunk, yf, BF16, name="retention_bwd").reshape(m, RET_V_WIDTH)

        z = branch_merge(att, ret, w_att_o[layer].astype(BF16), w_ret_o[layer].astype(BF16), pf, 2 * RET_V_WIDTH, tm=tm_big, tn=tn_in)
        xa = out_proj_residual(z, w_out[layer].astype(BF16), xa, g1, tm=tm_big, tpb=tpb_big, seq=seq, ctx_row=ctx_row, tn=tn_in)

        j = layer // 2
        if layer % 2 == 0:
            h2 = norm_mod(xa, norm2_g[layer], sc2, sh2)
            xa = ffn_residual(h2, ffn_w_gate[j].astype(BF16), ffn_w_up[j].astype(BF16), ffn_w_down[j].astype(BF16),
                              xa, g2, tm=tm_ffn, tpb=tpb_ffn, seq=seq, ctx_row=ctx_row)
        else:
            assert last, "routed-expert layers before the last layer are not supported"
            wr = jnp.zeros((d, LANES), F32).at[:, :N_EXPERTS].set(moe_router[j])
            sb = pl.BlockSpec((None, 1, d), lambda i: (i // tpb_norm, 0, 0))
            scx = pl.BlockSpec((None, 1, d), lambda i: (ctx_row, 0, 0))
            row = pl.BlockSpec((tm_norm, d), lambda i: (i, 0))
            h2, rec = pl.pallas_call(
                functools.partial(_norm_mod_router_kernel, tm=tm_norm, tpb=tpb_norm, seq=seq),
                out_shape=(jax.ShapeDtypeStruct((m, d), F32), jax.ShapeDtypeStruct((m, LANES), F32)),
                grid=(m // tm_norm,),
                in_specs=[row, pl.BlockSpec((1, d), lambda i: (0, 0)), sb, scx, sb, scx, pl.BlockSpec((d, LANES), lambda i: (0, 0))],
                out_specs=(row, pl.BlockSpec((tm_norm, LANES), lambda i: (i, 0))),
                compiler_params=_cparams("parallel"),
                name="norm_mod_router",
            )(xa, norm2_g[layer].reshape(1, d), sc2, sc2, sh2, sh2, wr)
            e12 = rec.reshape(b, nt, LANES)[:, :seq, :2].reshape(b * seq, 2).astype(jnp.int32)
            t_tok = b * seq
            eff = moe_w_gate.shape[-1]
            tf_moe = 512 if eff % 512 == 0 else eff
            row_step = int(np.lcm(eff // tf_moe, 16))
            tm_moe = max(784 // row_step, 1) * row_step
            plan = moe_plan(e12[:, 0], e12[:, 1], tm_moe)
            pairs = moe_experts(h2, moe_w_gate[j], moe_w_up[j], moe_w_down[j],
                                plan, t_tok, lambda tok: (tok // seq) * nt + tok % seq, tm=tm_moe, tf=tf_moe)
            tmc = Q_TILE
            spb = seq // tmc
            return pl.pallas_call(
                _combine_final_kernel,
                out_shape=jax.ShapeDtypeStruct((b, seq, d), F32),
                grid=(b, spb),
                in_specs=[
                    pl.BlockSpec((tmc, d), lambda bi, i: (bi * spb + i, 0)),
                    pl.BlockSpec((tmc, d), lambda bi, i: (t_tok // tmc + bi * spb + i, 0)),
                    pl.BlockSpec((tmc, LANES), lambda bi, i: (bi * (nt // tmc) + i, 0)),
                    pl.BlockSpec((None, tmc, d), lambda bi, i: (bi, i, 0)),
                    pl.BlockSpec((None, 1, d), lambda bi, i: (bi, 0, 0)),
                    pl.BlockSpec((1, d), lambda bi, i: (0, 0)),
                ],
                out_specs=pl.BlockSpec((None, tmc, d), lambda bi, i: (bi, i, 0)),
                compiler_params=_cparams("parallel", "parallel"),
                name="combine_final",
            )(pairs, pairs, rec, xa.reshape(b, nt, d), g2, final_g.reshape(1, d))

    tm_f = Q_TILE
    return pl.pallas_call(
        _final_norm_kernel,
        out_shape=jax.ShapeDtypeStruct((b, seq, d), F32),
        grid=(b, seq // tm_f),
        in_specs=[pl.BlockSpec((None, tm_f, d), lambda bi, i: (bi, i, 0)), pl.BlockSpec((1, d), lambda bi, i: (0, 0))],
        out_specs=pl.BlockSpec((None, tm_f, d), lambda bi, i: (bi, i, 0)),
        compiler_params=_cparams("parallel", "parallel"),
        name="final_norm",
    )(xa.reshape(b, nt, d), final_g.reshape(1, d))
```

```python
import functools

import jax
import jax.numpy as jnp
import numpy as np
from jax import lax
from jax.experimental import pallas as pl
from jax.experimental.pallas import tpu as pltpu

F32 = jnp.float32
BF16 = jnp.bfloat16

GRID_W = 64
N_HEADS = 16
N_KV_HEADS = 4
HEAD_DIM = 128
KV_GROUP = N_HEADS // N_KV_HEADS
ATT_WIDTH = N_HEADS * HEAD_DIM
KV_WIDTH = N_KV_HEADS * HEAD_DIM
ROPE_THETA = 10000.0
RET_HEADS = 8
RET_DK = 128
RET_DV = 256
RET_QK_WIDTH = RET_HEADS * RET_DK
RET_V_WIDTH = RET_HEADS * RET_DV
RET_CHUNK = 128
N_EXPERTS = 8
N_MOD = 6
EPS = 1e-6
GN_EPS = 1e-5

Q_TILE = 256
ATTN_ROW_CHUNKS = 2
LANES = 128
VMEM_LIMIT = 56 * 2**20


def _cparams(*sem):
    return pltpu.CompilerParams(dimension_semantics=sem, vmem_limit_bytes=VMEM_LIMIT)


def _silu(v):
    return v * jax.nn.sigmoid(v)


def _adaln_kernel(c_ref, w_ref, b_ref, o_ref):
    s = _silu(c_ref[...]).astype(BF16)
    o_ref[...] = jnp.dot(s, w_ref[...].astype(BF16), preferred_element_type=F32) + b_ref[...]


def adaln_all(cc, w_ada, b_ada):
    depth, d, n = w_ada.shape
    r = cc.shape[0]
    tn = 1024 if n % 1024 == 0 else n
    return pl.pallas_call(
        _adaln_kernel,
        out_shape=jax.ShapeDtypeStruct((depth, r, n), F32),
        grid=(depth, n // tn),
        in_specs=[
            pl.BlockSpec((r, d), lambda l, j: (0, 0)),
            pl.BlockSpec((None, d, tn), lambda l, j: (l, 0, j)),
            pl.BlockSpec((None, 1, tn), lambda l, j: (l, 0, j)),
        ],
        out_specs=pl.BlockSpec((None, r, tn), lambda l, j: (l, 0, j)),
        compiler_params=_cparams("parallel", "parallel"),
        name="adaln",
    )(cc, w_ada, b_ada.reshape(depth, 1, n))


def _ctx_rows(i, tm, tiles_per_batch, seq):
    rows = lax.broadcasted_iota(jnp.int32, (tm, 1), 0) + (i % tiles_per_batch) * tm
    return rows >= seq


def _rms(x, g):
    return x * lax.rsqrt(jnp.mean(x * x, axis=-1, keepdims=True) + EPS) * g


def _norm_mod_kernel(x_ref, g_ref, scb_ref, scc_ref, shb_ref, shc_ref, o_ref, *, tm, tpb, seq):
    i = pl.program_id(0)
    is_ctx = _ctx_rows(i, tm, tpb, seq)
    y = _rms(x_ref[...], g_ref[...])
    scale = jnp.where(is_ctx, scc_ref[...], scb_ref[...])
    shift = jnp.where(is_ctx, shc_ref[...], shb_ref[...])
    o_ref[...] = (y * (1 + scale) + shift).astype(o_ref.dtype)


def _route(t, wr):
    logits = jnp.dot(t, wr, preferred_element_type=F32, precision=lax.Precision.HIGHEST)
    lane = lax.broadcasted_iota(jnp.int32, logits.shape, 1)
    neg = jnp.float32(-jnp.inf)
    logits = jnp.where(lane < N_EXPERTS, logits, neg)
    m1 = jnp.max(logits, axis=-1, keepdims=True)
    i1 = jnp.min(jnp.where(logits == m1, lane, LANES), axis=-1, keepdims=True)
    rest = jnp.where(lane == i1, neg, logits)
    m2 = jnp.max(rest, axis=-1, keepdims=True)
    i2 = jnp.min(jnp.where(rest == m2, lane, LANES), axis=-1, keepdims=True)
    e2 = jnp.exp(m2 - m1)
    w1 = 1.0 / (1.0 + e2)
    w2 = e2 / (1.0 + e2)
    rec = jnp.where(lane == 0, i1.astype(F32), jnp.where(lane == 1, i2.astype(F32), jnp.where(lane == 2, w1, w2)))
    return jnp.where(lane < 4, rec, 0.0)


def _final_norm_kernel(x_ref, g_ref, o_ref):
    o_ref[...] = _rms(x_ref[...], g_ref[...])


def _mm_kernel(x_ref, w_ref, *rest, segments, row_chunks):
    o_ref = rest[-1]
    j = pl.program_id(1)
    tm = x_ref.shape[0]
    rows = tm // row_chunks
    for j_lo, j_hi, epilogue, extra_idx in segments:
        @pl.when((j >= j_lo) & (j < j_hi))
        def _():
            for c in range(row_chunks):
                rs = slice(c * rows, (c + 1) * rows)
                acc = jnp.dot(x_ref[rs, :], w_ref[...], preferred_element_type=F32)
                extras = [rest[e][rs, :] if rest[e].shape[0] == tm else rest[e][...] for e in extra_idx]
                o_ref[rs, :] = epilogue(acc, *extras).astype(o_ref.dtype)


def _rope(y, cos, sin):
    return y * cos + pltpu.roll(y, HEAD_DIM // 2, 1) * sin


def _ep_plain(acc):
    return acc


def _ep_sigmoid(acc):
    return jax.nn.sigmoid(acc)


def _ep_head_norm_rope(acc, g, cos, sin):
    outs = []
    for h in range(acc.shape[1] // HEAD_DIM):
        y = _rms(acc[:, h * HEAD_DIM:(h + 1) * HEAD_DIM], g)
        outs.append(_rope(y, cos, sin))
    return jnp.concatenate(outs, axis=1)


def _ep_rope(acc, cos, sin, *, scale):
    outs = []
    for h in range(acc.shape[1] // RET_DK):
        y = _rope(acc[:, h * RET_DK:(h + 1) * RET_DK], cos, sin)
        outs.append(y * scale if scale != 1.0 else y)
    return jnp.concatenate(outs, axis=1)


def proj(h, w, layer, col0, segments, out_dtype, extras, out_block, *, tm, tn, row_chunks, name):
    m, k = h.shape
    assert col0 % tn == 0 and m % tm == 0 and tm % (16 * row_chunks) == 0
    jb = col0 // tn
    segs, j0 = [], 0
    for width, epilogue, extra_idx in segments:
        assert width % tn == 0
        segs.append((j0, j0 + width // tn, epilogue, tuple(extra_idx)))
        j0 += width // tn
    in_specs = [
        pl.BlockSpec((tm, k), lambda i, j: (i, 0)),
        pl.BlockSpec((None, k, tn), lambda i, j: (layer, 0, jb + j)),
    ]
    args = [h, w]
    for arr, bs, im in extras:
        in_specs.append(pl.BlockSpec(bs, im))
        args.append(arr)
    return pl.pallas_call(
        functools.partial(_mm_kernel, segments=tuple(segs), row_chunks=row_chunks),
        out_shape=jax.ShapeDtypeStruct((m, j0 * tn), out_dtype),
        grid=(m // tm, j0),
        in_specs=in_specs,
        out_specs=pl.BlockSpec((tm, tn), lambda i, j: (i, out_block(j))),
        compiler_params=_cparams("parallel", "parallel"),
        name=name,
    )(*args)


def _attn_kernel(q_ref, k_ref, v_ref, o_ref, v1_ref, *, seq, n_lat_tiles, with_ctx):
    qi = pl.program_id(2)
    c = HEAD_DIM ** -0.5 * np.log2(np.e)

    @pl.when(qi == 0)
    def _():
        v1_ref[:, :HEAD_DIM] = v_ref[...]
        v1_ref[:, HEAD_DIM:] = jnp.ones(v_ref.shape, v1_ref.dtype)

    def run(k, v1):
        rows = q_ref.shape[0] // ATTN_ROW_CHUNKS
        for g in range(KV_GROUP):
            for r in range(ATTN_ROW_CHUNKS):
                rs = slice(r * rows, (r + 1) * rows)
                q = q_ref[rs, g * HEAD_DIM:(g + 1) * HEAD_DIM]
                s = lax.dot_general(q, k, (((1,), (1,)), ((), ())), preferred_element_type=F32)
                m = jnp.max(s, axis=-1, keepdims=True)
                p = jnp.exp2((s - m) * c).astype(BF16)
                o = jnp.dot(p, v1, preferred_element_type=F32)
                o_ref[rs, g * HEAD_DIM:(g + 1) * HEAD_DIM] = (o[:, :HEAD_DIM] / o[:, HEAD_DIM:]).astype(o_ref.dtype)

    @pl.when(qi < n_lat_tiles)
    def _():
        run(k_ref[...], v1_ref[...])

    @pl.when(qi >= n_lat_tiles)
    def _():
        if with_ctx:
            run(k_ref[seq:, :], v1_ref[seq:, :])
        else:
            o_ref[...] = jnp.zeros_like(o_ref)


def attention(qkv, q_col, k_col, v_col, *, seq, with_ctx):
    b, nt, _ = qkv.shape
    n_lat = seq // Q_TILE
    n_tiles = nt // Q_TILE
    gw = KV_GROUP * HEAD_DIM
    return pl.pallas_call(
        functools.partial(_attn_kernel, seq=seq, n_lat_tiles=n_lat, with_ctx=with_ctx),
        out_shape=jax.ShapeDtypeStruct((b, nt, ATT_WIDTH), BF16),
        grid=(b, N_KV_HEADS, n_tiles),
        in_specs=[
            pl.BlockSpec((None, Q_TILE, gw), lambda bi, h, i: (bi, i, q_col // gw + h)),
            pl.BlockSpec((None, nt, HEAD_DIM), lambda bi, h, i: (bi, 0, k_col // HEAD_DIM + h)),
            pl.BlockSpec((None, nt, HEAD_DIM), lambda bi, h, i: (bi, 0, v_col // HEAD_DIM + h)),
        ],
        out_specs=pl.BlockSpec((None, Q_TILE, gw), lambda bi, h, i: (bi, i, h)),
        scratch_shapes=[pltpu.VMEM((nt, 2 * HEAD_DIM), BF16)],
        compiler_params=_cparams("parallel", "parallel", "arbitrary"),
        name="attention",
    )(qkv, qkv, qkv)


def _retention_kernel(*refs, has_prev):
    if has_prev:
        q_ref, k_ref, v_ref, g_ref, dec_ref, xi_ref, zeta_ref, cd_ref, prev_ref, o_ref, r_ref = refs
    else:
        q_ref, k_ref, v_ref, g_ref, dec_ref, xi_ref, zeta_ref, cd_ref, o_ref, r_ref = refs

    @pl.when(pl.program_id(1) == 0)
    def _():
        r_ref[...] = jnp.zeros_like(r_ref)

    for bi in range(q_ref.shape[0]):
        for h in range(RET_HEADS):
            ks = slice(h * RET_DK, (h + 1) * RET_DK)
            vs = slice(h * RET_DV, (h + 1) * RET_DV)
            q = q_ref[bi, :, ks]
            k = k_ref[bi, :, ks]
            v = v_ref[bi, :, vs]
            state = r_ref[bi, h]
            inner = lax.dot_general(q, k, (((1,), (1,)), ((), ())), preferred_element_type=F32) * dec_ref[h]
            o = jnp.dot(inner.astype(BF16), v, preferred_element_type=F32)
            o = o + jnp.dot(q, state.astype(BF16), preferred_element_type=F32) * xi_ref[h]
            kz = (k.astype(F32) * zeta_ref[h]).astype(BF16)
            r_ref[bi, h] = state * cd_ref[h] + lax.dot_general(kz, v, (((0,), (0,)), ((), ())), preferred_element_type=F32)
            mu = jnp.mean(o, axis=-1, keepdims=True)
            d = o - mu
            var = jnp.mean(d * d, axis=-1, keepdims=True)
            y = _silu(g_ref[bi, :, vs]) * (d * lax.rsqrt(var + GN_EPS))
            if has_prev:
                y = prev_ref[bi, :, vs] + y
            o_ref[bi, :, vs] = y.astype(o_ref.dtype)


def retention_dir(qkv, q_col, k_col, v_col, rg, gate_col, tables, chunk_of, prev, out_dtype, *, name):
    b, nt, _ = qkv.shape
    nc = nt // RET_CHUNK
    L = RET_CHUNK
    bb = next(n for n in (4, 2, 1) if b % n == 0)
    tok = lambda w, col=0: pl.BlockSpec((bb, L, w), lambda bi, t: (bi, chunk_of(t), col // w))
    full = lambda a: pl.BlockSpec(a.shape, lambda bi, t: (0,) * a.ndim)
    assert q_col % RET_QK_WIDTH == 0 and k_col % RET_QK_WIDTH == 0 and v_col % RET_V_WIDTH == 0
    in_specs = [tok(RET_QK_WIDTH, q_col), tok(RET_QK_WIDTH, k_col), tok(RET_V_WIDTH, v_col),
                tok(RET_V_WIDTH, gate_col * RET_V_WIDTH)]
    in_specs += [full(a) for a in tables]
    args = [qkv, qkv, qkv, rg, *tables]
    if prev is not None:
        in_specs.append(tok(RET_V_WIDTH))
        args.append(prev)
    return pl.pallas_call(
        functools.partial(_retention_kernel, has_prev=prev is not None),
        out_shape=jax.ShapeDtypeStruct((b, nt, RET_V_WIDTH), out_dtype),
        grid=(b // bb, nc),
        in_specs=in_specs,
        out_specs=tok(RET_V_WIDTH),
        scratch_shapes=[pltpu.VMEM((bb, RET_HEADS, RET_DK, RET_DV), F32)],
        compiler_params=_cparams("parallel", "arbitrary"),
        name=name,
    )(*args)


def retention_tables(decay_logit, backward):
    L = RET_CHUNK
    lg = jax.nn.log_sigmoid(decay_logit.astype(F32))
    pos = jnp.arange(L, dtype=F32)
    diff = pos[:, None] - pos[None, :]
    if backward:
        diff = -diff
        xi_e = L - pos
        zeta_e = pos
    else:
        xi_e = pos + 1.0
        zeta_e = L - 1.0 - pos
    decay = jnp.where(diff >= 0, jnp.exp(jnp.maximum(diff, 0.0) * lg[:, None, None]), 0.0)
    xi = jnp.broadcast_to(jnp.exp(xi_e * lg[:, None])[:, :, None], (RET_HEADS, L, RET_DV))
    zeta = jnp.broadcast_to(jnp.exp(zeta_e * lg[:, None])[:, :, None], (RET_HEADS, L, RET_DK))
    cd = jnp.broadcast_to(jnp.exp(L * lg)[:, None, None], (RET_HEADS, 1, RET_DV))
    return decay, xi, zeta, cd


def _merge_kernel(a_ref, r_ref, wa_ref, wr_ref, ga_ref, gr_ref, o_ref):
    za = jnp.dot(a_ref[...], wa_ref[...], preferred_element_type=F32)
    zr = jnp.dot(r_ref[...], wr_ref[...], preferred_element_type=F32)
    o_ref[...] = (ga_ref[...] * za + gr_ref[...] * zr).astype(o_ref.dtype)


def branch_merge(att, ret, w_att_o, w_ret_o, gates, gate_col, *, tm, tn=512):
    m, ka = att.shape
    kr = ret.shape[1]
    d = w_att_o.shape[1]
    nj = d // tn
    return pl.pallas_call(
        _merge_kernel,
        out_shape=jax.ShapeDtypeStruct((m, d), BF16),
        grid=(m // tm, nj),
        in_specs=[
            pl.BlockSpec((tm, ka), lambda i, j: (i, 0)),
            pl.BlockSpec((tm, kr), lambda i, j: (i, 0)),
            pl.BlockSpec((ka, tn), lambda i, j: (0, j)),
            pl.BlockSpec((kr, tn), lambda i, j: (0, j)),
            pl.BlockSpec((tm, tn), lambda i, j: (i, gate_col // tn + j)),
            pl.BlockSpec((tm, tn), lambda i, j: (i, gate_col // tn + nj + j)),
        ],
        out_specs=pl.BlockSpec((tm, tn), lambda i, j: (i, j)),
        compiler_params=_cparams("parallel", "parallel"),
        name="branch_merge",
    )(att, ret, w_att_o, w_ret_o, gates, gates)


def _out_norm_kernel(*refs, tm, tpb, seq, row_chunks, route):
    if route:
        (z_ref, w_ref, x_ref, gb_ref, gc_ref, ng_ref, scb_ref, scc_ref, shb_ref, shc_ref, wr_ref,
         xo_ref, h_ref, rec_ref) = refs
    else:
        z_ref, w_ref, x_ref, gb_ref, gc_ref, ng_ref, scb_ref, scc_ref, shb_ref, shc_ref, xo_ref, h_ref = refs
    i = pl.program_id(0)
    rows = tm // row_chunks
    for c in range(row_chunks):
        rs = slice(c * rows, (c + 1) * rows)
        is_ctx = (lax.broadcasted_iota(jnp.int32, (rows, 1), 0) + ((i % tpb) * tm + c * rows)) >= seq
        y = jnp.dot(z_ref[rs, :], w_ref[...], preferred_element_type=F32)
        xn = x_ref[rs, :] + jnp.where(is_ctx, gc_ref[...], gb_ref[...]) * y
        xo_ref[rs, :] = xn
        scale = jnp.where(is_ctx, scc_ref[...], scb_ref[...])
        shift = jnp.where(is_ctx, shc_ref[...], shb_ref[...])
        t = _rms(xn, ng_ref[...]) * (1 + scale) + shift
        h_ref[rs, :] = t.astype(h_ref.dtype)
        if route:
            rec_ref[rs, :] = _route(t, wr_ref[...])


def out_proj_norm(z, w_out, x, gate, norm_g, scale, shift, router_w, *, tm, tpb, seq, ctx_row):
    m, k = z.shape
    d = w_out.shape[1]
    route = router_w is not None
    row_chunks = 2
    assert tm % (16 * row_chunks) == 0
    rows = lambda w: pl.BlockSpec((tm, w), lambda i: (i, 0))
    vec_b = pl.BlockSpec((None, 1, d), lambda i: (i // tpb, 0, 0))
    vec_c = pl.BlockSpec((None, 1, d), lambda i: (ctx_row, 0, 0))
    const = lambda a: pl.BlockSpec(a.shape, lambda i: (0, 0), pipeline_mode=pl.Buffered(1))
    in_specs = [rows(k), const(w_out), rows(d), vec_b, vec_c, pl.BlockSpec((1, d), lambda i: (0, 0)), vec_b, vec_c, vec_b, vec_c]
    args = [z, w_out, x, gate, gate, norm_g.reshape(1, d), scale, scale, shift, shift]
    out_shape = [jax.ShapeDtypeStruct((m, d), F32), jax.ShapeDtypeStruct((m, d), F32 if route else BF16)]
    out_specs = [rows(d), rows(d)]
    if route:
        in_specs.append(const(router_w))
        args.append(router_w)
        out_shape.append(jax.ShapeDtypeStruct((m, LANES), F32))
        out_specs.append(rows(LANES))
    return pl.pallas_call(
        functools.partial(_out_norm_kernel, tm=tm, tpb=tpb, seq=seq, row_chunks=row_chunks, route=route),
        out_shape=tuple(out_shape),
        grid=(m // tm,),
        in_specs=in_specs,
        out_specs=tuple(out_specs),
        compiler_params=_cparams("parallel"),
        name="out_proj_norm",
    )(*args)


def _ffn_kernel(h_ref, wg_ref, wu_ref, wd_ref, x_ref, gb_ref, gc_ref, o_ref, acc_ref, *, tm, tpb, seq):
    i, f = pl.program_id(0), pl.program_id(1)

    @pl.when(f == 0)
    def _():
        acc_ref[...] = jnp.zeros_like(acc_ref)

    h = h_ref[...]
    g = jnp.dot(h, wg_ref[...], preferred_element_type=F32)
    u = jnp.dot(h, wu_ref[...], preferred_element_type=F32)
    a = (_silu(g) * u).astype(BF16)
    acc_ref[...] += jnp.dot(a, wd_ref[...], preferred_element_type=F32)

    @pl.when(f == pl.num_programs(1) - 1)
    def _():
        gate = jnp.where(_ctx_rows(i, tm, tpb, seq), gc_ref[...], gb_ref[...])
        o_ref[...] = x_ref[...] + gate * acc_ref[...]


def ffn_residual(h, wg, wu, wd, x, gate, *, tm, tpb, seq, ctx_row, tf=512):
    m, d = h.shape
    ff = wg.shape[1]
    return pl.pallas_call(
        functools.partial(_ffn_kernel, tm=tm, tpb=tpb, seq=seq),
        out_shape=jax.ShapeDtypeStruct((m, d), F32),
        grid=(m // tm, ff // tf),
        in_specs=[
            pl.BlockSpec((tm, d), lambda i, f: (i, 0)),
            pl.BlockSpec((d, tf), lambda i, f: (0, f)),
            pl.BlockSpec((d, tf), lambda i, f: (0, f)),
            pl.BlockSpec((tf, d), lambda i, f: (f, 0)),
            pl.BlockSpec((tm, d), lambda i, f: (i, 0)),
            pl.BlockSpec((None, 1, d), lambda i, f: (i // tpb, 0, 0)),
            pl.BlockSpec((None, 1, d), lambda i, f: (ctx_row, 0, 0)),
        ],
        out_specs=pl.BlockSpec((tm, d), lambda i, f: (i, 0)),
        scratch_shapes=[pltpu.VMEM((tm, d), F32)],
        compiler_params=_cparams("parallel", "arbitrary"),
        name="ffn",
    )(h, wg, wu, wd, x, gate, gate)


def moe_plan(e1, e2, tm):
    t = e1.shape[0]
    p = 2 * t
    e = jnp.stack([e1, e2], axis=1).reshape(p)
    onehot = (e[:, None] == jnp.arange(N_EXPERTS, dtype=jnp.int32)[None, :]).astype(jnp.int32)
    csum = jnp.cumsum(onehot, axis=0)
    rank = jnp.take_along_axis(csum, e[:, None], axis=1)[:, 0] - 1
    cnt = csum[-1]
    gsz = (cnt + tm - 1) // tm * tm
    gend = jnp.cumsum(gsz)
    gstart = gend - gsz
    pos = gstart[e] + rank
    n_tiles = -(-p // tm) + N_EXPERTS
    pair_of = jnp.full((n_tiles * tm,), -1, jnp.int32).at[pos].set(jnp.arange(p, dtype=jnp.int32))
    tile_start = jnp.arange(n_tiles, dtype=jnp.int32) * tm
    tile_e = jnp.minimum(jnp.searchsorted(gend, tile_start, side="right").astype(jnp.int32), N_EXPERTS - 1)
    tile_valid = jnp.clip(gstart[tile_e] + cnt[tile_e] - tile_start, 0, tm).astype(jnp.int32)
    used = gend[-1] // tm
    tile_e = jnp.where(tile_start < gend[-1], tile_e, tile_e[jnp.maximum(used - 1, 0)])
    return pair_of, tile_e, tile_valid


def _moe_kernel(te_ref, tv_ref, src_ref, nsrc_ref, pdst_ref, h_hbm, wg_ref, wu_ref, wd_ref, pairs_hbm,
                xg_ref, xb_ref, acc_ref, ys_ref, gsem, ssem, *, tm, n_f, dump_row):
    i, f = pl.program_id(0), pl.program_id(1)
    n_i = pl.num_programs(0)
    has_rows = tv_ref[i] > 0
    slot = i % 2
    rows_per_step = tm // n_f

    def gather_copy(s, r, row):
        return pltpu.make_async_copy(h_hbm.at[pl.ds(row, 1)], xg_ref.at[s, pl.ds(r, 1)], gsem.at[s])

    def scatter_copy(r, row):
        return pltpu.make_async_copy(ys_ref.at[pl.ds(r, 1)], pairs_hbm.at[pl.ds(row, 1)], ssem)

    def wait_gather(s):
        def body(r, carry):
            gather_copy(s, r, 0).wait()
            return carry
        lax.fori_loop(0, tm, body, 0, unroll=8)

    def wait_scatter():
        def body(r, carry):
            scatter_copy(r, 0).wait()
            return carry
        lax.fori_loop(0, tm, body, 0, unroll=8)

    def start_step_dmas():
        base = f * rows_per_step
        for j in range(rows_per_step):
            r = base + j
            gather_copy(1 - slot, r, nsrc_ref[0, r]).start()
            scatter_copy(r, jnp.where(i == 0, dump_row + r, pdst_ref[0, r])).start()

    @pl.when((i == 0) & (f == 0))
    def _():
        ys_ref[...] = jnp.zeros_like(ys_ref)

        def body(r, carry):
            gather_copy(0, r, src_ref[0, r]).start()
            return carry
        lax.fori_loop(0, tm, body, 0, unroll=8)

    @pl.when(f == 0)
    def _():
        wait_gather(slot)

    @pl.when(has_rows)
    def _():
        @pl.when(f == 0)
        def _():
            xb_ref[...] = xg_ref[slot].astype(BF16)
            acc_ref[...] = jnp.zeros_like(acc_ref)

        start_step_dmas()
        x = xb_ref[...]
        g = jnp.dot(x, wg_ref[...].astype(BF16), preferred_element_type=F32)
        u = jnp.dot(x, wu_ref[...].astype(BF16), preferred_element_type=F32)
        a = (_silu(g) * u).astype(BF16)
        acc_ref[...] += jnp.dot(a, wd_ref[...].astype(BF16), preferred_element_type=F32)

    @pl.when(jnp.logical_not(has_rows))
    def _():
        start_step_dmas()

    @pl.when(f == n_f - 1)
    def _():
        wait_scatter()

        @pl.when(has_rows)
        def _():
            ys_ref[...] = acc_ref[...]

        @pl.when(i == n_i - 1)
        def _():
            wait_gather(1 - slot)


def moe_experts(h, wg, wu, wd, plan, t, src_of_token, *, tm, tf):
    pair_of, tile_e, tile_valid = plan
    n_tiles = tile_e.shape[0]
    _, d, ff = wg.shape
    nf = ff // tf
    assert tm % nf == 0
    real = pair_of >= 0
    pr = jnp.maximum(pair_of, 0)
    dump = 2 * t + jnp.arange(n_tiles * tm, dtype=jnp.int32) % tm
    row_src = jnp.where(real, src_of_token(pr // 2), 0).reshape(n_tiles, 1, tm)
    row_dst = jnp.where(real, (pr % 2) * t + pr // 2, dump).reshape(n_tiles, 1, tm)
    fe = lambda i, f, te, tv: jnp.where(tv[i] > 0, f, nf - 1)
    rows_at = lambda off: pl.BlockSpec((None, 1, tm), lambda i, f, te, tv: (jnp.clip(i + off, 0, n_tiles - 1), 0, 0),
                                       memory_space=pltpu.SMEM)
    grid_spec = pltpu.PrefetchScalarGridSpec(
        num_scalar_prefetch=2,
        grid=(n_tiles, nf),
        in_specs=[
            rows_at(0), rows_at(1), rows_at(-1),
            pl.BlockSpec(memory_space=pl.ANY),
            pl.BlockSpec((None, d, tf), lambda i, f, te, tv: (te[i], 0, fe(i, f, te, tv))),
            pl.BlockSpec((None, d, tf), lambda i, f, te, tv: (te[i], 0, fe(i, f, te, tv))),
            pl.BlockSpec((None, tf, d), lambda i, f, te, tv: (te[i], fe(i, f, te, tv), 0)),
        ],
        out_specs=pl.BlockSpec(memory_space=pl.ANY),
        scratch_shapes=[pltpu.VMEM((2, tm, d), F32), pltpu.VMEM((tm, d), BF16), pltpu.VMEM((tm, d), F32),
                        pltpu.VMEM((tm, d), F32), pltpu.SemaphoreType.DMA((2,)), pltpu.SemaphoreType.DMA(())],
    )
    return pl.pallas_call(
        functools.partial(_moe_kernel, tm=tm, n_f=nf, dump_row=2 * t),
        out_shape=jax.ShapeDtypeStruct((2 * t + tm, d), F32),
        grid_spec=grid_spec,
        compiler_params=pltpu.CompilerParams(dimension_semantics=("arbitrary", "arbitrary"),
                                             vmem_limit_bytes=VMEM_LIMIT, disable_bounds_checks=True),
        name="moe_experts",
    )(tile_e, tile_valid, row_src, row_src, row_dst, h, wg, wu, wd)


def _combine_final_kernel(p0_ref, p1_ref, rec_ref, x_ref, g_ref, fg_ref, o_ref):
    rec = rec_ref[...]
    y = rec[:, 2:3] * p0_ref[...] + rec[:, 3:4] * p1_ref[...]
    o_ref[...] = _rms(x_ref[...] + g_ref[...] * y, fg_ref[...])


def _rope_tables(ang):
    c, s = jnp.cos(ang), jnp.sin(ang)
    return jnp.concatenate([c, c], axis=-1), jnp.concatenate([-s, s], axis=-1)


def _axial_angles(seq):
    rows = seq // GRID_W
    row = jnp.repeat(jnp.arange(rows), GRID_W).astype(F32)
    col = jnp.tile(jnp.arange(GRID_W), rows).astype(F32)
    n_freq = HEAD_DIM // 4
    inv = ROPE_THETA ** (-jnp.arange(n_freq, dtype=F32) / n_freq)
    return jnp.concatenate([row[:, None] * inv, col[:, None] * inv], axis=-1)


def _linear_angles(pos, dim):
    n_freq = dim // 2
    inv = ROPE_THETA ** (-jnp.arange(n_freq, dtype=F32) / n_freq)
    return pos.astype(F32)[:, None] * inv


def kernel(x, c, ctx, c_ctx, w_ada, b_ada, norm1_g, norm2_g, w_in, q_norm_g, k_norm_g, ret_decay_f, ret_decay_b, w_att_o, w_ret_o, w_out, ffn_w_gate, ffn_w_up, ffn_w_down, moe_router, moe_w_gate, moe_w_up, moe_w_down, final_g):
    b, seq, d = x.shape
    ctx_len = ctx.shape[1]
    depth = w_ada.shape[0]
    assert ctx_len == Q_TILE and seq % Q_TILE == 0 and seq % GRID_W == 0
    nt = seq + ctx_len
    m = b * nt
    tpb_big, tpb_ffn = 4, 8
    tm_big, tm_ffn = nt // tpb_big, nt // tpb_ffn
    tm_norm, tpb_norm = tm_big, tpb_big
    ctx_row = b
    nchunks = nt // RET_CHUNK
    nc_lat = seq // RET_CHUNK

    cos_a, sin_a = _rope_tables(_axial_angles(seq))
    cos_a = jnp.concatenate([cos_a, jnp.ones((ctx_len, HEAD_DIM), F32)], axis=0)
    sin_a = jnp.concatenate([sin_a, jnp.zeros((ctx_len, HEAD_DIM), F32)], axis=0)
    pos = jnp.concatenate([ctx_len + jnp.arange(seq), jnp.arange(ctx_len)])
    cos_r, sin_r = _rope_tables(_linear_angles(pos, RET_DK))

    n_rows = -(-(b + 1) // 16) * 16
    cc = jnp.zeros((n_rows, d), F32).at[:b].set(c).at[b].set(c_ctx)
    mods = adaln_all(cc, w_ada, b_ada)
    mods = mods.reshape(depth, n_rows, N_MOD, d).transpose(0, 2, 1, 3).reshape(depth, N_MOD, n_rows, 1, d)

    xa = jnp.concatenate([x, ctx], axis=1).reshape(m, d)

    def norm_mod(xcur, g, sc, sh):
        sb = pl.BlockSpec((None, 1, d), lambda i: (i // tpb_norm, 0, 0))
        scx = pl.BlockSpec((None, 1, d), lambda i: (ctx_row, 0, 0))
        row = pl.BlockSpec((tm_norm, d), lambda i: (i, 0))
        return pl.pallas_call(
            functools.partial(_norm_mod_kernel, tm=tm_norm, tpb=tpb_norm, seq=seq),
            out_shape=jax.ShapeDtypeStruct((m, d), BF16),
            grid=(m // tm_norm,),
            in_specs=[row, pl.BlockSpec((1, d), lambda i: (0, 0)), sb, scx, sb, scx],
            out_specs=row,
            compiler_params=_cparams("parallel"),
            name="norm_mod",
        )(xcur, g.reshape(1, d), sc, sc, sh, sh)

    tok_tab = lambda t: (t, (tm_big, HEAD_DIM), lambda i, j: (i % tpb_big, 0))
    row_vec = lambda v: (v.reshape(1, -1), (1, v.shape[-1]), lambda i, j: (0, 0))
    rope_scaled = functools.partial(_ep_rope, scale=RET_DK ** -0.5)
    rope_plain = functools.partial(_ep_rope, scale=1.0)
    fwd_chunk = lambda t: (t + nc_lat) % nchunks
    bwd_chunk = lambda t: nchunks - 1 - t

    w_in_b = w_in.astype(BF16)
    for layer in range(depth):
        last = layer == depth - 1
        sh1, sc1, g1, sh2, sc2, g2 = [mods[layer, n] for n in range(N_MOD)]
        tn_in = 512 if d % 512 == 0 else 256

        h = norm_mod(xa, norm1_g[layer], sc1, sh1)
        rc = 4 if tm_big % 64 == 0 else 1
        pre = ATT_WIDTH + 2 * KV_WIDTH + 2 * RET_QK_WIDTH
        tabs = [row_vec(q_norm_g[layer]), row_vec(k_norm_g[layer]), tok_tab(cos_a), tok_tab(sin_a), tok_tab(cos_r), tok_tab(sin_r)]
        pb = proj(h, w_in_b, layer, 0,
                  [(ATT_WIDTH, _ep_head_norm_rope, (0, 2, 3)), (KV_WIDTH, _ep_head_norm_rope, (1, 2, 3)), (KV_WIDTH, _ep_plain, ()),
                   (RET_QK_WIDTH, rope_plain, (4, 5)), (RET_QK_WIDTH, rope_scaled, (4, 5)), (RET_V_WIDTH, _ep_plain, ())],
                  BF16, tabs, lambda j: jnp.where(j < pre // tn_in, j + RET_V_WIDTH // tn_in, j - pre // tn_in),
                  tm=tm_big, tn=tn_in, row_chunks=rc, name="proj_bf16")
        o_rv, o_q = 0, RET_V_WIDTH
        o_k = o_q + ATT_WIDTH
        o_v = o_k + KV_WIDTH
        o_rq = o_v + KV_WIDTH
        o_rk = o_rq + RET_QK_WIDTH
        pf = proj(h, w_in_b, layer, pre + RET_V_WIDTH, [(2 * RET_V_WIDTH, _ep_plain, ()), (2 * d, _ep_sigmoid, ())],
                  F32, [], lambda j: j, tm=tm_big, tn=tn_in, row_chunks=rc, name="proj_f32")

        pb3 = pb.reshape(b, nt, pb.shape[-1])
        pf3 = pf.reshape(b, nt, pf.shape[-1])
        att = attention(pb3, o_q, o_k, o_v, seq=seq, with_ctx=not last).reshape(m, ATT_WIDTH)
        tf_ = retention_tables(ret_decay_f[layer], backward=False)
        tb_ = retention_tables(ret_decay_b[layer], backward=True)
        yf = retention_dir(pb3, o_rq, o_rk, o_rv, pf3, 0, tf_, fwd_chunk, None, F32, name="retention_fwd")
        ret = retention_dir(pb3, o_rq, o_rk, o_rv, pf3, 1, tb_, bwd_chunk, yf, BF16, name="retention_bwd").reshape(m, RET_V_WIDTH)

        z = branch_merge(att, ret, w_att_o[layer].astype(BF16), w_ret_o[layer].astype(BF16), pf, 2 * RET_V_WIDTH, tm=tm_big, tn=tn_in)
        j = layer // 2
        out_norm = functools.partial(out_proj_norm, z, w_out[layer].astype(BF16), xa, g1, norm2_g[layer], sc2, sh2,
                                     tm=tm_ffn, tpb=tpb_ffn, seq=seq, ctx_row=ctx_row)
        if layer % 2 == 0:
            xa, h2 = out_norm(None)
            xa = ffn_residual(h2, ffn_w_gate[j].astype(BF16), ffn_w_up[j].astype(BF16), ffn_w_down[j].astype(BF16),
                              xa, g2, tm=tm_ffn, tpb=tpb_ffn, seq=seq, ctx_row=ctx_row)
        else:
            assert last, "routed-expert layers before the last layer are not supported"
            wr = jnp.zeros((d, LANES), F32).at[:, :N_EXPERTS].set(moe_router[j])
            xa, h2, rec = out_norm(wr)
            e12 = rec.reshape(b, nt, LANES)[:, :seq, :2].reshape(b * seq, 2).astype(jnp.int32)
            t_tok = b * seq
            eff = moe_w_gate.shape[-1]
            tf_moe = 512 if eff % 512 == 0 else eff
            row_step = int(np.lcm(eff // tf_moe, 16))
            tm_moe = max(784 // row_step, 1) * row_step
            plan = moe_plan(e12[:, 0], e12[:, 1], tm_moe)
            pairs = moe_experts(h2, moe_w_gate[j], moe_w_up[j], moe_w_down[j],
                                plan, t_tok, lambda tok: (tok // seq) * nt + tok % seq, tm=tm_moe, tf=tf_moe)
            tmc = Q_TILE
            spb = seq // tmc
            return pl.pallas_call(
                _combine_final_kernel,
                out_shape=jax.ShapeDtypeStruct((b, seq, d), F32),
                grid=(b, spb),
                in_specs=[
                    pl.BlockSpec((tmc, d), lambda bi, i: (bi * spb + i, 0)),
                    pl.BlockSpec((tmc, d), lambda bi, i: (t_tok // tmc + bi * spb + i, 0)),
                    pl.BlockSpec((tmc, LANES), lambda bi, i: (bi * (nt // tmc) + i, 0)),
                    pl.BlockSpec((None, tmc, d), lambda bi, i: (bi, i, 0)),
                    pl.BlockSpec((None, 1, d), lambda bi, i: (bi, 0, 0)),
                    pl.BlockSpec((1, d), lambda bi, i: (0, 0)),
                ],
                out_specs=pl.BlockSpec((None, tmc, d), lambda bi, i: (bi, i, 0)),
                compiler_params=_cparams("parallel", "parallel"),
                name="combine_final",
            )(pairs, pairs, rec, xa.reshape(b, nt, d), g2, final_g.reshape(1, d))

    tm_f = Q_TILE
    return pl.pallas_call(
        _final_norm_kernel,
        out_shape=jax.ShapeDtypeStruct((b, seq, d), F32),
        grid=(b, seq // tm_f),
        in_specs=[pl.BlockSpec((None, tm_f, d), lambda bi, i: (bi, i, 0)), pl.BlockSpec((1, d), lambda bi, i: (0, 0))],
        out_specs=pl.BlockSpec((None, tm_f, d), lambda bi, i: (bi, i, 0)),
        compiler_params=_cparams("parallel", "parallel"),
        name="final_norm",
    )(xa.reshape(b, nt, d), final_g.reshape(1, d))
```

```python
import functools

import jax
import jax.numpy as jnp
import numpy as np
from jax import lax
from jax.experimental import pallas as pl
from jax.experimental.pallas import tpu as pltpu

F32 = jnp.float32
BF16 = jnp.bfloat16

GRID_W = 64
N_HEADS = 16
N_KV_HEADS = 4
HEAD_DIM = 128
KV_GROUP = N_HEADS // N_KV_HEADS
ATT_WIDTH = N_HEADS * HEAD_DIM
KV_WIDTH = N_KV_HEADS * HEAD_DIM
ROPE_THETA = 10000.0
RET_HEADS = 8
RET_DK = 128
RET_DV = 256
RET_QK_WIDTH = RET_HEADS * RET_DK
RET_V_WIDTH = RET_HEADS * RET_DV
RET_CHUNK = 128
N_EXPERTS = 8
N_MOD = 6
EPS = 1e-6
GN_EPS = 1e-5

Q_TILE = 256
ATTN_ROW_CHUNKS = 2
LANES = 128
VMEM_LIMIT = 56 * 2**20


def _cparams(*sem):
    return pltpu.CompilerParams(dimension_semantics=sem, vmem_limit_bytes=VMEM_LIMIT)


def _silu(v):
    return v * jax.nn.sigmoid(v)


def _adaln_kernel(c_ref, w_ref, b_ref, o_ref):
    s = _silu(c_ref[...]).astype(BF16)
    o_ref[...] = jnp.dot(s, w_ref[...].astype(BF16), preferred_element_type=F32) + b_ref[...]


def adaln_all(cc, w_ada, b_ada):
    depth, d, n = w_ada.shape
    r = cc.shape[0]
    tn = 1024 if n % 1024 == 0 else n
    return pl.pallas_call(
        _adaln_kernel,
        out_shape=jax.ShapeDtypeStruct((depth, r, n), F32),
        grid=(depth, n // tn),
        in_specs=[
            pl.BlockSpec((r, d), lambda l, j: (0, 0)),
            pl.BlockSpec((None, d, tn), lambda l, j: (l, 0, j)),
            pl.BlockSpec((None, 1, tn), lambda l, j: (l, 0, j)),
        ],
        out_specs=pl.BlockSpec((None, r, tn), lambda l, j: (l, 0, j)),
        compiler_params=_cparams("parallel", "parallel"),
        name="adaln",
    )(cc, w_ada, b_ada.reshape(depth, 1, n))


def _ctx_rows(i, tm, tiles_per_batch, seq):
    rows = lax.broadcasted_iota(jnp.int32, (tm, 1), 0) + (i % tiles_per_batch) * tm
    return rows >= seq


def _rms(x, g):
    return x * lax.rsqrt(jnp.mean(x * x, axis=-1, keepdims=True) + EPS) * g


def _norm_mod_kernel(x_ref, g_ref, scb_ref, scc_ref, shb_ref, shc_ref, o_ref, *, tm, tpb, seq):
    i = pl.program_id(0)
    is_ctx = _ctx_rows(i, tm, tpb, seq)
    y = _rms(x_ref[...], g_ref[...])
    scale = jnp.where(is_ctx, scc_ref[...], scb_ref[...])
    shift = jnp.where(is_ctx, shc_ref[...], shb_ref[...])
    o_ref[...] = (y * (1 + scale) + shift).astype(o_ref.dtype)


def _route(t, wr_hi, wr_lo):
    t_hi = t.astype(BF16)
    t_lo = (t - t_hi.astype(F32)).astype(BF16)
    logits = jnp.dot(t_hi, wr_hi, preferred_element_type=F32) + (
        jnp.dot(t_lo, wr_hi, preferred_element_type=F32) + jnp.dot(t_hi, wr_lo, preferred_element_type=F32))
    lane = lax.broadcasted_iota(jnp.int32, logits.shape, 1)
    neg = jnp.float32(-jnp.inf)
    logits = jnp.where(lane < N_EXPERTS, logits, neg)
    m1 = jnp.max(logits, axis=-1, keepdims=True)
    i1 = jnp.min(jnp.where(logits == m1, lane, LANES), axis=-1, keepdims=True)
    rest = jnp.where(lane == i1, neg, logits)
    m2 = jnp.max(rest, axis=-1, keepdims=True)
    i2 = jnp.min(jnp.where(rest == m2, lane, LANES), axis=-1, keepdims=True)
    e2 = jnp.exp(m2 - m1)
    w1 = 1.0 / (1.0 + e2)
    w2 = e2 / (1.0 + e2)
    rec = jnp.where(lane == 0, i1.astype(F32), jnp.where(lane == 1, i2.astype(F32), jnp.where(lane == 2, w1, w2)))
    return jnp.where(lane < 4, rec, 0.0)


def _final_norm_kernel(x_ref, g_ref, o_ref):
    o_ref[...] = _rms(x_ref[...], g_ref[...])


def _mm_kernel(x_ref, w_ref, *rest, segments, row_chunks):
    o_ref = rest[-1]
    j = pl.program_id(1)
    tm = x_ref.shape[0]
    rows = tm // row_chunks
    for j_lo, j_hi, epilogue, extra_idx in segments:
        @pl.when((j >= j_lo) & (j < j_hi))
        def _():
            for c in range(row_chunks):
                rs = slice(c * rows, (c + 1) * rows)
                acc = jnp.dot(x_ref[rs, :], w_ref[...], preferred_element_type=F32)
                extras = [rest[e][rs, :] if rest[e].shape[0] == tm else rest[e][...] for e in extra_idx]
                o_ref[rs, :] = epilogue(acc, *extras).astype(o_ref.dtype)


def _rope(y, cos, sin):
    return y * cos + pltpu.roll(y, HEAD_DIM // 2, 1) * sin


def _ep_plain(acc):
    return acc


def _ep_sigmoid(acc):
    return jax.nn.sigmoid(acc)


def _ep_head_norm_rope(acc, g, cos, sin):
    outs = []
    for h in range(acc.shape[1] // HEAD_DIM):
        y = _rms(acc[:, h * HEAD_DIM:(h + 1) * HEAD_DIM], g)
        outs.append(_rope(y, cos, sin))
    return jnp.concatenate(outs, axis=1)


def _ep_rope(acc, cos, sin, *, scale):
    outs = []
    for h in range(acc.shape[1] // RET_DK):
        y = _rope(acc[:, h * RET_DK:(h + 1) * RET_DK], cos, sin)
        outs.append(y * scale if scale != 1.0 else y)
    return jnp.concatenate(outs, axis=1)


def proj(h, w, layer, col0, segments, out_dtype, extras, out_block, *, tm, tn, row_chunks, name):
    m, k = h.shape
    assert col0 % tn == 0 and m % tm == 0 and tm % (16 * row_chunks) == 0
    jb = col0 // tn
    segs, j0 = [], 0
    for width, epilogue, extra_idx in segments:
        assert width % tn == 0
        segs.append((j0, j0 + width // tn, epilogue, tuple(extra_idx)))
        j0 += width // tn
    in_specs = [
        pl.BlockSpec((tm, k), lambda i, j: (i, 0)),
        pl.BlockSpec((None, k, tn), lambda i, j: (layer, 0, jb + j)),
    ]
    args = [h, w]
    for arr, bs, im in extras:
        in_specs.append(pl.BlockSpec(bs, im))
        args.append(arr)
    return pl.pallas_call(
        functools.partial(_mm_kernel, segments=tuple(segs), row_chunks=row_chunks),
        out_shape=jax.ShapeDtypeStruct((m, j0 * tn), out_dtype),
        grid=(m // tm, j0),
        in_specs=in_specs,
        out_specs=pl.BlockSpec((tm, tn), lambda i, j: (i, out_block(j))),
        compiler_params=_cparams("parallel", "parallel"),
        name=name,
    )(*args)


def _attn_kernel(q_ref, k_ref, v_ref, o_ref, v1_ref, *, seq, n_lat_tiles, with_ctx):
    qi = pl.program_id(2)
    c = HEAD_DIM ** -0.5 * np.log2(np.e)

    @pl.when(qi == 0)
    def _():
        v1_ref[:, :HEAD_DIM] = v_ref[...]
        v1_ref[:, HEAD_DIM:] = jnp.ones(v_ref.shape, v1_ref.dtype)

    def run(k, v1):
        rows = q_ref.shape[0] // ATTN_ROW_CHUNKS
        for g in range(KV_GROUP):
            for r in range(ATTN_ROW_CHUNKS):
                rs = slice(r * rows, (r + 1) * rows)
                q = q_ref[rs, g * HEAD_DIM:(g + 1) * HEAD_DIM]
                s = lax.dot_general(q, k, (((1,), (1,)), ((), ())), preferred_element_type=F32)
                m = jnp.max(s, axis=-1, keepdims=True)
                p = jnp.exp2((s - m) * c).astype(BF16)
                o = jnp.dot(p, v1, preferred_element_type=F32)
                o_ref[rs, g * HEAD_DIM:(g + 1) * HEAD_DIM] = (o[:, :HEAD_DIM] / o[:, HEAD_DIM:]).astype(o_ref.dtype)

    @pl.when(qi < n_lat_tiles)
    def _():
        run(k_ref[...], v1_ref[...])

    @pl.when(qi >= n_lat_tiles)
    def _():
        if with_ctx:
            run(k_ref[seq:, :], v1_ref[seq:, :])
        else:
            o_ref[...] = jnp.zeros_like(o_ref)


def attention(qkv, q_col, k_col, v_col, *, seq, with_ctx):
    b, nt, _ = qkv.shape
    n_lat = seq // Q_TILE
    n_tiles = nt // Q_TILE
    gw = KV_GROUP * HEAD_DIM
    return pl.pallas_call(
        functools.partial(_attn_kernel, seq=seq, n_lat_tiles=n_lat, with_ctx=with_ctx),
        out_shape=jax.ShapeDtypeStruct((b, nt, ATT_WIDTH), BF16),
        grid=(b, N_KV_HEADS, n_tiles),
        in_specs=[
            pl.BlockSpec((None, Q_TILE, gw), lambda bi, h, i: (bi, i, q_col // gw + h)),
            pl.BlockSpec((None, nt, HEAD_DIM), lambda bi, h, i: (bi, 0, k_col // HEAD_DIM + h)),
            pl.BlockSpec((None, nt, HEAD_DIM), lambda bi, h, i: (bi, 0, v_col // HEAD_DIM + h)),
        ],
        out_specs=pl.BlockSpec((None, Q_TILE, gw), lambda bi, h, i: (bi, i, h)),
        scratch_shapes=[pltpu.VMEM((nt, 2 * HEAD_DIM), BF16)],
        compiler_params=_cparams("parallel", "parallel", "arbitrary"),
        name="attention",
    )(qkv, qkv, qkv)


def _retention_kernel(*refs, has_prev):
    if has_prev:
        q_ref, k_ref, v_ref, g_ref, dec_ref, xi_ref, zeta_ref, cd_ref, prev_ref, o_ref, r_ref = refs
    else:
        q_ref, k_ref, v_ref, g_ref, dec_ref, xi_ref, zeta_ref, cd_ref, o_ref, r_ref = refs

    @pl.when(pl.program_id(1) == 0)
    def _():
        r_ref[...] = jnp.zeros_like(r_ref)

    for bi in range(q_ref.shape[0]):
        for h in range(RET_HEADS):
            ks = slice(h * RET_DK, (h + 1) * RET_DK)
            vs = slice(h * RET_DV, (h + 1) * RET_DV)
            q = q_ref[bi, :, ks]
            k = k_ref[bi, :, ks]
            v = v_ref[bi, :, vs]
            state = r_ref[bi, h]
            inner = lax.dot_general(q, k, (((1,), (1,)), ((), ())), preferred_element_type=F32) * dec_ref[h]
            o = jnp.dot(inner.astype(BF16), v, preferred_element_type=F32)
            o = o + jnp.dot(q, state.astype(BF16), preferred_element_type=F32) * xi_ref[h]
            kz = (k.astype(F32) * zeta_ref[h]).astype(BF16)
            r_ref[bi, h] = state * cd_ref[h] + lax.dot_general(kz, v, (((0,), (0,)), ((), ())), preferred_element_type=F32)
            mu = jnp.mean(o, axis=-1, keepdims=True)
            d = o - mu
            var = jnp.mean(d * d, axis=-1, keepdims=True)
            y = _silu(g_ref[bi, :, vs]) * (d * lax.rsqrt(var + GN_EPS))
            if has_prev:
                y = prev_ref[bi, :, vs] + y
            o_ref[bi, :, vs] = y.astype(o_ref.dtype)


def retention_dir(qkv, q_col, k_col, v_col, rg, gate_col, tables, chunk_of, prev, out_dtype, *, name):
    b, nt, _ = qkv.shape
    nc = nt // RET_CHUNK
    L = RET_CHUNK
    bb = next(n for n in (4, 2, 1) if b % n == 0)
    tok = lambda w, col=0: pl.BlockSpec((bb, L, w), lambda bi, t: (bi, chunk_of(t), col // w))
    full = lambda a: pl.BlockSpec(a.shape, lambda bi, t: (0,) * a.ndim)
    assert q_col % RET_QK_WIDTH == 0 and k_col % RET_QK_WIDTH == 0 and v_col % RET_V_WIDTH == 0
    in_specs = [tok(RET_QK_WIDTH, q_col), tok(RET_QK_WIDTH, k_col), tok(RET_V_WIDTH, v_col),
                tok(RET_V_WIDTH, gate_col * RET_V_WIDTH)]
    in_specs += [full(a) for a in tables]
    args = [qkv, qkv, qkv, rg, *tables]
    if prev is not None:
        in_specs.append(tok(RET_V_WIDTH))
        args.append(prev)
    return pl.pallas_call(
        functools.partial(_retention_kernel, has_prev=prev is not None),
        out_shape=jax.ShapeDtypeStruct((b, nt, RET_V_WIDTH), out_dtype),
        grid=(b // bb, nc),
        in_specs=in_specs,
        out_specs=tok(RET_V_WIDTH),
        scratch_shapes=[pltpu.VMEM((bb, RET_HEADS, RET_DK, RET_DV), F32)],
        compiler_params=_cparams("parallel", "arbitrary"),
        name=name,
    )(*args)


def retention_tables(decay_logit, backward):
    L = RET_CHUNK
    lg = jax.nn.log_sigmoid(decay_logit.astype(F32))
    pos = jnp.arange(L, dtype=F32)
    diff = pos[:, None] - pos[None, :]
    if backward:
        diff = -diff
        xi_e = L - pos
        zeta_e = pos
    else:
        xi_e = pos + 1.0
        zeta_e = L - 1.0 - pos
    decay = jnp.where(diff >= 0, jnp.exp(jnp.maximum(diff, 0.0) * lg[:, None, None]), 0.0)
    xi = jnp.broadcast_to(jnp.exp(xi_e * lg[:, None])[:, :, None], (RET_HEADS, L, RET_DV))
    zeta = jnp.broadcast_to(jnp.exp(zeta_e * lg[:, None])[:, :, None], (RET_HEADS, L, RET_DK))
    cd = jnp.broadcast_to(jnp.exp(L * lg)[:, None, None], (RET_HEADS, 1, RET_DV))
    return decay, xi, zeta, cd


def _merge_kernel(a_ref, r_ref, wa_ref, wr_ref, ga_ref, gr_ref, o_ref):
    za = jnp.dot(a_ref[...], wa_ref[...], preferred_element_type=F32)
    zr = jnp.dot(r_ref[...], wr_ref[...], preferred_element_type=F32)
    o_ref[...] = (ga_ref[...] * za + gr_ref[...] * zr).astype(o_ref.dtype)


def branch_merge(att, ret, w_att_o, w_ret_o, gates, gate_col, *, tm, tn=512):
    m, ka = att.shape
    kr = ret.shape[1]
    d = w_att_o.shape[1]
    nj = d // tn
    return pl.pallas_call(
        _merge_kernel,
        out_shape=jax.ShapeDtypeStruct((m, d), BF16),
        grid=(m // tm, nj),
        in_specs=[
            pl.BlockSpec((tm, ka), lambda i, j: (i, 0)),
            pl.BlockSpec((tm, kr), lambda i, j: (i, 0)),
            pl.BlockSpec((ka, tn), lambda i, j: (0, j)),
            pl.BlockSpec((kr, tn), lambda i, j: (0, j)),
            pl.BlockSpec((tm, tn), lambda i, j: (i, gate_col // tn + j)),
            pl.BlockSpec((tm, tn), lambda i, j: (i, gate_col // tn + nj + j)),
        ],
        out_specs=pl.BlockSpec((tm, tn), lambda i, j: (i, j)),
        compiler_params=_cparams("parallel", "parallel"),
        name="branch_merge",
    )(att, ret, w_att_o, w_ret_o, gates, gates)


def _out_norm_kernel(*refs, tm, tpb, seq, row_chunks, route):
    if route:
        (z_ref, w_ref, x_ref, gb_ref, gc_ref, ng_ref, scb_ref, scc_ref, shb_ref, shc_ref, wrh_ref, wrl_ref,
         xo_ref, h_ref, rec_ref) = refs
    else:
        z_ref, w_ref, x_ref, gb_ref, gc_ref, ng_ref, scb_ref, scc_ref, shb_ref, shc_ref, xo_ref, h_ref = refs
    i = pl.program_id(0)
    rows = tm // row_chunks
    for c in range(row_chunks):
        rs = slice(c * rows, (c + 1) * rows)
        is_ctx = (lax.broadcasted_iota(jnp.int32, (rows, 1), 0) + ((i % tpb) * tm + c * rows)) >= seq
        y = jnp.dot(z_ref[rs, :], w_ref[...], preferred_element_type=F32)
        xn = x_ref[rs, :] + jnp.where(is_ctx, gc_ref[...], gb_ref[...]) * y
        xo_ref[rs, :] = xn
        scale = jnp.where(is_ctx, scc_ref[...], scb_ref[...])
        shift = jnp.where(is_ctx, shc_ref[...], shb_ref[...])
        t = _rms(xn, ng_ref[...]) * (1 + scale) + shift
        h_ref[rs, :] = t.astype(h_ref.dtype)
        if route:
            rec_ref[rs, :] = _route(t, wrh_ref[...], wrl_ref[...])


def out_proj_norm(z, w_out, x, gate, norm_g, scale, shift, router_w, *, tm, tpb, seq, ctx_row):
    m, k = z.shape
    d = w_out.shape[1]
    route = router_w is not None
    row_chunks = 2
    assert tm % (16 * row_chunks) == 0
    rows = lambda w: pl.BlockSpec((tm, w), lambda i: (i, 0))
    vec_b = pl.BlockSpec((None, 1, d), lambda i: (i // tpb, 0, 0))
    vec_c = pl.BlockSpec((None, 1, d), lambda i: (ctx_row, 0, 0))
    const = lambda a: pl.BlockSpec(a.shape, lambda i: (0, 0), pipeline_mode=pl.Buffered(1))
    in_specs = [rows(k), const(w_out), rows(d), vec_b, vec_c, pl.BlockSpec((1, d), lambda i: (0, 0)), vec_b, vec_c, vec_b, vec_c]
    args = [z, w_out, x, gate, gate, norm_g.reshape(1, d), scale, scale, shift, shift]
    out_shape = [jax.ShapeDtypeStruct((m, d), F32), jax.ShapeDtypeStruct((m, d), F32 if route else BF16)]
    out_specs = [rows(d), rows(d)]
    if route:
        w_hi = router_w.astype(BF16)
        w_lo = (router_w - w_hi.astype(F32)).astype(BF16)
        in_specs += [const(w_hi), const(w_lo)]
        args += [w_hi, w_lo]
        out_shape.append(jax.ShapeDtypeStruct((m, LANES), F32))
        out_specs.append(rows(LANES))
    return pl.pallas_call(
        functools.partial(_out_norm_kernel, tm=tm, tpb=tpb, seq=seq, row_chunks=row_chunks, route=route),
        out_shape=tuple(out_shape),
        grid=(m // tm,),
        in_specs=in_specs,
        out_specs=tuple(out_specs),
        compiler_params=_cparams("parallel"),
        name="out_proj_norm",
    )(*args)


def _ffn_kernel(h_ref, wg_ref, wu_ref, wd_ref, x_ref, gb_ref, gc_ref, o_ref, acc_ref, *, tm, tpb, seq):
    i, f = pl.program_id(0), pl.program_id(1)

    @pl.when(f == 0)
    def _():
        acc_ref[...] = jnp.zeros_like(acc_ref)

    h = h_ref[...]
    g = jnp.dot(h, wg_ref[...], preferred_element_type=F32)
    u = jnp.dot(h, wu_ref[...], preferred_element_type=F32)
    a = (_silu(g) * u).astype(BF16)
    acc_ref[...] += jnp.dot(a, wd_ref[...], preferred_element_type=F32)

    @pl.when(f == pl.num_programs(1) - 1)
    def _():
        gate = jnp.where(_ctx_rows(i, tm, tpb, seq), gc_ref[...], gb_ref[...])
        o_ref[...] = x_ref[...] + gate * acc_ref[...]


def ffn_residual(h, wg, wu, wd, x, gate, *, tm, tpb, seq, ctx_row, tf=512):
    m, d = h.shape
    ff = wg.shape[1]
    return pl.pallas_call(
        functools.partial(_ffn_kernel, tm=tm, tpb=tpb, seq=seq),
        out_shape=jax.ShapeDtypeStruct((m, d), F32),
        grid=(m // tm, ff // tf),
        in_specs=[
            pl.BlockSpec((tm, d), lambda i, f: (i, 0)),
            pl.BlockSpec((d, tf), lambda i, f: (0, f)),
            pl.BlockSpec((d, tf), lambda i, f: (0, f)),
            pl.BlockSpec((tf, d), lambda i, f: (f, 0)),
            pl.BlockSpec((tm, d), lambda i, f: (i, 0)),
            pl.BlockSpec((None, 1, d), lambda i, f: (i // tpb, 0, 0)),
            pl.BlockSpec((None, 1, d), lambda i, f: (ctx_row, 0, 0)),
        ],
        out_specs=pl.BlockSpec((tm, d), lambda i, f: (i, 0)),
        scratch_shapes=[pltpu.VMEM((tm, d), F32)],
        compiler_params=_cparams("parallel", "arbitrary"),
        name="ffn",
    )(h, wg, wu, wd, x, gate, gate)


def moe_plan(e1, e2, tm):
    t = e1.shape[0]
    p = 2 * t
    e = jnp.stack([e1, e2], axis=1).reshape(p)
    onehot = (e[:, None] == jnp.arange(N_EXPERTS, dtype=jnp.int32)[None, :]).astype(jnp.int32)
    csum = jnp.cumsum(onehot, axis=0)
    rank = jnp.take_along_axis(csum, e[:, None], axis=1)[:, 0] - 1
    cnt = csum[-1]
    gsz = (cnt + tm - 1) // tm * tm
    gend = jnp.cumsum(gsz)
    gstart = gend - gsz
    pos = gstart[e] + rank
    n_tiles = -(-p // tm) + N_EXPERTS
    pair_of = jnp.full((n_tiles * tm,), -1, jnp.int32).at[pos].set(jnp.arange(p, dtype=jnp.int32))
    tile_start = jnp.arange(n_tiles, dtype=jnp.int32) * tm
    tile_e = jnp.minimum(jnp.searchsorted(gend, tile_start, side="right").astype(jnp.int32), N_EXPERTS - 1)
    tile_valid = jnp.clip(gstart[tile_e] + cnt[tile_e] - tile_start, 0, tm).astype(jnp.int32)
    used = gend[-1] // tm
    tile_e = jnp.where(tile_start < gend[-1], tile_e, tile_e[jnp.maximum(used - 1, 0)])
    return pair_of, tile_e, tile_valid


def _moe_kernel(te_ref, tv_ref, src_ref, nsrc_ref, pdst_ref, h_hbm, wg_ref, wu_ref, wd_ref, pairs_hbm,
                xg_ref, xb_ref, acc_ref, ys_ref, gsem, ssem, *, tm, n_f, dump_row):
    i, f = pl.program_id(0), pl.program_id(1)
    n_i = pl.num_programs(0)
    has_rows = tv_ref[i] > 0
    slot = i % 2
    rows_per_step = tm // n_f

    def gather_copy(s, r, row):
        return pltpu.make_async_copy(h_hbm.at[pl.ds(row, 1)], xg_ref.at[s, pl.ds(r, 1)], gsem.at[s])

    def scatter_copy(r, row):
        return pltpu.make_async_copy(ys_ref.at[pl.ds(r, 1)], pairs_hbm.at[pl.ds(row, 1)], ssem)

    def wait_gather(s):
        def body(r, carry):
            gather_copy(s, r, 0).wait()
            return carry
        lax.fori_loop(0, tm, body, 0, unroll=8)

    def wait_scatter():
        def body(r, carry):
            scatter_copy(r, 0).wait()
            return carry
        lax.fori_loop(0, tm, body, 0, unroll=8)

    def start_step_dmas():
        base = f * rows_per_step
        for j in range(rows_per_step):
            r = base + j
            gather_copy(1 - slot, r, nsrc_ref[0, r]).start()
            scatter_copy(r, jnp.where(i == 0, dump_row + r, pdst_ref[0, r])).start()

    @pl.when((i == 0) & (f == 0))
    def _():
        ys_ref[...] = jnp.zeros_like(ys_ref)

        def body(r, carry):
            gather_copy(0, r, src_ref[0, r]).start()
            return carry
        lax.fori_loop(0, tm, body, 0, unroll=8)

    @pl.when(f == 0)
    def _():
        wait_gather(slot)

    @pl.when(has_rows)
    def _():
        @pl.when(f == 0)
        def _():
            xb_ref[...] = xg_ref[slot].astype(BF16)
            acc_ref[...] = jnp.zeros_like(acc_ref)

        start_step_dmas()
        x = xb_ref[...]
        g = jnp.dot(x, wg_ref[...].astype(BF16), preferred_element_type=F32)
        u = jnp.dot(x, wu_ref[...].astype(BF16), preferred_element_type=F32)
        a = (_silu(g) * u).astype(BF16)
        acc_ref[...] += jnp.dot(a, wd_ref[...].astype(BF16), preferred_element_type=F32)

    @pl.when(jnp.logical_not(has_rows))
    def _():
        start_step_dmas()

    @pl.when(f == n_f - 1)
    def _():
        wait_scatter()

        @pl.when(has_rows)
        def _():
            ys_ref[...] = acc_ref[...]

        @pl.when(i == n_i - 1)
        def _():
            wait_gather(1 - slot)


def moe_experts(h, wg, wu, wd, plan, t, src_of_token, *, tm, tf):
    pair_of, tile_e, tile_valid = plan
    n_tiles = tile_e.shape[0]
    _, d, ff = wg.shape
    nf = ff // tf
    assert tm % nf == 0
    real = pair_of >= 0
    pr = jnp.maximum(pair_of, 0)
    dump = 2 * t + jnp.arange(n_tiles * tm, dtype=jnp.int32) % tm
    row_src = jnp.where(real, src_of_token(pr // 2), 0).reshape(n_tiles, 1, tm)
    row_dst = jnp.where(real, (pr % 2) * t + pr // 2, dump).reshape(n_tiles, 1, tm)
    fe = lambda i, f, te, tv: jnp.where(tv[i] > 0, f, nf - 1)
    rows_at = lambda off: pl.BlockSpec((None, 1, tm), lambda i, f, te, tv: (jnp.clip(i + off, 0, n_tiles - 1), 0, 0),
                                       memory_space=pltpu.SMEM)
    grid_spec = pltpu.PrefetchScalarGridSpec(
        num_scalar_prefetch=2,
        grid=(n_tiles, nf),
        in_specs=[
            rows_at(0), rows_at(1), rows_at(-1),
            pl.BlockSpec(memory_space=pl.ANY),
            pl.BlockSpec((None, d, tf), lambda i, f, te, tv: (te[i], 0, fe(i, f, te, tv))),
            pl.BlockSpec((None, d, tf), lambda i, f, te, tv: (te[i], 0, fe(i, f, te, tv))),
            pl.BlockSpec((None, tf, d), lambda i, f, te, tv: (te[i], fe(i, f, te, tv), 0)),
        ],
        out_specs=pl.BlockSpec(memory_space=pl.ANY),
        scratch_shapes=[pltpu.VMEM((2, tm, d), F32), pltpu.VMEM((tm, d), BF16), pltpu.VMEM((tm, d), F32),
                        pltpu.VMEM((tm, d), F32), pltpu.SemaphoreType.DMA((2,)), pltpu.SemaphoreType.DMA(())],
    )
    return pl.pallas_call(
        functools.partial(_moe_kernel, tm=tm, n_f=nf, dump_row=2 * t),
        out_shape=jax.ShapeDtypeStruct((2 * t + tm, d), F32),
        grid_spec=grid_spec,
        compiler_params=pltpu.CompilerParams(dimension_semantics=("arbitrary", "arbitrary"),
                                             vmem_limit_bytes=VMEM_LIMIT, disable_bounds_checks=True),
        name="moe_experts",
    )(tile_e, tile_valid, row_src, row_src, row_dst, h, wg, wu, wd)


def _combine_final_kernel(p0_ref, p1_ref, rec_ref, x_ref, g_ref, fg_ref, o_ref):
    rec = rec_ref[...]
    y = rec[:, 2:3] * p0_ref[...] + rec[:, 3:4] * p1_ref[...]
    o_ref[...] = _rms(x_ref[...] + g_ref[...] * y, fg_ref[...])


def _rope_tables(ang):
    c, s = jnp.cos(ang), jnp.sin(ang)
    return jnp.concatenate([c, c], axis=-1), jnp.concatenate([-s, s], axis=-1)


def _axial_angles(seq):
    rows = seq // GRID_W
    row = jnp.repeat(jnp.arange(rows), GRID_W).astype(F32)
    col = jnp.tile(jnp.arange(GRID_W), rows).astype(F32)
    n_freq = HEAD_DIM // 4
    inv = ROPE_THETA ** (-jnp.arange(n_freq, dtype=F32) / n_freq)
    return jnp.concatenate([row[:, None] * inv, col[:, None] * inv], axis=-1)


def _linear_angles(pos, dim):
    n_freq = dim // 2
    inv = ROPE_THETA ** (-jnp.arange(n_freq, dtype=F32) / n_freq)
    return pos.astype(F32)[:, None] * inv


def kernel(x, c, ctx, c_ctx, w_ada, b_ada, norm1_g, norm2_g, w_in, q_norm_g, k_norm_g, ret_decay_f, ret_decay_b, w_att_o, w_ret_o, w_out, ffn_w_gate, ffn_w_up, ffn_w_down, moe_router, moe_w_gate, moe_w_up, moe_w_down, final_g):
    b, seq, d = x.shape
    ctx_len = ctx.shape[1]
    depth = w_ada.shape[0]
    assert ctx_len == Q_TILE and seq % Q_TILE == 0 and seq % GRID_W == 0
    nt = seq + ctx_len
    m = b * nt
    tpb_big, tpb_ffn = 4, 8
    tm_big, tm_ffn = nt // tpb_big, nt // tpb_ffn
    tm_norm, tpb_norm = tm_big, tpb_big
    ctx_row = b
    nchunks = nt // RET_CHUNK
    nc_lat = seq // RET_CHUNK

    cos_a, sin_a = _rope_tables(_axial_angles(seq))
    cos_a = jnp.concatenate([cos_a, jnp.ones((ctx_len, HEAD_DIM), F32)], axis=0)
    sin_a = jnp.concatenate([sin_a, jnp.zeros((ctx_len, HEAD_DIM), F32)], axis=0)
    pos = jnp.concatenate([ctx_len + jnp.arange(seq), jnp.arange(ctx_len)])
    cos_r, sin_r = _rope_tables(_linear_angles(pos, RET_DK))

    n_rows = -(-(b + 1) // 16) * 16
    cc = jnp.zeros((n_rows, d), F32).at[:b].set(c).at[b].set(c_ctx)
    mods = adaln_all(cc, w_ada, b_ada)
    mods = mods.reshape(depth, n_rows, N_MOD, d).transpose(0, 2, 1, 3).reshape(depth, N_MOD, n_rows, 1, d)

    xa = jnp.concatenate([x, ctx], axis=1).reshape(m, d)

    def norm_mod(xcur, g, sc, sh):
        sb = pl.BlockSpec((None, 1, d), lambda i: (i // tpb_norm, 0, 0))
        scx = pl.BlockSpec((None, 1, d), lambda i: (ctx_row, 0, 0))
        row = pl.BlockSpec((tm_norm, d), lambda i: (i, 0))
        return pl.pallas_call(
            functools.partial(_norm_mod_kernel, tm=tm_norm, tpb=tpb_norm, seq=seq),
            out_shape=jax.ShapeDtypeStruct((m, d), BF16),
            grid=(m // tm_norm,),
            in_specs=[row, pl.BlockSpec((1, d), lambda i: (0, 0)), sb, scx, sb, scx],
            out_specs=row,
            compiler_params=_cparams("parallel"),
            name="norm_mod",
        )(xcur, g.reshape(1, d), sc, sc, sh, sh)

    tok_tab = lambda t: (t, (tm_big, HEAD_DIM), lambda i, j: (i % tpb_big, 0))
    row_vec = lambda v: (v.reshape(1, -1), (1, v.shape[-1]), lambda i, j: (0, 0))
    rope_scaled = functools.partial(_ep_rope, scale=RET_DK ** -0.5)
    rope_plain = functools.partial(_ep_rope, scale=1.0)
    fwd_chunk = lambda t: (t + nc_lat) % nchunks
    bwd_chunk = lambda t: nchunks - 1 - t

    w_in_b = w_in.astype(BF16)
    for layer in range(depth):
        last = layer == depth - 1
        sh1, sc1, g1, sh2, sc2, g2 = [mods[layer, n] for n in range(N_MOD)]
        tn_in = 512 if d % 512 == 0 else 256

        h = norm_mod(xa, norm1_g[layer], sc1, sh1)
        rc = 4 if tm_big % 64 == 0 else 1
        pre = ATT_WIDTH + 2 * KV_WIDTH + 2 * RET_QK_WIDTH
        tabs = [row_vec(q_norm_g[layer]), row_vec(k_norm_g[layer]), tok_tab(cos_a), tok_tab(sin_a), tok_tab(cos_r), tok_tab(sin_r)]
        pb = proj(h, w_in_b, layer, 0,
                  [(ATT_WIDTH, _ep_head_norm_rope, (0, 2, 3)), (KV_WIDTH, _ep_head_norm_rope, (1, 2, 3)), (KV_WIDTH, _ep_plain, ()),
                   (RET_QK_WIDTH, rope_plain, (4, 5)), (RET_QK_WIDTH, rope_scaled, (4, 5)), (RET_V_WIDTH, _ep_plain, ())],
                  BF16, tabs, lambda j: jnp.where(j < pre // tn_in, j + RET_V_WIDTH // tn_in, j - pre // tn_in),
                  tm=tm_big, tn=tn_in, row_chunks=rc, name="proj_bf16")
        o_rv, o_q = 0, RET_V_WIDTH
        o_k = o_q + ATT_WIDTH
        o_v = o_k + KV_WIDTH
        o_rq = o_v + KV_WIDTH
        o_rk = o_rq + RET_QK_WIDTH
        pf = proj(h, w_in_b, layer, pre + RET_V_WIDTH, [(2 * RET_V_WIDTH, _ep_plain, ()), (2 * d, _ep_sigmoid, ())],
                  F32, [], lambda j: j, tm=tm_big, tn=tn_in, row_chunks=rc, name="proj_f32")

        pb3 = pb.reshape(b, nt, pb.shape[-1])
        pf3 = pf.reshape(b, nt, pf.shape[-1])
        att = attention(pb3, o_q, o_k, o_v, seq=seq, with_ctx=not last).reshape(m, ATT_WIDTH)
        tf_ = retention_tables(ret_decay_f[layer], backward=False)
        tb_ = retention_tables(ret_decay_b[layer], backward=True)
        yf = retention_dir(pb3, o_rq, o_rk, o_rv, pf3, 0, tf_, fwd_chunk, None, F32, name="retention_fwd")
        ret = retention_dir(pb3, o_rq, o_rk, o_rv, pf3, 1, tb_, bwd_chunk, yf, BF16, name="retention_bwd").reshape(m, RET_V_WIDTH)

        z = branch_merge(att, ret, w_att_o[layer].astype(BF16), w_ret_o[layer].astype(BF16), pf, 2 * RET_V_WIDTH, tm=tm_big, tn=tn_in)
        j = layer // 2
        out_norm = functools.partial(out_proj_norm, z, w_out[layer].astype(BF16), xa, g1, norm2_g[layer], sc2, sh2,
                                     tm=tm_ffn, tpb=tpb_ffn, seq=seq, ctx_row=ctx_row)
        if layer % 2 == 0:
            xa, h2 = out_norm(None)
            xa = ffn_residual(h2, ffn_w_gate[j].astype(BF16), ffn_w_up[j].astype(BF16), ffn_w_down[j].astype(BF16),
                              xa, g2, tm=tm_ffn, tpb=tpb_ffn, seq=seq, ctx_row=ctx_row)
        else:
            assert last, "routed-expert layers before the last layer are not supported"
            wr = jnp.zeros((d, LANES), F32).at[:, :N_EXPERTS].set(moe_router[j])
            xa, h2, rec = out_norm(wr)
            e12 = rec.reshape(b, nt, LANES)[:, :seq, :2].reshape(b * seq, 2).astype(jnp.int32)
            t_tok = b * seq
            eff = moe_w_gate.shape[-1]
            tf_moe = 512 if eff % 512 == 0 else eff
            row_step = int(np.lcm(eff // tf_moe, 16))
            tm_moe = max(784 // row_step, 1) * row_step
            plan = moe_plan(e12[:, 0], e12[:, 1], tm_moe)
            pairs = moe_experts(h2, moe_w_gate[j], moe_w_up[j], moe_w_down[j],
                                plan, t_tok, lambda tok: (tok // seq) * nt + tok % seq, tm=tm_moe, tf=tf_moe)
            tmc = Q_TILE
            spb = seq // tmc
            return pl.pallas_call(
                _combine_final_kernel,
                out_shape=jax.ShapeDtypeStruct((b, seq, d), F32),
                grid=(b, spb),
                in_specs=[
                    pl.BlockSpec((tmc, d), lambda bi, i: (bi * spb + i, 0)),
                    pl.BlockSpec((tmc, d), lambda bi, i: (t_tok // tmc + bi * spb + i, 0)),
                    pl.BlockSpec((tmc, LANES), lambda bi, i: (bi * (nt // tmc) + i, 0)),
                    pl.BlockSpec((None, tmc, d), lambda bi, i: (bi, i, 0)),
                    pl.BlockSpec((None, 1, d), lambda bi, i: (bi, 0, 0)),
                    pl.BlockSpec((1, d), lambda bi, i: (0, 0)),
                ],
                out_specs=pl.BlockSpec((None, tmc, d), lambda bi, i: (bi, i, 0)),
                compiler_params=_cparams("parallel", "parallel"),
                name="combine_final",
            )(pairs, pairs, rec, xa.reshape(b, nt, d), g2, final_g.reshape(1, d))

    tm_f = Q_TILE
    return pl.pallas_call(
        _final_norm_kernel,
        out_shape=jax.ShapeDtypeStruct((b, seq, d), F32),
        grid=(b, seq // tm_f),
        in_specs=[pl.BlockSpec((None, tm_f, d), lambda bi, i: (bi, i, 0)), pl.BlockSpec((1, d), lambda bi, i: (0, 0))],
        out_specs=pl.BlockSpec((None, tm_f, d), lambda bi, i: (bi, i, 0)),
        compiler_params=_cparams("parallel", "parallel"),
        name="final_norm",
    )(xa.reshape(b, nt, d), final_g.reshape(1, d))
```

```python
import functools

import jax
import jax.numpy as jnp
import numpy as np
from jax import lax
from jax.experimental import pallas as pl
from jax.experimental.pallas import tpu as pltpu

F32 = jnp.float32
BF16 = jnp.bfloat16

GRID_W = 64
N_HEADS = 16
N_KV_HEADS = 4
HEAD_DIM = 128
KV_GROUP = N_HEADS // N_KV_HEADS
ATT_WIDTH = N_HEADS * HEAD_DIM
KV_WIDTH = N_KV_HEADS * HEAD_DIM
ROPE_THETA = 10000.0
RET_HEADS = 8
RET_DK = 128
RET_DV = 256
RET_QK_WIDTH = RET_HEADS * RET_DK
RET_V_WIDTH = RET_HEADS * RET_DV
RET_CHUNK = 128
N_EXPERTS = 8
N_MOD = 6
EPS = 1e-6
GN_EPS = 1e-5

Q_TILE = 256
ATTN_ROW_CHUNKS = 2
LANES = 128
VMEM_LIMIT = 56 * 2**20


def _cparams(*sem):
    return pltpu.CompilerParams(dimension_semantics=sem, vmem_limit_bytes=VMEM_LIMIT)


def _silu(v):
    return v * jax.nn.sigmoid(v)


def _adaln_kernel(c_ref, w_ref, b_ref, o_ref):
    s = _silu(c_ref[...]).astype(BF16)
    o_ref[...] = jnp.dot(s, w_ref[...].astype(BF16), preferred_element_type=F32) + b_ref[...]


def adaln_all(cc, w_ada, b_ada):
    depth, d, n = w_ada.shape
    r = cc.shape[0]
    tn = 1024 if n % 1024 == 0 else n
    return pl.pallas_call(
        _adaln_kernel,
        out_shape=jax.ShapeDtypeStruct((depth, r, n), F32),
        grid=(depth, n // tn),
        in_specs=[
            pl.BlockSpec((r, d), lambda l, j: (0, 0)),
            pl.BlockSpec((None, d, tn), lambda l, j: (l, 0, j)),
            pl.BlockSpec((None, 1, tn), lambda l, j: (l, 0, j)),
        ],
        out_specs=pl.BlockSpec((None, r, tn), lambda l, j: (l, 0, j)),
        compiler_params=_cparams("parallel", "parallel"),
        name="adaln",
    )(cc, w_ada, b_ada.reshape(depth, 1, n))


def _ctx_rows(i, tm, tiles_per_batch, seq):
    rows = lax.broadcasted_iota(jnp.int32, (tm, 1), 0) + (i % tiles_per_batch) * tm
    return rows >= seq


def _rms(x, g):
    return x * lax.rsqrt(jnp.mean(x * x, axis=-1, keepdims=True) + EPS) * g


def _norm_mod_kernel(x_ref, g_ref, scb_ref, scc_ref, shb_ref, shc_ref, o_ref, *, tm, tpb, seq):
    i = pl.program_id(0)
    is_ctx = _ctx_rows(i, tm, tpb, seq)
    y = _rms(x_ref[...], g_ref[...])
    scale = jnp.where(is_ctx, scc_ref[...], scb_ref[...])
    shift = jnp.where(is_ctx, shc_ref[...], shb_ref[...])
    o_ref[...] = (y * (1 + scale) + shift).astype(o_ref.dtype)


def _route(t, wr_hi, wr_lo):
    t_hi = t.astype(BF16)
    t_lo = (t - t_hi.astype(F32)).astype(BF16)
    logits = jnp.dot(t_hi, wr_hi, preferred_element_type=F32) + (
        jnp.dot(t_lo, wr_hi, preferred_element_type=F32) + jnp.dot(t_hi, wr_lo, preferred_element_type=F32))
    lane = lax.broadcasted_iota(jnp.int32, logits.shape, 1)
    neg = jnp.float32(-jnp.inf)
    logits = jnp.where(lane < N_EXPERTS, logits, neg)
    m1 = jnp.max(logits, axis=-1, keepdims=True)
    i1 = jnp.min(jnp.where(logits == m1, lane, LANES), axis=-1, keepdims=True)
    rest = jnp.where(lane == i1, neg, logits)
    m2 = jnp.max(rest, axis=-1, keepdims=True)
    i2 = jnp.min(jnp.where(rest == m2, lane, LANES), axis=-1, keepdims=True)
    e2 = jnp.exp(m2 - m1)
    w1 = 1.0 / (1.0 + e2)
    w2 = e2 / (1.0 + e2)
    rec = jnp.where(lane == 0, i1.astype(F32), jnp.where(lane == 1, i2.astype(F32), jnp.where(lane == 2, w1, w2)))
    return jnp.where(lane < 4, rec, 0.0)


def _final_norm_kernel(x_ref, g_ref, o_ref):
    o_ref[...] = _rms(x_ref[...], g_ref[...])


def _mm_kernel(x_ref, w_ref, *rest, segments, row_chunks):
    o_ref = rest[-1]
    j = pl.program_id(1)
    tm = x_ref.shape[0]
    rows = tm // row_chunks
    for j_lo, j_hi, epilogue, extra_idx in segments:
        @pl.when((j >= j_lo) & (j < j_hi))
        def _():
            for c in range(row_chunks):
                rs = slice(c * rows, (c + 1) * rows)
                acc = jnp.dot(x_ref[rs, :], w_ref[...], preferred_element_type=F32)
                extras = [rest[e][rs, :] if rest[e].shape[0] == tm else rest[e][...] for e in extra_idx]
                o_ref[rs, :] = epilogue(acc, *extras).astype(o_ref.dtype)


def _rope(y, cos, sin):
    return y * cos + pltpu.roll(y, HEAD_DIM // 2, 1) * sin


def _ep_plain(acc):
    return acc


def _ep_sigmoid(acc):
    return jax.nn.sigmoid(acc)


def _ep_head_norm_rope(acc, g, cos, sin):
    outs = []
    for h in range(acc.shape[1] // HEAD_DIM):
        y = _rms(acc[:, h * HEAD_DIM:(h + 1) * HEAD_DIM], g)
        outs.append(_rope(y, cos, sin))
    return jnp.concatenate(outs, axis=1)


def _ep_rope(acc, cos, sin, *, scale):
    outs = []
    for h in range(acc.shape[1] // RET_DK):
        y = _rope(acc[:, h * RET_DK:(h + 1) * RET_DK], cos, sin)
        outs.append(y * scale if scale != 1.0 else y)
    return jnp.concatenate(outs, axis=1)


def proj(h, w, layer, col0, segments, out_dtype, extras, out_block, *, tm, tn, row_chunks, name):
    m, k = h.shape
    assert col0 % tn == 0 and m % tm == 0 and tm % (16 * row_chunks) == 0
    jb = col0 // tn
    segs, j0 = [], 0
    for width, epilogue, extra_idx in segments:
        assert width % tn == 0
        segs.append((j0, j0 + width // tn, epilogue, tuple(extra_idx)))
        j0 += width // tn
    in_specs = [
        pl.BlockSpec((tm, k), lambda i, j: (i, 0)),
        pl.BlockSpec((None, k, tn), lambda i, j: (layer, 0, jb + j)),
    ]
    args = [h, w]
    for arr, bs, im in extras:
        in_specs.append(pl.BlockSpec(bs, im))
        args.append(arr)
    return pl.pallas_call(
        functools.partial(_mm_kernel, segments=tuple(segs), row_chunks=row_chunks),
        out_shape=jax.ShapeDtypeStruct((m, j0 * tn), out_dtype),
        grid=(m // tm, j0),
        in_specs=in_specs,
        out_specs=pl.BlockSpec((tm, tn), lambda i, j: (i, out_block(j))),
        compiler_params=_cparams("parallel", "parallel"),
        name=name,
    )(*args)


def _attn_kernel(q_ref, k_ref, v_ref, o_ref, v1_ref, *, seq, n_lat_tiles, with_ctx):
    qi = pl.program_id(2)
    c = HEAD_DIM ** -0.5 * np.log2(np.e)

    @pl.when(qi == 0)
    def _():
        v1_ref[:, :HEAD_DIM] = v_ref[...]
        v1_ref[:, HEAD_DIM:] = jnp.ones(v_ref.shape, v1_ref.dtype)

    def run(k, v1):
        rows = q_ref.shape[0] // ATTN_ROW_CHUNKS
        for g in range(KV_GROUP):
            for r in range(ATTN_ROW_CHUNKS):
                rs = slice(r * rows, (r + 1) * rows)
                q = q_ref[rs, g * HEAD_DIM:(g + 1) * HEAD_DIM]
                s = lax.dot_general(q, k, (((1,), (1,)), ((), ())), preferred_element_type=F32)
                m = jnp.max(s, axis=-1, keepdims=True)
                p = jnp.exp2((s - m) * c).astype(BF16)
                o = jnp.dot(p, v1, preferred_element_type=F32)
                o_ref[rs, g * HEAD_DIM:(g + 1) * HEAD_DIM] = (o[:, :HEAD_DIM] / o[:, HEAD_DIM:]).astype(o_ref.dtype)

    @pl.when(qi < n_lat_tiles)
    def _():
        run(k_ref[...], v1_ref[...])

    @pl.when(qi >= n_lat_tiles)
    def _():
        if with_ctx:
            run(k_ref[seq:, :], v1_ref[seq:, :])
        else:
            o_ref[...] = jnp.zeros_like(o_ref)


def attention(qkv, q_col, k_col, v_col, *, seq, with_ctx):
    b, nt, _ = qkv.shape
    n_lat = seq // Q_TILE
    n_tiles = nt // Q_TILE
    gw = KV_GROUP * HEAD_DIM
    return pl.pallas_call(
        functools.partial(_attn_kernel, seq=seq, n_lat_tiles=n_lat, with_ctx=with_ctx),
        out_shape=jax.ShapeDtypeStruct((b, nt, ATT_WIDTH), BF16),
        grid=(b, N_KV_HEADS, n_tiles),
        in_specs=[
            pl.BlockSpec((None, Q_TILE, gw), lambda bi, h, i: (bi, i, q_col // gw + h)),
            pl.BlockSpec((None, nt, HEAD_DIM), lambda bi, h, i: (bi, 0, k_col // HEAD_DIM + h)),
            pl.BlockSpec((None, nt, HEAD_DIM), lambda bi, h, i: (bi, 0, v_col // HEAD_DIM + h)),
        ],
        out_specs=pl.BlockSpec((None, Q_TILE, gw), lambda bi, h, i: (bi, i, h)),
        scratch_shapes=[pltpu.VMEM((nt, 2 * HEAD_DIM), BF16)],
        compiler_params=_cparams("parallel", "parallel", "arbitrary"),
        name="attention",
    )(qkv, qkv, qkv)


def _retention_kernel(*refs, has_prev):
    if has_prev:
        q_ref, k_ref, v_ref, g_ref, dec_ref, xi_ref, zeta_ref, cd_ref, prev_ref, o_ref, r_ref = refs
    else:
        q_ref, k_ref, v_ref, g_ref, dec_ref, xi_ref, zeta_ref, cd_ref, o_ref, r_ref = refs

    @pl.when(pl.program_id(1) == 0)
    def _():
        r_ref[...] = jnp.zeros_like(r_ref)

    for bi in range(q_ref.shape[0]):
        for h in range(RET_HEADS):
            ks = slice(h * RET_DK, (h + 1) * RET_DK)
            vs = slice(h * RET_DV, (h + 1) * RET_DV)
            q = q_ref[bi, :, ks]
            k = k_ref[bi, :, ks]
            v = v_ref[bi, :, vs]
            state = r_ref[bi, h]
            inner = lax.dot_general(q, k, (((1,), (1,)), ((), ())), preferred_element_type=F32) * dec_ref[h]
            o = jnp.dot(inner.astype(BF16), v, preferred_element_type=F32)
            o = o + jnp.dot(q, state.astype(BF16), preferred_element_type=F32) * xi_ref[h]
            kz = (k.astype(F32) * zeta_ref[h]).astype(BF16)
            r_ref[bi, h] = state * cd_ref[h] + lax.dot_general(kz, v, (((0,), (0,)), ((), ())), preferred_element_type=F32)
            mu = jnp.mean(o, axis=-1, keepdims=True)
            d = o - mu
            var = jnp.mean(d * d, axis=-1, keepdims=True)
            y = _silu(g_ref[bi, :, vs]) * (d * lax.rsqrt(var + GN_EPS))
            if has_prev:
                y = prev_ref[bi, :, vs] + y
            o_ref[bi, :, vs] = y.astype(o_ref.dtype)


def retention_dir(qkv, q_col, k_col, v_col, rg, gate_col, tables, chunk_of, prev, out_dtype, *, name):
    b, nt, _ = qkv.shape
    nc = nt // RET_CHUNK
    L = RET_CHUNK
    bb = next(n for n in (4, 2, 1) if b % n == 0)
    tok = lambda w, col=0: pl.BlockSpec((bb, L, w), lambda bi, t: (bi, chunk_of(t), col // w))
    full = lambda a: pl.BlockSpec(a.shape, lambda bi, t: (0,) * a.ndim)
    assert q_col % RET_QK_WIDTH == 0 and k_col % RET_QK_WIDTH == 0 and v_col % RET_V_WIDTH == 0
    in_specs = [tok(RET_QK_WIDTH, q_col), tok(RET_QK_WIDTH, k_col), tok(RET_V_WIDTH, v_col),
                tok(RET_V_WIDTH, gate_col * RET_V_WIDTH)]
    in_specs += [full(a) for a in tables]
    args = [qkv, qkv, qkv, rg, *tables]
    if prev is not None:
        in_specs.append(tok(RET_V_WIDTH))
        args.append(prev)
    return pl.pallas_call(
        functools.partial(_retention_kernel, has_prev=prev is not None),
        out_shape=jax.ShapeDtypeStruct((b, nt, RET_V_WIDTH), out_dtype),
        grid=(b // bb, nc),
        in_specs=in_specs,
        out_specs=tok(RET_V_WIDTH),
        scratch_shapes=[pltpu.VMEM((bb, RET_HEADS, RET_DK, RET_DV), F32)],
        compiler_params=_cparams("parallel", "arbitrary"),
        name=name,
    )(*args)


def retention_tables(decay_logit, backward):
    L = RET_CHUNK
    lg = jax.nn.log_sigmoid(decay_logit.astype(F32))
    pos = jnp.arange(L, dtype=F32)
    diff = pos[:, None] - pos[None, :]
    if backward:
        diff = -diff
        xi_e = L - pos
        zeta_e = pos
    else:
        xi_e = pos + 1.0
        zeta_e = L - 1.0 - pos
    decay = jnp.where(diff >= 0, jnp.exp(jnp.maximum(diff, 0.0) * lg[:, None, None]), 0.0)
    xi = jnp.broadcast_to(jnp.exp(xi_e * lg[:, None])[:, :, None], (RET_HEADS, L, RET_DV))
    zeta = jnp.broadcast_to(jnp.exp(zeta_e * lg[:, None])[:, :, None], (RET_HEADS, L, RET_DK))
    cd = jnp.broadcast_to(jnp.exp(L * lg)[:, None, None], (RET_HEADS, 1, RET_DV))
    return decay, xi, zeta, cd


def _merge_kernel(a_ref, r_ref, wa_ref, wr_ref, ga_ref, gr_ref, o_ref):
    za = jnp.dot(a_ref[...], wa_ref[...], preferred_element_type=F32)
    zr = jnp.dot(r_ref[...], wr_ref[...], preferred_element_type=F32)
    o_ref[...] = (ga_ref[...] * za + gr_ref[...] * zr).astype(o_ref.dtype)


def branch_merge(att, ret, w_att_o, w_ret_o, gates, gate_col, *, tm, tn=512):
    m, ka = att.shape
    kr = ret.shape[1]
    d = w_att_o.shape[1]
    nj = d // tn
    return pl.pallas_call(
        _merge_kernel,
        out_shape=jax.ShapeDtypeStruct((m, d), BF16),
        grid=(m // tm, nj),
        in_specs=[
            pl.BlockSpec((tm, ka), lambda i, j: (i, 0)),
            pl.BlockSpec((tm, kr), lambda i, j: (i, 0)),
            pl.BlockSpec((ka, tn), lambda i, j: (0, j)),
            pl.BlockSpec((kr, tn), lambda i, j: (0, j)),
            pl.BlockSpec((tm, tn), lambda i, j: (i, gate_col // tn + j)),
            pl.BlockSpec((tm, tn), lambda i, j: (i, gate_col // tn + nj + j)),
        ],
        out_specs=pl.BlockSpec((tm, tn), lambda i, j: (i, j)),
        compiler_params=_cparams("parallel", "parallel"),
        name="branch_merge",
    )(att, ret, w_att_o, w_ret_o, gates, gates)


def _out_norm_kernel(*refs, tm, tpb, seq, row_chunks, route):
    if route:
        (z_ref, w_ref, x_ref, gb_ref, gc_ref, ng_ref, scb_ref, scc_ref, shb_ref, shc_ref, wrh_ref, wrl_ref,
         xo_ref, h_ref, rec_ref) = refs
    else:
        z_ref, w_ref, x_ref, gb_ref, gc_ref, ng_ref, scb_ref, scc_ref, shb_ref, shc_ref, xo_ref, h_ref = refs
    i = pl.program_id(0)
    rows = tm // row_chunks
    for c in range(row_chunks):
        rs = slice(c * rows, (c + 1) * rows)
        is_ctx = (lax.broadcasted_iota(jnp.int32, (rows, 1), 0) + ((i % tpb) * tm + c * rows)) >= seq
        y = jnp.dot(z_ref[rs, :], w_ref[...], preferred_element_type=F32)
        xn = x_ref[rs, :] + jnp.where(is_ctx, gc_ref[...], gb_ref[...]) * y
        xo_ref[rs, :] = xn
        scale = jnp.where(is_ctx, scc_ref[...], scb_ref[...])
        shift = jnp.where(is_ctx, shc_ref[...], shb_ref[...])
        t = _rms(xn, ng_ref[...]) * (1 + scale) + shift
        h_ref[rs, :] = t.astype(h_ref.dtype)
        if route:
            rec_ref[rs, :] = _route(t, wrh_ref[...], wrl_ref[...])


def out_proj_norm(z, w_out, x, gate, norm_g, scale, shift, router_w, *, tm, tpb, seq, ctx_row):
    m, k = z.shape
    d = w_out.shape[1]
    route = router_w is not None
    row_chunks = 2
    assert tm % (16 * row_chunks) == 0
    rows = lambda w: pl.BlockSpec((tm, w), lambda i: (i, 0))
    vec_b = pl.BlockSpec((None, 1, d), lambda i: (i // tpb, 0, 0))
    vec_c = pl.BlockSpec((None, 1, d), lambda i: (ctx_row, 0, 0))
    const = lambda a: pl.BlockSpec(a.shape, lambda i: (0, 0), pipeline_mode=pl.Buffered(1))
    in_specs = [rows(k), const(w_out), rows(d), vec_b, vec_c, pl.BlockSpec((1, d), lambda i: (0, 0)), vec_b, vec_c, vec_b, vec_c]
    args = [z, w_out, x, gate, gate, norm_g.reshape(1, d), scale, scale, shift, shift]
    out_shape = [jax.ShapeDtypeStruct((m, d), F32), jax.ShapeDtypeStruct((m, d), F32 if route else BF16)]
    out_specs = [rows(d), rows(d)]
    if route:
        w_hi = router_w.astype(BF16)
        w_lo = (router_w - w_hi.astype(F32)).astype(BF16)
        in_specs += [const(w_hi), const(w_lo)]
        args += [w_hi, w_lo]
        out_shape.append(jax.ShapeDtypeStruct((m, LANES), F32))
        out_specs.append(rows(LANES))
    return pl.pallas_call(
        functools.partial(_out_norm_kernel, tm=tm, tpb=tpb, seq=seq, row_chunks=row_chunks, route=route),
        out_shape=tuple(out_shape),
        grid=(m // tm,),
        in_specs=in_specs,
        out_specs=tuple(out_specs),
        compiler_params=_cparams("parallel"),
        name="out_proj_norm",
    )(*args)


def _ffn_kernel(h_ref, wg_ref, wu_ref, wd_ref, x_ref, gb_ref, gc_ref, o_ref, acc_ref, *, tm, tpb, seq):
    i, f = pl.program_id(0), pl.program_id(1)

    @pl.when(f == 0)
    def _():
        acc_ref[...] = jnp.zeros_like(acc_ref)

    h = h_ref[...]
    g = jnp.dot(h, wg_ref[...], preferred_element_type=F32)
    u = jnp.dot(h, wu_ref[...], preferred_element_type=F32)
    a = (_silu(g) * u).astype(BF16)
    acc_ref[...] += jnp.dot(a, wd_ref[...], preferred_element_type=F32)

    @pl.when(f == pl.num_programs(1) - 1)
    def _():
        gate = jnp.where(_ctx_rows(i, tm, tpb, seq), gc_ref[...], gb_ref[...])
        o_ref[...] = x_ref[...] + gate * acc_ref[...]


def ffn_residual(h, wg, wu, wd, x, gate, *, tm, tpb, seq, ctx_row, tf=512):
    m, d = h.shape
    ff = wg.shape[1]
    return pl.pallas_call(
        functools.partial(_ffn_kernel, tm=tm, tpb=tpb, seq=seq),
        out_shape=jax.ShapeDtypeStruct((m, d), F32),
        grid=(m // tm, ff // tf),
        in_specs=[
            pl.BlockSpec((tm, d), lambda i, f: (i, 0)),
            pl.BlockSpec((d, tf), lambda i, f: (0, f)),
            pl.BlockSpec((d, tf), lambda i, f: (0, f)),
            pl.BlockSpec((tf, d), lambda i, f: (f, 0)),
            pl.BlockSpec((tm, d), lambda i, f: (i, 0)),
            pl.BlockSpec((None, 1, d), lambda i, f: (i // tpb, 0, 0)),
            pl.BlockSpec((None, 1, d), lambda i, f: (ctx_row, 0, 0)),
        ],
        out_specs=pl.BlockSpec((tm, d), lambda i, f: (i, 0)),
        scratch_shapes=[pltpu.VMEM((tm, d), F32)],
        compiler_params=_cparams("parallel", "arbitrary"),
        name="ffn",
    )(h, wg, wu, wd, x, gate, gate)


def moe_plan(e1, e2, tm):
    t = e1.shape[0]
    p = 2 * t
    e = jnp.stack([e1, e2], axis=1).reshape(p)
    onehot = (e[:, None] == jnp.arange(N_EXPERTS, dtype=jnp.int32)[None, :]).astype(jnp.int32)
    csum = jnp.cumsum(onehot, axis=0)
    rank = jnp.take_along_axis(csum, e[:, None], axis=1)[:, 0] - 1
    cnt = csum[-1]
    gsz = (cnt + tm - 1) // tm * tm
    gend = jnp.cumsum(gsz)
    gstart = gend - gsz
    pos = gstart[e] + rank
    n_tiles = -(-p // tm) + N_EXPERTS
    pair_of = jnp.full((n_tiles * tm,), -1, jnp.int32).at[pos].set(jnp.arange(p, dtype=jnp.int32))
    tile_start = jnp.arange(n_tiles, dtype=jnp.int32) * tm
    tile_e = jnp.minimum(jnp.searchsorted(gend, tile_start, side="right").astype(jnp.int32), N_EXPERTS - 1)
    tile_valid = jnp.clip(gstart[tile_e] + cnt[tile_e] - tile_start, 0, tm).astype(jnp.int32)
    used = gend[-1] // tm
    tile_e = jnp.where(tile_start < gend[-1], tile_e, tile_e[jnp.maximum(used - 1, 0)])
    return pair_of, tile_e, tile_valid


def _moe_kernel(te_ref, tv_ref, src_ref, nsrc_ref, pdst_ref, h_hbm, wg_ref, wu_ref, wd_ref, pairs_hbm,
                xg_ref, xb_ref, acc_ref, ys_ref, gsem, ssem, *, tm, n_f, dump_row):
    i, f = pl.program_id(0), pl.program_id(1)
    n_i = pl.num_programs(0)
    has_rows = tv_ref[i] > 0
    slot = i % 2
    rows_per_step = tm // n_f

    def gather_copy(s, r, row):
        return pltpu.make_async_copy(h_hbm.at[pl.ds(row, 1)], xg_ref.at[s, pl.ds(r, 1)], gsem.at[s])

    def scatter_copy(r, row):
        return pltpu.make_async_copy(ys_ref.at[pl.ds(r, 1)], pairs_hbm.at[pl.ds(row, 1)], ssem)

    def wait_gather(s):
        def body(r, carry):
            gather_copy(s, r, 0).wait()
            return carry
        lax.fori_loop(0, tm, body, 0, unroll=8)

    def wait_scatter():
        def body(r, carry):
            scatter_copy(r, 0).wait()
            return carry
        lax.fori_loop(0, tm, body, 0, unroll=8)

    def start_step_dmas():
        base = f * rows_per_step
        for j in range(rows_per_step):
            r = base + j
            gather_copy(1 - slot, r, nsrc_ref[0, r]).start()
            scatter_copy(r, jnp.where(i == 0, dump_row + r, pdst_ref[0, r])).start()

    @pl.when((i == 0) & (f == 0))
    def _():
        ys_ref[...] = jnp.zeros_like(ys_ref)

        def body(r, carry):
            gather_copy(0, r, src_ref[0, r]).start()
            return carry
        lax.fori_loop(0, tm, body, 0, unroll=8)

    @pl.when(f == 0)
    def _():
        wait_gather(slot)

    @pl.when(has_rows)
    def _():
        @pl.when(f == 0)
        def _():
            xb_ref[...] = xg_ref[slot].astype(BF16)
            acc_ref[...] = jnp.zeros_like(acc_ref)

        start_step_dmas()
        x = xb_ref[...]
        g = jnp.dot(x, wg_ref[...].astype(BF16), preferred_element_type=F32)
        u = jnp.dot(x, wu_ref[...].astype(BF16), preferred_element_type=F32)
        a = (_silu(g) * u).astype(BF16)
        acc_ref[...] += jnp.dot(a, wd_ref[...].astype(BF16), preferred_element_type=F32)

    @pl.when(jnp.logical_not(has_rows))
    def _():
        start_step_dmas()

    @pl.when(f == n_f - 1)
    def _():
        wait_scatter()

        @pl.when(has_rows)
        def _():
            ys_ref[...] = acc_ref[...]

        @pl.when(i == n_i - 1)
        def _():
            wait_gather(1 - slot)


def moe_experts(h, wg, wu, wd, plan, t, src_of_token, *, tm, tf):
    pair_of, tile_e, tile_valid = plan
    n_tiles = tile_e.shape[0]
    _, d, ff = wg.shape
    nf = ff // tf
    assert tm % nf == 0
    real = pair_of >= 0
    pr = jnp.maximum(pair_of, 0)
    dump = 2 * t + jnp.arange(n_tiles * tm, dtype=jnp.int32) % tm
    row_src = jnp.where(real, src_of_token(pr // 2), 0).reshape(n_tiles, 1, tm)
    row_dst = jnp.where(real, (pr % 2) * t + pr // 2, dump).reshape(n_tiles, 1, tm)
    fe = lambda i, f, te, tv: jnp.where(tv[i] > 0, f, nf - 1)
    rows_at = lambda off: pl.BlockSpec((None, 1, tm), lambda i, f, te, tv: (jnp.clip(i + off, 0, n_tiles - 1), 0, 0),
                                       memory_space=pltpu.SMEM)
    grid_spec = pltpu.PrefetchScalarGridSpec(
        num_scalar_prefetch=2,
        grid=(n_tiles, nf),
        in_specs=[
            rows_at(0), rows_at(1), rows_at(-1),
            pl.BlockSpec(memory_space=pl.ANY),
            pl.BlockSpec((None, d, tf), lambda i, f, te, tv: (te[i], 0, fe(i, f, te, tv))),
            pl.BlockSpec((None, d, tf), lambda i, f, te, tv: (te[i], 0, fe(i, f, te, tv))),
            pl.BlockSpec((None, tf, d), lambda i, f, te, tv: (te[i], fe(i, f, te, tv), 0)),
        ],
        out_specs=pl.BlockSpec(memory_space=pl.ANY),
        scratch_shapes=[pltpu.VMEM((2, tm, d), F32), pltpu.VMEM((tm, d), BF16), pltpu.VMEM((tm, d), F32),
                        pltpu.VMEM((tm, d), F32), pltpu.SemaphoreType.DMA((2,)), pltpu.SemaphoreType.DMA(())],
    )
    return pl.pallas_call(
        functools.partial(_moe_kernel, tm=tm, n_f=nf, dump_row=2 * t),
        out_shape=jax.ShapeDtypeStruct((2 * t + tm, d), F32),
        grid_spec=grid_spec,
        compiler_params=pltpu.CompilerParams(dimension_semantics=("arbitrary", "arbitrary"),
                                             vmem_limit_bytes=VMEM_LIMIT, disable_bounds_checks=True),
        name="moe_experts",
    )(tile_e, tile_valid, row_src, row_src, row_dst, h, wg, wu, wd)


def _combine_final_kernel(p0_ref, p1_ref, rec_ref, x_ref, g_ref, fg_ref, o_ref):
    rec = rec_ref[...]
    y = rec[:, 2:3] * p0_ref[...] + rec[:, 3:4] * p1_ref[...]
    o_ref[...] = _rms(x_ref[...] + g_ref[...] * y, fg_ref[...])


def _rope_tables(ang):
    c, s = jnp.cos(ang), jnp.sin(ang)
    return jnp.concatenate([c, c], axis=-1), jnp.concatenate([-s, s], axis=-1)


def _axial_angles(seq):
    rows = seq // GRID_W
    row = jnp.repeat(jnp.arange(rows), GRID_W).astype(F32)
    col = jnp.tile(jnp.arange(GRID_W), rows).astype(F32)
    n_freq = HEAD_DIM // 4
    inv = ROPE_THETA ** (-jnp.arange(n_freq, dtype=F32) / n_freq)
    return jnp.concatenate([row[:, None] * inv, col[:, None] * inv], axis=-1)


def _linear_angles(pos, dim):
    n_freq = dim // 2
    inv = ROPE_THETA ** (-jnp.arange(n_freq, dtype=F32) / n_freq)
    return pos.astype(F32)[:, None] * inv


def kernel(x, c, ctx, c_ctx, w_ada, b_ada, norm1_g, norm2_g, w_in, q_norm_g, k_norm_g, ret_decay_f, ret_decay_b, w_att_o, w_ret_o, w_out, ffn_w_gate, ffn_w_up, ffn_w_down, moe_router, moe_w_gate, moe_w_up, moe_w_down, final_g):
    b, seq, d = x.shape
    ctx_len = ctx.shape[1]
    depth = w_ada.shape[0]
    assert ctx_len == Q_TILE and seq % Q_TILE == 0 and seq % GRID_W == 0
    nt = seq + ctx_len
    m = b * nt
    tpb_big, tpb_ffn = 4, 8
    tm_big, tm_ffn = nt // tpb_big, nt // tpb_ffn
    tm_norm, tpb_norm = tm_big, tpb_big
    ctx_row = b
    nchunks = nt // RET_CHUNK
    nc_lat = seq // RET_CHUNK

    cos_a, sin_a = _rope_tables(_axial_angles(seq))
    cos_a = jnp.concatenate([cos_a, jnp.ones((ctx_len, HEAD_DIM), F32)], axis=0)
    sin_a = jnp.concatenate([sin_a, jnp.zeros((ctx_len, HEAD_DIM), F32)], axis=0)
    pos = jnp.concatenate([ctx_len + jnp.arange(seq), jnp.arange(ctx_len)])
    cos_r, sin_r = _rope_tables(_linear_angles(pos, RET_DK))

    n_rows = -(-(b + 1) // 16) * 16
    cc = jnp.zeros((n_rows, d), F32).at[:b].set(c).at[b].set(c_ctx)
    mods = adaln_all(cc, w_ada, b_ada)
    mods = mods.reshape(depth, n_rows, N_MOD, d).transpose(0, 2, 1, 3).reshape(depth, N_MOD, n_rows, 1, d)

    xa = jnp.concatenate([x, ctx], axis=1).reshape(m, d)

    def norm_mod(xcur, g, sc, sh):
        sb = pl.BlockSpec((None, 1, d), lambda i: (i // tpb_norm, 0, 0))
        scx = pl.BlockSpec((None, 1, d), lambda i: (ctx_row, 0, 0))
        row = pl.BlockSpec((tm_norm, d), lambda i: (i, 0))
        return pl.pallas_call(
            functools.partial(_norm_mod_kernel, tm=tm_norm, tpb=tpb_norm, seq=seq),
            out_shape=jax.ShapeDtypeStruct((m, d), BF16),
            grid=(m // tm_norm,),
            in_specs=[row, pl.BlockSpec((1, d), lambda i: (0, 0)), sb, scx, sb, scx],
            out_specs=row,
            compiler_params=_cparams("parallel"),
            name="norm_mod",
        )(xcur, g.reshape(1, d), sc, sc, sh, sh)

    tok_tab = lambda t: (t, (tm_big, HEAD_DIM), lambda i, j: (i % tpb_big, 0))
    row_vec = lambda v: (v.reshape(1, -1), (1, v.shape[-1]), lambda i, j: (0, 0))
    rope_scaled = functools.partial(_ep_rope, scale=RET_DK ** -0.5)
    rope_plain = functools.partial(_ep_rope, scale=1.0)
    fwd_chunk = lambda t: (t + nc_lat) % nchunks
    bwd_chunk = lambda t: nchunks - 1 - t

    w_in_b = w_in.astype(BF16)
    for layer in range(depth):
        last = layer == depth - 1
        sh1, sc1, g1, sh2, sc2, g2 = [mods[layer, n] for n in range(N_MOD)]
        tn_in = 512 if d % 512 == 0 else 256

        h = norm_mod(xa, norm1_g[layer], sc1, sh1)
        rc = 4 if tm_big % 64 == 0 else 1
        pre = ATT_WIDTH + 2 * KV_WIDTH + 2 * RET_QK_WIDTH
        tabs = [row_vec(q_norm_g[layer]), row_vec(k_norm_g[layer]), tok_tab(cos_a), tok_tab(sin_a), tok_tab(cos_r), tok_tab(sin_r)]
        pb = proj(h, w_in_b, layer, 0,
                  [(ATT_WIDTH, _ep_head_norm_rope, (0, 2, 3)), (KV_WIDTH, _ep_head_norm_rope, (1, 2, 3)), (KV_WIDTH, _ep_plain, ()),
                   (RET_QK_WIDTH, rope_plain, (4, 5)), (RET_QK_WIDTH, rope_scaled, (4, 5)), (RET_V_WIDTH, _ep_plain, ())],
                  BF16, tabs, lambda j: jnp.where(j < pre // tn_in, j + RET_V_WIDTH // tn_in, j - pre // tn_in),
                  tm=tm_big, tn=tn_in, row_chunks=rc, name="proj_bf16")
        o_rv, o_q = 0, RET_V_WIDTH
        o_k = o_q + ATT_WIDTH
        o_v = o_k + KV_WIDTH
        o_rq = o_v + KV_WIDTH
        o_rk = o_rq + RET_QK_WIDTH
        pf = proj(h, w_in_b, layer, pre + RET_V_WIDTH, [(2 * RET_V_WIDTH, _ep_plain, ()), (2 * d, _ep_sigmoid, ())],
                  F32, [], lambda j: j, tm=tm_big, tn=2 * tn_in, row_chunks=rc, name="proj_f32")

        pb3 = pb.reshape(b, nt, pb.shape[-1])
        pf3 = pf.reshape(b, nt, pf.shape[-1])
        att = attention(pb3, o_q, o_k, o_v, seq=seq, with_ctx=not last).reshape(m, ATT_WIDTH)
        tf_ = retention_tables(ret_decay_f[layer], backward=False)
        tb_ = retention_tables(ret_decay_b[layer], backward=True)
        yf = retention_dir(pb3, o_rq, o_rk, o_rv, pf3, 0, tf_, fwd_chunk, None, F32, name="retention_fwd")
        ret = retention_dir(pb3, o_rq, o_rk, o_rv, pf3, 1, tb_, bwd_chunk, yf, BF16, name="retention_bwd").reshape(m, RET_V_WIDTH)

        z = branch_merge(att, ret, w_att_o[layer].astype(BF16), w_ret_o[layer].astype(BF16), pf, 2 * RET_V_WIDTH, tm=tm_big, tn=tn_in)
        j = layer // 2
        out_norm = functools.partial(out_proj_norm, z, w_out[layer].astype(BF16), xa, g1, norm2_g[layer], sc2, sh2,
                                     tm=tm_ffn, tpb=tpb_ffn, seq=seq, ctx_row=ctx_row)
        if layer % 2 == 0:
            xa, h2 = out_norm(None)
            xa = ffn_residual(h2, ffn_w_gate[j].astype(BF16), ffn_w_up[j].astype(BF16), ffn_w_down[j].astype(BF16),
                              xa, g2, tm=tm_ffn, tpb=tpb_ffn, seq=seq, ctx_row=ctx_row)
        else:
            assert last, "routed-expert layers before the last layer are not supported"
            wr = jnp.zeros((d, LANES), F32).at[:, :N_EXPERTS].set(moe_router[j])
            xa, h2, rec = out_norm(wr)
            e12 = rec.reshape(b, nt, LANES)[:, :seq, :2].reshape(b * seq, 2).astype(jnp.int32)
            t_tok = b * seq
            eff = moe_w_gate.shape[-1]
            tf_moe = 512 if eff % 512 == 0 else eff
            row_step = int(np.lcm(eff // tf_moe, 16))
            tm_moe = max(784 // row_step, 1) * row_step
            plan = moe_plan(e12[:, 0], e12[:, 1], tm_moe)
            pairs = moe_experts(h2, moe_w_gate[j], moe_w_up[j], moe_w_down[j],
                                plan, t_tok, lambda tok: (tok // seq) * nt + tok % seq, tm=tm_moe, tf=tf_moe)
            tmc = Q_TILE
            spb = seq // tmc
            return pl.pallas_call(
                _combine_final_kernel,
                out_shape=jax.ShapeDtypeStruct((b, seq, d), F32),
                grid=(b, spb),
                in_specs=[
                    pl.BlockSpec((tmc, d), lambda bi, i: (bi * spb + i, 0)),
                    pl.BlockSpec((tmc, d), lambda bi, i: (t_tok // tmc + bi * spb + i, 0)),
                    pl.BlockSpec((tmc, LANES), lambda bi, i: (bi * (nt // tmc) + i, 0)),
                    pl.BlockSpec((None, tmc, d), lambda bi, i: (bi, i, 0)),
                    pl.BlockSpec((None, 1, d), lambda bi, i: (bi, 0, 0)),
                    pl.BlockSpec((1, d), lambda bi, i: (0, 0)),
                ],
                out_specs=pl.BlockSpec((None, tmc, d), lambda bi, i: (bi, i, 0)),
                compiler_params=_cparams("parallel", "parallel"),
                name="combine_final",
            )(pairs, pairs, rec, xa.reshape(b, nt, d), g2, final_g.reshape(1, d))

    tm_f = Q_TILE
    return pl.pallas_call(
        _final_norm_kernel,
        out_shape=jax.ShapeDtypeStruct((b, seq, d), F32),
        grid=(b, seq // tm_f),
        in_specs=[pl.BlockSpec((None, tm_f, d), lambda bi, i: (bi, i, 0)), pl.BlockSpec((1, d), lambda bi, i: (0, 0))],
        out_specs=pl.BlockSpec((None, tm_f, d), lambda bi, i: (bi, i, 0)),
        compiler_params=_cparams("parallel", "parallel"),
        name="final_norm",
    )(xa.reshape(b, nt, d), final_g.reshape(1, d))
```

```python
import functools

import jax
import jax.numpy as jnp
import numpy as np
from jax import lax
from jax.experimental import pallas as pl
from jax.experimental.pallas import tpu as pltpu

F32 = jnp.float32
BF16 = jnp.bfloat16

GRID_W = 64
N_HEADS = 16
N_KV_HEADS = 4
HEAD_DIM = 128
KV_GROUP = N_HEADS // N_KV_HEADS
ATT_WIDTH = N_HEADS * HEAD_DIM
KV_WIDTH = N_KV_HEADS * HEAD_DIM
ROPE_THETA = 10000.0
RET_HEADS = 8
RET_DK = 128
RET_DV = 256
RET_QK_WIDTH = RET_HEADS * RET_DK
RET_V_WIDTH = RET_HEADS * RET_DV
RET_CHUNK = 128
N_EXPERTS = 8
N_MOD = 6
EPS = 1e-6
GN_EPS = 1e-5

Q_TILE = 256
ATTN_ROW_CHUNKS = 2
LANES = 128
VMEM_LIMIT = 56 * 2**20


def _cparams(*sem):
    return pltpu.CompilerParams(dimension_semantics=sem, vmem_limit_bytes=VMEM_LIMIT)


def _silu(v):
    return v * jax.nn.sigmoid(v)


def _adaln_kernel(c_ref, w_ref, b_ref, o_ref):
    s = _silu(c_ref[...]).astype(BF16)
    o_ref[...] = jnp.dot(s, w_ref[...].astype(BF16), preferred_element_type=F32) + b_ref[...]


def adaln_all(cc, w_ada, b_ada):
    depth, d, n = w_ada.shape
    r = cc.shape[0]
    tn = 1024 if n % 1024 == 0 else n
    return pl.pallas_call(
        _adaln_kernel,
        out_shape=jax.ShapeDtypeStruct((depth, r, n), F32),
        grid=(depth, n // tn),
        in_specs=[
            pl.BlockSpec((r, d), lambda l, j: (0, 0)),
            pl.BlockSpec((None, d, tn), lambda l, j: (l, 0, j)),
            pl.BlockSpec((None, 1, tn), lambda l, j: (l, 0, j)),
        ],
        out_specs=pl.BlockSpec((None, r, tn), lambda l, j: (l, 0, j)),
        compiler_params=_cparams("parallel", "parallel"),
        name="adaln",
    )(cc, w_ada, b_ada.reshape(depth, 1, n))


def _ctx_rows(i, tm, tiles_per_batch, seq):
    rows = lax.broadcasted_iota(jnp.int32, (tm, 1), 0) + (i % tiles_per_batch) * tm
    return rows >= seq


def _rms(x, g):
    return x * lax.rsqrt(jnp.mean(x * x, axis=-1, keepdims=True) + EPS) * g


def _norm_mod_kernel(x_ref, g_ref, scb_ref, scc_ref, shb_ref, shc_ref, o_ref, *, tm, tpb, seq):
    i = pl.program_id(0)
    is_ctx = _ctx_rows(i, tm, tpb, seq)
    y = _rms(x_ref[...], g_ref[...])
    scale = jnp.where(is_ctx, scc_ref[...], scb_ref[...])
    shift = jnp.where(is_ctx, shc_ref[...], shb_ref[...])
    o_ref[...] = (y * (1 + scale) + shift).astype(o_ref.dtype)


def _route(t, wr_hi, wr_lo):
    t_hi = t.astype(BF16)
    t_lo = (t - t_hi.astype(F32)).astype(BF16)
    logits = jnp.dot(t_hi, wr_hi, preferred_element_type=F32) + (
        jnp.dot(t_lo, wr_hi, preferred_element_type=F32) + jnp.dot(t_hi, wr_lo, preferred_element_type=F32))
    lane = lax.broadcasted_iota(jnp.int32, logits.shape, 1)
    neg = jnp.float32(-jnp.inf)
    logits = jnp.where(lane < N_EXPERTS, logits, neg)
    m1 = jnp.max(logits, axis=-1, keepdims=True)
    i1 = jnp.min(jnp.where(logits == m1, lane, LANES), axis=-1, keepdims=True)
    rest = jnp.where(lane == i1, neg, logits)
    m2 = jnp.max(rest, axis=-1, keepdims=True)
    i2 = jnp.min(jnp.where(rest == m2, lane, LANES), axis=-1, keepdims=True)
    e2 = jnp.exp(m2 - m1)
    w1 = 1.0 / (1.0 + e2)
    w2 = e2 / (1.0 + e2)
    rec = jnp.where(lane == 0, i1.astype(F32), jnp.where(lane == 1, i2.astype(F32), jnp.where(lane == 2, w1, w2)))
    return jnp.where(lane < 4, rec, 0.0)


def _final_norm_kernel(x_ref, g_ref, o_ref):
    o_ref[...] = _rms(x_ref[...], g_ref[...])


def _mm_kernel(x_ref, w_ref, *rest, segments, row_chunks):
    o_ref = rest[-1]
    j = pl.program_id(1)
    tm = x_ref.shape[0]
    rows = tm // row_chunks
    for j_lo, j_hi, epilogue, extra_idx in segments:
        @pl.when((j >= j_lo) & (j < j_hi))
        def _():
            for c in range(row_chunks):
                rs = slice(c * rows, (c + 1) * rows)
                acc = jnp.dot(x_ref[rs, :], w_ref[...], preferred_element_type=F32)
                extras = [rest[e][rs, :] if rest[e].shape[0] == tm else rest[e][...] for e in extra_idx]
                o_ref[rs, :] = epilogue(acc, *extras).astype(o_ref.dtype)


def _rope(y, cos, sin):
    return y * cos + pltpu.roll(y, HEAD_DIM // 2, 1) * sin


def _ep_plain(acc):
    return acc


def _ep_sigmoid(acc):
    return jax.nn.sigmoid(acc)


def _ep_head_norm_rope(acc, g, cos, sin):
    outs = []
    for h in range(acc.shape[1] // HEAD_DIM):
        y = _rms(acc[:, h * HEAD_DIM:(h + 1) * HEAD_DIM], g)
        outs.append(_rope(y, cos, sin))
    return jnp.concatenate(outs, axis=1)


def _ep_rope(acc, cos, sin, *, scale):
    outs = []
    for h in range(acc.shape[1] // RET_DK):
        y = _rope(acc[:, h * RET_DK:(h + 1) * RET_DK], cos, sin)
        outs.append(y * scale if scale != 1.0 else y)
    return jnp.concatenate(outs, axis=1)


def proj(h, w, layer, col0, segments, out_dtype, extras, out_block, *, tm, tn, row_chunks, name):
    m, k = h.shape
    assert col0 % tn == 0 and m % tm == 0 and tm % (16 * row_chunks) == 0
    jb = col0 // tn
    segs, j0 = [], 0
    for width, epilogue, extra_idx in segments:
        assert width % tn == 0
        segs.append((j0, j0 + width // tn, epilogue, tuple(extra_idx)))
        j0 += width // tn
    in_specs = [
        pl.BlockSpec((tm, k), lambda i, j: (i, 0)),
        pl.BlockSpec((None, k, tn), lambda i, j: (layer, 0, jb + j)),
    ]
    args = [h, w]
    for arr, bs, im in extras:
        in_specs.append(pl.BlockSpec(bs, im))
        args.append(arr)
    return pl.pallas_call(
        functools.partial(_mm_kernel, segments=tuple(segs), row_chunks=row_chunks),
        out_shape=jax.ShapeDtypeStruct((m, j0 * tn), out_dtype),
        grid=(m // tm, j0),
        in_specs=in_specs,
        out_specs=pl.BlockSpec((tm, tn), lambda i, j: (i, out_block(j))),
        compiler_params=_cparams("parallel", "parallel"),
        name=name,
    )(*args)


def _attn_kernel(q_ref, k_ref, v_ref, o_ref, v1_ref, *, seq, n_lat_tiles, with_ctx):
    qi = pl.program_id(2)
    c = HEAD_DIM ** -0.5 * np.log2(np.e)

    @pl.when(qi == 0)
    def _():
        v1_ref[:, :HEAD_DIM] = v_ref[...]
        v1_ref[:, HEAD_DIM:] = jnp.ones(v_ref.shape, v1_ref.dtype)

    def run(k, v1):
        rows = q_ref.shape[0] // ATTN_ROW_CHUNKS
        for g in range(KV_GROUP):
            for r in range(ATTN_ROW_CHUNKS):
                rs = slice(r * rows, (r + 1) * rows)
                q = q_ref[rs, g * HEAD_DIM:(g + 1) * HEAD_DIM]
                s = lax.dot_general(q, k, (((1,), (1,)), ((), ())), preferred_element_type=F32)
                m = jnp.max(s, axis=-1, keepdims=True)
                p = jnp.exp2((s - m) * c).astype(BF16)
                o = jnp.dot(p, v1, preferred_element_type=F32)
                o_ref[rs, g * HEAD_DIM:(g + 1) * HEAD_DIM] = (o[:, :HEAD_DIM] / o[:, HEAD_DIM:]).astype(o_ref.dtype)

    @pl.when(qi < n_lat_tiles)
    def _():
        run(k_ref[...], v1_ref[...])

    @pl.when(qi >= n_lat_tiles)
    def _():
        if with_ctx:
            run(k_ref[seq:, :], v1_ref[seq:, :])
        else:
            o_ref[...] = jnp.zeros_like(o_ref)


def attention(qkv, q_col, k_col, v_col, *, seq, with_ctx):
    b, nt, _ = qkv.shape
    n_lat = seq // Q_TILE
    n_tiles = nt // Q_TILE
    gw = KV_GROUP * HEAD_DIM
    return pl.pallas_call(
        functools.partial(_attn_kernel, seq=seq, n_lat_tiles=n_lat, with_ctx=with_ctx),
        out_shape=jax.ShapeDtypeStruct((b, nt, ATT_WIDTH), BF16),
        grid=(b, N_KV_HEADS, n_tiles),
        in_specs=[
            pl.BlockSpec((None, Q_TILE, gw), lambda bi, h, i: (bi, i, q_col // gw + h)),
            pl.BlockSpec((None, nt, HEAD_DIM), lambda bi, h, i: (bi, 0, k_col // HEAD_DIM + h)),
            pl.BlockSpec((None, nt, HEAD_DIM), lambda bi, h, i: (bi, 0, v_col // HEAD_DIM + h)),
        ],
        out_specs=pl.BlockSpec((None, Q_TILE, gw), lambda bi, h, i: (bi, i, h)),
        scratch_shapes=[pltpu.VMEM((nt, 2 * HEAD_DIM), BF16)],
        compiler_params=_cparams("parallel", "parallel", "arbitrary"),
        name="attention",
    )(qkv, qkv, qkv)


def _retention_kernel(*refs, has_prev):
    if has_prev:
        q_ref, k_ref, v_ref, g_ref, dec_ref, xi_ref, zeta_ref, cd_ref, prev_ref, o_ref, r_ref = refs
    else:
        q_ref, k_ref, v_ref, g_ref, dec_ref, xi_ref, zeta_ref, cd_ref, o_ref, r_ref = refs

    @pl.when(pl.program_id(1) == 0)
    def _():
        r_ref[...] = jnp.zeros_like(r_ref)

    for bi in range(q_ref.shape[0]):
        for h in range(RET_HEADS):
            ks = slice(h * RET_DK, (h + 1) * RET_DK)
            vs = slice(h * RET_DV, (h + 1) * RET_DV)
            q = q_ref[bi, :, ks]
            k = k_ref[bi, :, ks]
            v = v_ref[bi, :, vs]
            state = r_ref[bi, h]
            inner = lax.dot_general(q, k, (((1,), (1,)), ((), ())), preferred_element_type=F32) * dec_ref[h]
            o = jnp.dot(inner.astype(BF16), v, preferred_element_type=F32)
            o = o + jnp.dot(q, state.astype(BF16), preferred_element_type=F32) * xi_ref[h]
            kz = (k.astype(F32) * zeta_ref[h]).astype(BF16)
            r_ref[bi, h] = state * cd_ref[h] + lax.dot_general(kz, v, (((0,), (0,)), ((), ())), preferred_element_type=F32)
            mu = jnp.mean(o, axis=-1, keepdims=True)
            d = o - mu
            var = jnp.mean(d * d, axis=-1, keepdims=True)
            y = _silu(g_ref[bi, :, vs]) * (d * lax.rsqrt(var + GN_EPS))
            if has_prev:
                y = prev_ref[bi, :, vs] + y
            o_ref[bi, :, vs] = y.astype(o_ref.dtype)


def retention_dir(qkv, q_col, k_col, v_col, rg, gate_col, tables, chunk_of, prev, out_dtype, *, name):
    b, nt, _ = qkv.shape
    nc = nt // RET_CHUNK
    L = RET_CHUNK
    bb = next(n for n in (4, 2, 1) if b % n == 0)
    tok = lambda w, col=0: pl.BlockSpec((bb, L, w), lambda bi, t: (bi, chunk_of(t), col // w))
    full = lambda a: pl.BlockSpec(a.shape, lambda bi, t: (0,) * a.ndim)
    assert q_col % RET_QK_WIDTH == 0 and k_col % RET_QK_WIDTH == 0 and v_col % RET_V_WIDTH == 0
    in_specs = [tok(RET_QK_WIDTH, q_col), tok(RET_QK_WIDTH, k_col), tok(RET_V_WIDTH, v_col),
                tok(RET_V_WIDTH, gate_col * RET_V_WIDTH)]
    in_specs += [full(a) for a in tables]
    args = [qkv, qkv, qkv, rg, *tables]
    if prev is not None:
        in_specs.append(tok(RET_V_WIDTH))
        args.append(prev)
    return pl.pallas_call(
        functools.partial(_retention_kernel, has_prev=prev is not None),
        out_shape=jax.ShapeDtypeStruct((b, nt, RET_V_WIDTH), out_dtype),
        grid=(b // bb, nc),
        in_specs=in_specs,
        out_specs=tok(RET_V_WIDTH),
        scratch_shapes=[pltpu.VMEM((bb, RET_HEADS, RET_DK, RET_DV), F32)],
        compiler_params=_cparams("parallel", "arbitrary"),
        name=name,
    )(*args)


def retention_tables(decay_logit, backward):
    L = RET_CHUNK
    lg = jax.nn.log_sigmoid(decay_logit.astype(F32))
    pos = jnp.arange(L, dtype=F32)
    diff = pos[:, None] - pos[None, :]
    if backward:
        diff = -diff
        xi_e = L - pos
        zeta_e = pos
    else:
        xi_e = pos + 1.0
        zeta_e = L - 1.0 - pos
    decay = jnp.where(diff >= 0, jnp.exp(jnp.maximum(diff, 0.0) * lg[:, None, None]), 0.0)
    xi = jnp.broadcast_to(jnp.exp(xi_e * lg[:, None])[:, :, None], (RET_HEADS, L, RET_DV))
    zeta = jnp.broadcast_to(jnp.exp(zeta_e * lg[:, None])[:, :, None], (RET_HEADS, L, RET_DK))
    cd = jnp.broadcast_to(jnp.exp(L * lg)[:, None, None], (RET_HEADS, 1, RET_DV))
    return decay, xi, zeta, cd


def _merge_kernel(a_ref, r_ref, wa_ref, wr_ref, ga_ref, gr_ref, o_ref):
    za = jnp.dot(a_ref[...], wa_ref[...], preferred_element_type=F32)
    zr = jnp.dot(r_ref[...], wr_ref[...], preferred_element_type=F32)
    o_ref[...] = (ga_ref[...] * za + gr_ref[...] * zr).astype(o_ref.dtype)


def branch_merge(att, ret, w_att_o, w_ret_o, gates, gate_col, *, tm, tn=512):
    m, ka = att.shape
    kr = ret.shape[1]
    d = w_att_o.shape[1]
    nj = d // tn
    return pl.pallas_call(
        _merge_kernel,
        out_shape=jax.ShapeDtypeStruct((m, d), BF16),
        grid=(m // tm, nj),
        in_specs=[
            pl.BlockSpec((tm, ka), lambda i, j: (i, 0)),
            pl.BlockSpec((tm, kr), lambda i, j: (i, 0)),
            pl.BlockSpec((ka, tn), lambda i, j: (0, j)),
            pl.BlockSpec((kr, tn), lambda i, j: (0, j)),
            pl.BlockSpec((tm, tn), lambda i, j: (i, gate_col // tn + j)),
            pl.BlockSpec((tm, tn), lambda i, j: (i, gate_col // tn + nj + j)),
        ],
        out_specs=pl.BlockSpec((tm, tn), lambda i, j: (i, j)),
        compiler_params=_cparams("parallel", "parallel"),
        name="branch_merge",
    )(att, ret, w_att_o, w_ret_o, gates, gates)


def _out_norm_kernel(*refs, tm, tpb, seq, row_chunks, route):
    if route:
        (z_ref, w_ref, x_ref, gb_ref, gc_ref, ng_ref, scb_ref, scc_ref, shb_ref, shc_ref, wrh_ref, wrl_ref,
         xo_ref, h_ref, rec_ref) = refs
    else:
        z_ref, w_ref, x_ref, gb_ref, gc_ref, ng_ref, scb_ref, scc_ref, shb_ref, shc_ref, xo_ref, h_ref = refs
    i = pl.program_id(0)
    rows = tm // row_chunks
    for c in range(row_chunks):
        rs = slice(c * rows, (c + 1) * rows)
        is_ctx = (lax.broadcasted_iota(jnp.int32, (rows, 1), 0) + ((i % tpb) * tm + c * rows)) >= seq
        y = jnp.dot(z_ref[rs, :], w_ref[...], preferred_element_type=F32)
        xn = x_ref[rs, :] + jnp.where(is_ctx, gc_ref[...], gb_ref[...]) * y
        xo_ref[rs, :] = xn
        scale = jnp.where(is_ctx, scc_ref[...], scb_ref[...])
        shift = jnp.where(is_ctx, shc_ref[...], shb_ref[...])
        t = _rms(xn, ng_ref[...]) * (1 + scale) + shift
        h_ref[rs, :] = t.astype(h_ref.dtype)
        if route:
            rec_ref[rs, :] = _route(t, wrh_ref[...], wrl_ref[...])


def out_proj_norm(z, w_out, x, gate, norm_g, scale, shift, router_w, *, tm, tpb, seq, ctx_row):
    m, k = z.shape
    d = w_out.shape[1]
    route = router_w is not None
    row_chunks = 2
    assert tm % (16 * row_chunks) == 0
    rows = lambda w: pl.BlockSpec((tm, w), lambda i: (i, 0))
    vec_b = pl.BlockSpec((None, 1, d), lambda i: (i // tpb, 0, 0))
    vec_c = pl.BlockSpec((None, 1, d), lambda i: (ctx_row, 0, 0))
    const = lambda a: pl.BlockSpec(a.shape, lambda i: (0, 0), pipeline_mode=pl.Buffered(1))
    in_specs = [rows(k), const(w_out), rows(d), vec_b, vec_c, pl.BlockSpec((1, d), lambda i: (0, 0)), vec_b, vec_c, vec_b, vec_c]
    args = [z, w_out, x, gate, gate, norm_g.reshape(1, d), scale, scale, shift, shift]
    out_shape = [jax.ShapeDtypeStruct((m, d), F32), jax.ShapeDtypeStruct((m, d), F32 if route else BF16)]
    out_specs = [rows(d), rows(d)]
    if route:
        w_hi = router_w.astype(BF16)
        w_lo = (router_w - w_hi.astype(F32)).astype(BF16)
        in_specs += [const(w_hi), const(w_lo)]
        args += [w_hi, w_lo]
        out_shape.append(jax.ShapeDtypeStruct((m, LANES), F32))
        out_specs.append(rows(LANES))
    return pl.pallas_call(
        functools.partial(_out_norm_kernel, tm=tm, tpb=tpb, seq=seq, row_chunks=row_chunks, route=route),
        out_shape=tuple(out_shape),
        grid=(m // tm,),
        in_specs=in_specs,
        out_specs=tuple(out_specs),
        compiler_params=_cparams("parallel"),
        name="out_proj_norm",
    )(*args)


def _ffn_kernel(h_ref, wg_ref, wu_ref, wd_ref, x_ref, gb_ref, gc_ref, o_ref, acc_ref, *, tm, tpb, seq):
    i, f = pl.program_id(0), pl.program_id(1)

    @pl.when(f == 0)
    def _():
        acc_ref[...] = jnp.zeros_like(acc_ref)

    h = h_ref[...]
    g = jnp.dot(h, wg_ref[...], preferred_element_type=F32)
    u = jnp.dot(h, wu_ref[...], preferred_element_type=F32)
    a = (_silu(g) * u).astype(BF16)
    acc_ref[...] += jnp.dot(a, wd_ref[...], preferred_element_type=F32)

    @pl.when(f == pl.num_programs(1) - 1)
    def _():
        gate = jnp.where(_ctx_rows(i, tm, tpb, seq), gc_ref[...], gb_ref[...])
        o_ref[...] = x_ref[...] + gate * acc_ref[...]


def ffn_residual(h, wg, wu, wd, x, gate, *, tm, tpb, seq, ctx_row, tf=512):
    m, d = h.shape
    ff = wg.shape[1]
    return pl.pallas_call(
        functools.partial(_ffn_kernel, tm=tm, tpb=tpb, seq=seq),
        out_shape=jax.ShapeDtypeStruct((m, d), F32),
        grid=(m // tm, ff // tf),
        in_specs=[
            pl.BlockSpec((tm, d), lambda i, f: (i, 0)),
            pl.BlockSpec((d, tf), lambda i, f: (0, f)),
            pl.BlockSpec((d, tf), lambda i, f: (0, f)),
            pl.BlockSpec((tf, d), lambda i, f: (f, 0)),
            pl.BlockSpec((tm, d), lambda i, f: (i, 0)),
            pl.BlockSpec((None, 1, d), lambda i, f: (i // tpb, 0, 0)),
            pl.BlockSpec((None, 1, d), lambda i, f: (ctx_row, 0, 0)),
        ],
        out_specs=pl.BlockSpec((tm, d), lambda i, f: (i, 0)),
        scratch_shapes=[pltpu.VMEM((tm, d), F32)],
        compiler_params=_cparams("parallel", "arbitrary"),
        name="ffn",
    )(h, wg, wu, wd, x, gate, gate)


def moe_plan(e1, e2, tm):
    t = e1.shape[0]
    p = 2 * t
    e = jnp.stack([e1, e2], axis=1).reshape(p)
    onehot = (e[:, None] == jnp.arange(N_EXPERTS, dtype=jnp.int32)[None, :]).astype(jnp.int32)
    csum = jnp.cumsum(onehot, axis=0)
    rank = jnp.take_along_axis(csum, e[:, None], axis=1)[:, 0] - 1
    cnt = csum[-1]
    gsz = (cnt + tm - 1) // tm * tm
    gend = jnp.cumsum(gsz)
    gstart = gend - gsz
    pos = gstart[e] + rank
    n_tiles = -(-p // tm) + N_EXPERTS
    pair_of = jnp.full((n_tiles * tm,), -1, jnp.int32).at[pos].set(jnp.arange(p, dtype=jnp.int32))
    tile_start = jnp.arange(n_tiles, dtype=jnp.int32) * tm
    tile_e = jnp.minimum(jnp.searchsorted(gend, tile_start, side="right").astype(jnp.int32), N_EXPERTS - 1)
    tile_valid = jnp.clip(gstart[tile_e] + cnt[tile_e] - tile_start, 0, tm).astype(jnp.int32)
    used = gend[-1] // tm
    tile_e = jnp.where(tile_start < gend[-1], tile_e, tile_e[jnp.maximum(used - 1, 0)])
    return pair_of, tile_e, tile_valid


def _moe_kernel(te_ref, tv_ref, src_ref, nsrc_ref, pdst_ref, h_hbm, wg_ref, wu_ref, wd_ref, pairs_hbm,
                xg_ref, xb_ref, acc_ref, ys_ref, gsem, ssem, *, tm, n_f, dump_row):
    i, f = pl.program_id(0), pl.program_id(1)
    n_i = pl.num_programs(0)
    has_rows = tv_ref[i] > 0
    slot = i % 2
    rows_per_step = tm // n_f

    def gather_copy(s, r, row):
        return pltpu.make_async_copy(h_hbm.at[pl.ds(row, 1)], xg_ref.at[s, pl.ds(r, 1)], gsem.at[s])

    def scatter_copy(r, row):
        return pltpu.make_async_copy(ys_ref.at[pl.ds(r, 1)], pairs_hbm.at[pl.ds(row, 1)], ssem)

    def wait_gather(s):
        def body(r, carry):
            gather_copy(s, r, 0).wait()
            return carry
        lax.fori_loop(0, tm, body, 0, unroll=8)

    def wait_scatter():
        def body(r, carry):
            scatter_copy(r, 0).wait()
            return carry
        lax.fori_loop(0, tm, body, 0, unroll=8)

    def start_step_dmas():
        base = f * rows_per_step
        for j in range(rows_per_step):
            r = base + j
            gather_copy(1 - slot, r, nsrc_ref[0, r]).start()
            scatter_copy(r, jnp.where(i == 0, dump_row + r, pdst_ref[0, r])).start()

    @pl.when((i == 0) & (f == 0))
    def _():
        ys_ref[...] = jnp.zeros_like(ys_ref)

        def body(r, carry):
            gather_copy(0, r, src_ref[0, r]).start()
            return carry
        lax.fori_loop(0, tm, body, 0, unroll=8)

    @pl.when(f == 0)
    def _():
        wait_gather(slot)

    @pl.when(has_rows)
    def _():
        @pl.when(f == 0)
        def _():
            xb_ref[...] = xg_ref[slot].astype(BF16)
            acc_ref[...] = jnp.zeros_like(acc_ref)

        start_step_dmas()
        x = xb_ref[...]
        g = jnp.dot(x, wg_ref[...].astype(BF16), preferred_element_type=F32)
        u = jnp.dot(x, wu_ref[...].astype(BF16), preferred_element_type=F32)
        a = (_silu(g) * u).astype(BF16)
        acc_ref[...] += jnp.dot(a, wd_ref[...].astype(BF16), preferred_element_type=F32)

    @pl.when(jnp.logical_not(has_rows))
    def _():
        start_step_dmas()

    @pl.when(f == n_f - 1)
    def _():
        wait_scatter()

        @pl.when(has_rows)
        def _():
            ys_ref[...] = acc_ref[...]

        @pl.when(i == n_i - 1)
        def _():
            wait_gather(1 - slot)


def moe_experts(h, wg, wu, wd, plan, t, src_of_token, *, tm, tf):
    pair_of, tile_e, tile_valid = plan
    n_tiles = tile_e.shape[0]
    _, d, ff = wg.shape
    nf = ff // tf
    assert tm % nf == 0
    real = pair_of >= 0
    pr = jnp.maximum(pair_of, 0)
    dump = 2 * t + jnp.arange(n_tiles * tm, dtype=jnp.int32) % tm
    row_src = jnp.where(real, src_of_token(pr // 2), 0).reshape(n_tiles, 1, tm)
    row_dst = jnp.where(real, (pr % 2) * t + pr // 2, dump).reshape(n_tiles, 1, tm)
    fe = lambda i, f, te, tv: jnp.where(tv[i] > 0, f, nf - 1)
    rows_at = lambda off: pl.BlockSpec((None, 1, tm), lambda i, f, te, tv: (jnp.clip(i + off, 0, n_tiles - 1), 0, 0),
                                       memory_space=pltpu.SMEM)
    grid_spec = pltpu.PrefetchScalarGridSpec(
        num_scalar_prefetch=2,
        grid=(n_tiles, nf),
        in_specs=[
            rows_at(0), rows_at(1), rows_at(-1),
            pl.BlockSpec(memory_space=pl.ANY),
            pl.BlockSpec((None, d, tf), lambda i, f, te, tv: (te[i], 0, fe(i, f, te, tv))),
            pl.BlockSpec((None, d, tf), lambda i, f, te, tv: (te[i], 0, fe(i, f, te, tv))),
            pl.BlockSpec((None, tf, d), lambda i, f, te, tv: (te[i], fe(i, f, te, tv), 0)),
        ],
        out_specs=pl.BlockSpec(memory_space=pl.ANY),
        scratch_shapes=[pltpu.VMEM((2, tm, d), F32), pltpu.VMEM((tm, d), BF16), pltpu.VMEM((tm, d), F32),
                        pltpu.VMEM((tm, d), F32), pltpu.SemaphoreType.DMA((2,)), pltpu.SemaphoreType.DMA(())],
    )
    return pl.pallas_call(
        functools.partial(_moe_kernel, tm=tm, n_f=nf, dump_row=2 * t),
        out_shape=jax.ShapeDtypeStruct((2 * t + tm, d), F32),
        grid_spec=grid_spec,
        compiler_params=pltpu.CompilerParams(dimension_semantics=("arbitrary", "arbitrary"),
                                             vmem_limit_bytes=VMEM_LIMIT, disable_bounds_checks=True),
        name="moe_experts",
    )(tile_e, tile_valid, row_src, row_src, row_dst, h, wg, wu, wd)


def _combine_final_kernel(p0_ref, p1_ref, rec_ref, x_ref, g_ref, fg_ref, o_ref):
    rec = rec_ref[...]
    y = rec[:, 2:3] * p0_ref[...] + rec[:, 3:4] * p1_ref[...]
    o_ref[...] = _rms(x_ref[...] + g_ref[...] * y, fg_ref[...])


def _rope_tables(ang):
    c, s = jnp.cos(ang), jnp.sin(ang)
    return jnp.concatenate([c, c], axis=-1), jnp.concatenate([-s, s], axis=-1)


def _axial_angles(seq):
    rows = seq // GRID_W
    row = jnp.repeat(jnp.arange(rows), GRID_W).astype(F32)
    col = jnp.tile(jnp.arange(GRID_W), rows).astype(F32)
    n_freq = HEAD_DIM // 4
    inv = ROPE_THETA ** (-jnp.arange(n_freq, dtype=F32) / n_freq)
    return jnp.concatenate([row[:, None] * inv, col[:, None] * inv], axis=-1)


def _linear_angles(pos, dim):
    n_freq = dim // 2
    inv = ROPE_THETA ** (-jnp.arange(n_freq, dtype=F32) / n_freq)
    return pos.astype(F32)[:, None] * inv


def kernel(x, c, ctx, c_ctx, w_ada, b_ada, norm1_g, norm2_g, w_in, q_norm_g, k_norm_g, ret_decay_f, ret_decay_b, w_att_o, w_ret_o, w_out, ffn_w_gate, ffn_w_up, ffn_w_down, moe_router, moe_w_gate, moe_w_up, moe_w_down, final_g):
    b, seq, d = x.shape
    ctx_len = ctx.shape[1]
    depth = w_ada.shape[0]
    assert ctx_len == Q_TILE and seq % Q_TILE == 0 and seq % GRID_W == 0
    nt = seq + ctx_len
    m = b * nt
    tpb_big, tpb_ffn = 4, 8
    tm_big, tm_ffn = nt // tpb_big, nt // tpb_ffn
    tm_norm, tpb_norm = tm_big, tpb_big
    ctx_row = b
    nchunks = nt // RET_CHUNK
    nc_lat = seq // RET_CHUNK

    cos_a, sin_a = _rope_tables(_axial_angles(seq))
    cos_a = jnp.concatenate([cos_a, jnp.ones((ctx_len, HEAD_DIM), F32)], axis=0)
    sin_a = jnp.concatenate([sin_a, jnp.zeros((ctx_len, HEAD_DIM), F32)], axis=0)
    pos = jnp.concatenate([ctx_len + jnp.arange(seq), jnp.arange(ctx_len)])
    cos_r, sin_r = _rope_tables(_linear_angles(pos, RET_DK))

    n_rows = -(-(b + 1) // 16) * 16
    cc = jnp.zeros((n_rows, d), F32).at[:b].set(c).at[b].set(c_ctx)
    mods = adaln_all(cc, w_ada, b_ada)
    mods = mods.reshape(depth, n_rows, N_MOD, d).transpose(0, 2, 1, 3).reshape(depth, N_MOD, n_rows, 1, d)

    xa = jnp.concatenate([x, ctx], axis=1).reshape(m, d)

    def norm_mod(xcur, g, sc, sh):
        sb = pl.BlockSpec((None, 1, d), lambda i: (i // tpb_norm, 0, 0))
        scx = pl.BlockSpec((None, 1, d), lambda i: (ctx_row, 0, 0))
        row = pl.BlockSpec((tm_norm, d), lambda i: (i, 0))
        return pl.pallas_call(
            functools.partial(_norm_mod_kernel, tm=tm_norm, tpb=tpb_norm, seq=seq),
            out_shape=jax.ShapeDtypeStruct((m, d), BF16),
            grid=(m // tm_norm,),
            in_specs=[row, pl.BlockSpec((1, d), lambda i: (0, 0)), sb, scx, sb, scx],
            out_specs=row,
            compiler_params=_cparams("parallel"),
            name="norm_mod",
        )(xcur, g.reshape(1, d), sc, sc, sh, sh)

    tok_tab = lambda t: (t, (tm_big, HEAD_DIM), lambda i, j: (i % tpb_big, 0))
    row_vec = lambda v: (v.reshape(1, -1), (1, v.shape[-1]), lambda i, j: (0, 0))
    rope_scaled = functools.partial(_ep_rope, scale=RET_DK ** -0.5)
    rope_plain = functools.partial(_ep_rope, scale=1.0)
    fwd_chunk = lambda t: (t + nc_lat) % nchunks
    bwd_chunk = lambda t: nchunks - 1 - t

    w_in_b = w_in.astype(BF16)
    for layer in range(depth):
        last = layer == depth - 1
        sh1, sc1, g1, sh2, sc2, g2 = [mods[layer, n] for n in range(N_MOD)]
        tn_in = 512 if d % 512 == 0 else 256

        h = norm_mod(xa, norm1_g[layer], sc1, sh1)
        rc = 4 if tm_big % 64 == 0 else 1
        pre = ATT_WIDTH + 2 * KV_WIDTH + 2 * RET_QK_WIDTH
        tabs = [row_vec(q_norm_g[layer]), row_vec(k_norm_g[layer]), tok_tab(cos_a), tok_tab(sin_a), tok_tab(cos_r), tok_tab(sin_r)]
        tn_b = 2 * tn_in

        def ep_kv(acc, g, cos, sin):
            return jnp.concatenate([_ep_head_norm_rope(acc[:, :KV_WIDTH], g, cos, sin), acc[:, KV_WIDTH:]], axis=1)

        assert tn_b == 2 * KV_WIDTH
        pb = proj(h, w_in_b, layer, 0,
                  [(ATT_WIDTH, _ep_head_norm_rope, (0, 2, 3)), (2 * KV_WIDTH, ep_kv, (1, 2, 3)),
                   (RET_QK_WIDTH, rope_plain, (4, 5)), (RET_QK_WIDTH, rope_scaled, (4, 5)), (RET_V_WIDTH, _ep_plain, ())],
                  BF16, tabs, lambda j: jnp.where(j < pre // tn_b, j + RET_V_WIDTH // tn_b, j - pre // tn_b),
                  tm=tm_big, tn=tn_b, row_chunks=rc, name="proj_bf16")
        o_rv, o_q = 0, RET_V_WIDTH
        o_k = o_q + ATT_WIDTH
        o_v = o_k + KV_WIDTH
        o_rq = o_v + KV_WIDTH
        o_rk = o_rq + RET_QK_WIDTH
        pf = proj(h, w_in_b, layer, pre + RET_V_WIDTH, [(2 * RET_V_WIDTH, _ep_plain, ()), (2 * d, _ep_sigmoid, ())],
                  F32, [], lambda j: j, tm=tm_big, tn=2 * tn_in, row_chunks=rc, name="proj_f32")

        pb3 = pb.reshape(b, nt, pb.shape[-1])
        pf3 = pf.reshape(b, nt, pf.shape[-1])
        att = attention(pb3, o_q, o_k, o_v, seq=seq, with_ctx=not last).reshape(m, ATT_WIDTH)
        tf_ = retention_tables(ret_decay_f[layer], backward=False)
        tb_ = retention_tables(ret_decay_b[layer], backward=True)
        yf = retention_dir(pb3, o_rq, o_rk, o_rv, pf3, 0, tf_, fwd_chunk, None, F32, name="retention_fwd")
        ret = retention_dir(pb3, o_rq, o_rk, o_rv, pf3, 1, tb_, bwd_chunk, yf, BF16, name="retention_bwd").reshape(m, RET_V_WIDTH)

        z = branch_merge(att, ret, w_att_o[layer].astype(BF16), w_ret_o[layer].astype(BF16), pf, 2 * RET_V_WIDTH, tm=tm_big, tn=tn_in)
        j = layer // 2
        out_norm = functools.partial(out_proj_norm, z, w_out[layer].astype(BF16), xa, g1, norm2_g[layer], sc2, sh2,
                                     tm=tm_ffn, tpb=tpb_ffn, seq=seq, ctx_row=ctx_row)
        if layer % 2 == 0:
            xa, h2 = out_norm(None)
            xa = ffn_residual(h2, ffn_w_gate[j].astype(BF16), ffn_w_up[j].astype(BF16), ffn_w_down[j].astype(BF16),
                              xa, g2, tm=tm_ffn, tpb=tpb_ffn, seq=seq, ctx_row=ctx_row)
        else:
            assert last, "routed-expert layers before the last layer are not supported"
            wr = jnp.zeros((d, LANES), F32).at[:, :N_EXPERTS].set(moe_router[j])
            xa, h2, rec = out_norm(wr)
            e12 = rec.reshape(b, nt, LANES)[:, :seq, :2].reshape(b * seq, 2).astype(jnp.int32)
            t_tok = b * seq
            eff = moe_w_gate.shape[-1]
            tf_moe = 512 if eff % 512 == 0 else eff
            row_step = int(np.lcm(eff // tf_moe, 16))
            tm_moe = max(784 // row_step, 1) * row_step
            plan = moe_plan(e12[:, 0], e12[:, 1], tm_moe)
            pairs = moe_experts(h2, moe_w_gate[j], moe_w_up[j], moe_w_down[j],
                                plan, t_tok, lambda tok: (tok // seq) * nt + tok % seq, tm=tm_moe, tf=tf_moe)
            tmc = Q_TILE
            spb = seq // tmc
            return pl.pallas_call(
                _combine_final_kernel,
                out_shape=jax.ShapeDtypeStruct((b, seq, d), F32),
                grid=(b, spb),
                in_specs=[
                    pl.BlockSpec((tmc, d), lambda bi, i: (bi * spb + i, 0)),
                    pl.BlockSpec((tmc, d), lambda bi, i: (t_tok // tmc + bi * spb + i, 0)),
                    pl.BlockSpec((tmc, LANES), lambda bi, i: (bi * (nt // tmc) + i, 0)),
                    pl.BlockSpec((None, tmc, d), lambda bi, i: (bi, i, 0)),
                    pl.BlockSpec((None, 1, d), lambda bi, i: (bi, 0, 0)),
                    pl.BlockSpec((1, d), lambda bi, i: (0, 0)),
                ],
                out_specs=pl.BlockSpec((None, tmc, d), lambda bi, i: (bi, i, 0)),
                compiler_params=_cparams("parallel", "parallel"),
                name="combine_final",
            )(pairs, pairs, rec, xa.reshape(b, nt, d), g2, final_g.reshape(1, d))

    tm_f = Q_TILE
    return pl.pallas_call(
        _final_norm_kernel,
        out_shape=jax.ShapeDtypeStruct((b, seq, d), F32),
        grid=(b, seq // tm_f),
        in_specs=[pl.BlockSpec((None, tm_f, d), lambda bi, i: (bi, i, 0)), pl.BlockSpec((1, d), lambda bi, i: (0, 0))],
        out_specs=pl.BlockSpec((None, tm_f, d), lambda bi, i: (bi, i, 0)),
        compiler_params=_cparams("parallel", "parallel"),
        name="final_norm",
    )(xa.reshape(b, nt, d), final_g.reshape(1, d))
```
